```python
import math
import jax
import jax.numpy as jnp
from jax import lax
import numpy as np

D_MODEL = 4096
BATCH = 4
SEQ = 2048
DEPTH = 2
DEC_BATCH = 8
DEC_SEQ = 4
PAST_LEN = 16384
PAGE_SIZE = 128

HEAD_DIM = 128
N_MIX_HEADS = D_MODEL // HEAD_DIM
MOBA_HEADS = N_MIX_HEADS // 4
MOBA_KV_HEADS = MOBA_HEADS // 2
GDN_HEADS = (3 * N_MIX_HEADS) // 8
GLA_HEADS = N_MIX_HEADS - MOBA_HEADS - GDN_HEADS
MOBA_BLOCK = 256
MOBA_TOPK = 3
MOBA_QCHUNK = 16
ROPE_THETA = 500000.0
ROPE_DIM = HEAD_DIM // 4
GDN_CONV = 4
GLA_DK = HEAD_DIM // 2
GLA_DV = HEAD_DIM
GLA_GATE_RANK = 16
GLA_TAU = 16.0
CHUNK = 64
D_FF = 11008
N_MOD = 9
EPS = 1e-6

MOBA_WIDTH = MOBA_HEADS * HEAD_DIM
MOBA_KV_WIDTH = MOBA_KV_HEADS * HEAD_DIM
GDN_WIDTH = GDN_HEADS * HEAD_DIM
GLA_KW = GLA_HEADS * GLA_DK
GLA_WIDTH = GLA_HEADS * GLA_DV
MIX_WIDTH = MOBA_WIDTH + GDN_WIDTH + GLA_WIDTH
GDN_CONV_CH = 3 * GDN_WIDTH
IN_SPLITS = (MOBA_WIDTH, MOBA_KV_WIDTH, MOBA_KV_WIDTH,
             GDN_WIDTH, GDN_WIDTH, GDN_WIDTH, GDN_WIDTH, GDN_HEADS, GDN_HEADS,
             GLA_KW, GLA_KW, GLA_WIDTH, GLA_WIDTH, GLA_GATE_RANK)
IN_WIDTH = (MOBA_WIDTH + 2 * MOBA_KV_WIDTH + 4 * GDN_WIDTH + 2 * GDN_HEADS
            + 2 * GLA_KW + 2 * GLA_WIDTH + GLA_GATE_RANK)

kernel_name = 'hybrid_moba_gdn_gla_macaron_step'


def rmsnorm(x, g):
    xf = x.astype(jnp.float32)
    y = xf * lax.rsqrt(jnp.mean(xf * xf, axis=-1, keepdims=True) + EPS)
    return (y * g.astype(jnp.float32)).astype(x.dtype)


def modulate(x, g, shift, scale):
    return rmsnorm(x, g) * (1 + scale[:, None, :]) + shift[:, None, :]


def l2norm(x):
    xf = x.astype(jnp.float32)
    return (xf * lax.rsqrt(jnp.sum(xf * xf, axis=-1, keepdims=True) + EPS)).astype(x.dtype)


def swiglu(h, wg, wu, wd):
    return (jax.nn.silu(h @ wg) * (h @ wu)) @ wd


def split_cols(a, sizes):
    return jnp.split(a, np.cumsum(sizes)[:-1].tolist(), axis=-1)


def rope_partial(x, pos):
    half = ROPE_DIM // 2
    inv_freq = ROPE_THETA ** (-jnp.arange(half, dtype=jnp.float32) / half)
    ang = pos.astype(jnp.float32)[:, None] * inv_freq[None, :]
    cos = jnp.cos(ang)[None, :, None, :]
    sin = jnp.sin(ang)[None, :, None, :]
    xr = x[..., :ROPE_DIM].astype(jnp.float32)
    x1, x2 = xr[..., :half], xr[..., half:]
    rot = jnp.concatenate([x1 * cos - x2 * sin, x2 * cos + x1 * sin], axis=-1)
    return jnp.concatenate([rot.astype(x.dtype), x[..., ROPE_DIM:]], axis=-1)


def causal_conv(x, buf, w):
    t = x.shape[1]
    xp = jnp.concatenate([buf.astype(x.dtype), x], axis=1)
    y = w[0] * xp[:, 0:t]
    for i in range(1, GDN_CONV):
        y = y + w[i] * xp[:, i:i + t]
    return jax.nn.silu(y), xp[:, t:]


def to_chunks(a, c):
    b, t = a.shape[:2]
    n = -(-t // c)
    a = jnp.pad(a.astype(jnp.float32), [(0, 0), (0, n * c - t)] + [(0, 0)] * (a.ndim - 2))
    a = a.reshape((b, n, c) + a.shape[2:])
    return jnp.moveaxis(a, (1, 3), (0, 2))


def from_chunks(o, t):
    o = jnp.moveaxis(o, (0, 2), (1, 3))
    return o.reshape((o.shape[0], o.shape[1] * o.shape[2]) + o.shape[3:])[:, :t]


def moba_attend(q, q_pos, k, v):
    b, nq_tok, h, dh = q.shape
    l, hkv = k.shape[1], k.shape[2]
    grp = h // hkv
    nb = max(-(-l // MOBA_BLOCK), MOBA_TOPK)
    kpad = nb * MOBA_BLOCK - l
    kb = jnp.pad(k, ((0, 0), (0, kpad), (0, 0), (0, 0))).reshape(b, nb, MOBA_BLOCK, hkv, dh)
    vb = jnp.pad(v, ((0, 0), (0, kpad), (0, 0), (0, 0))).reshape(b, nb, MOBA_BLOCK, hkv, dh)
    kmean = jnp.mean(kb.astype(jnp.float32), axis=2).astype(q.dtype)
    scale = dh ** -0.5
    qblk = q_pos // MOBA_BLOCK
    gate = jnp.einsum('bqkgd,bnkd->bqkgn', q.reshape(b, nq_tok, hkv, grp, dh), kmean)
    gate = gate.reshape(b, nq_tok, h, nb).astype(jnp.float32)
    gate = jnp.where(jnp.arange(nb)[None, None, None, :] < qblk[None, :, None, None], gate, -jnp.inf)
    _, sel = lax.top_k(gate, MOBA_TOPK)
    sel_ok = jnp.arange(MOBA_TOPK)[None, :] < qblk[:, None]
    qc = min(MOBA_QCHUNK, nq_tok)
    nch = -(-nq_tok // qc)
    qpad = nch * qc - nq_tok
    q_ch = jnp.pad(q, ((0, 0), (0, qpad), (0, 0), (0, 0))).reshape(b, nch, qc, h, dh).swapaxes(0, 1)
    s_ch = jnp.pad(sel, ((0, 0), (0, qpad), (0, 0), (0, 0))).reshape(b, nch, qc, h, MOBA_TOPK).swapaxes(0, 1)
    p_ch = jnp.pad(q_pos, (0, qpad), mode='edge').reshape(nch, qc)
    ok_ch = jnp.pad(sel_ok, ((0, qpad), (0, 0))).reshape(nch, qc, MOBA_TOPK)
    kbt = kb.transpose(0, 3, 1, 2, 4)
    vbt = vb.transpose(0, 3, 1, 2, 4)
    bi = jnp.arange(b)[:, None, None, None]
    hi = (jnp.arange(h) // grp)[None, None, :, None]
    blk_pos = jnp.arange(MOBA_BLOCK)

    def one_chunk(args):
        qch, sch, pch, okch = args
        ks = kbt[bi, hi, sch]
        vs = vbt[bi, hi, sch]
        own = pch // MOBA_BLOCK
        ko = kb[:, own]
        vo = vb[:, own]
        lp = jnp.einsum('bqhd,bqhknd->bqhkn', qch, ks).astype(jnp.float32) * scale
        lp = jnp.where(okch[None, :, None, :, None], lp, -jnp.inf)
        lo = jnp.einsum('bqkgd,bqnkd->bqkgn', qch.reshape(b, qc, hkv, grp, dh), ko)
        lo = lo.reshape(b, qc, h, MOBA_BLOCK).astype(jnp.float32) * scale
        kpos = own[:, None] * MOBA_BLOCK + blk_pos[None, :]
        lo = jnp.where((kpos <= pch[:, None])[None, :, None, :], lo, -jnp.inf)
        p = jax.nn.softmax(jnp.concatenate([lp.reshape(b, qc, h, MOBA_TOPK * MOBA_BLOCK), lo], axis=-1), axis=-1)
        p = p.astype(qch.dtype)
        pp = p[..., :MOBA_TOPK * MOBA_BLOCK].reshape(b, qc, h, MOBA_TOPK, MOBA_BLOCK)
        po = p[..., MOBA_TOPK * MOBA_BLOCK:].reshape(b, qc, hkv, grp, MOBA_BLOCK)
        o_own = jnp.einsum('bqkgn,bqnkd->bqkgd', po, vo).reshape(b, qc, h, dh)
        return jnp.einsum('bqhkn,bqhknd->bqhd', pp, vs) + o_own

    o = lax.map(one_chunk, (q_ch, s_ch, p_ch, ok_ch))
    return o.swapaxes(0, 1).reshape(b, nch * qc, h, dh)[:, :nq_tok]


def gated_delta_chunked(q, k, v, g, beta, s0):
    t = q.shape[1]
    c = min(CHUNK, t)
    q, k, v, g, beta = (to_chunks(a, c) for a in (q, k, v, g, beta))
    gam = jnp.cumsum(g, axis=-1)
    idx = jnp.arange(c)
    incl = idx[:, None] >= idx[None, :]
    strict = idx[:, None] > idx[None, :]
    decay = jnp.exp(jnp.where(incl, gam[..., :, None] - gam[..., None, :], -jnp.inf))
    lmat = jnp.where(strict, beta[..., :, None] * jnp.einsum('nbhid,nbhjd->nbhij', k, k) * decay, 0.0)
    eye = jnp.eye(c, dtype=jnp.float32)
    tmat = lax.linalg.triangular_solve(eye + lmat, jnp.broadcast_to(eye, lmat.shape),
                                       left_side=True, lower=True, unit_diagonal=True)
    u = tmat @ (beta[..., None] * v)
    w = tmat @ ((beta * jnp.exp(gam))[..., None] * k)
    qk = jnp.einsum('nbhid,nbhjd->nbhij', q, k) * decay

    def step(s, xs):
        qc_, kc, uc, wc, qkc, gc = xs
        v_new = uc - wc @ s
        o = (qc_ * jnp.exp(gc)[..., None]) @ s + qkc @ v_new
        gl = gc[..., -1]
        s = jnp.exp(gl)[..., None, None] * s + jnp.einsum('bhcd,bhce->bhde', kc * jnp.exp(gl[..., None] - gc)[..., None], v_new)
        return s, o

    s_fin, o = lax.scan(step, s0.astype(jnp.float32), (q, k, u, w, qk, gam))
    return from_chunks(o, t), s_fin


def gla_chunked(q, k, v, log_a, s0):
    t = q.shape[1]
    c = min(CHUNK, t)
    q, k, v, log_a = (to_chunks(a, c) for a in (q, k, v, log_a))
    bcum = jnp.cumsum(log_a, axis=-2)
    idx = jnp.arange(c)
    incl = (idx[:, None] >= idx[None, :])[:, :, None]

    def step(s, xs):
        qc_, kc, vc, bc = xs
        diff = bc[:, :, :, None, :] - bc[:, :, None, :, :]
        dec = jnp.exp(jnp.where(incl, diff, -jnp.inf))
        att = jnp.einsum('bhid,bhijd,bhjd->bhij', qc_, dec, kc)
        o = (qc_ * jnp.exp(bc)) @ s + att @ vc
        bl = bc[:, :, -1]
        s = jnp.exp(bl)[..., None] * s + jnp.einsum('bhcd,bhce->bhde', kc * jnp.exp(bl[:, :, None] - bc), vc)
        return s, o

    s_fin, o = lax.scan(step, s0.astype(jnp.float32), (q, k, v, bcum))
    return from_chunks(o, t), s_fin


def mixer(h, pos, lw, k_past, v_past, s_gdn, conv_buf, s_gla):
    b, t, _ = h.shape
    dt = h.dtype
    f32 = jnp.float32
    (q_m, k_m, v_m, q_d, k_d, v_d, z_d, a_d, b_d,
     q_l, k_l, v_l, r_l, f_l) = split_cols(h @ lw['w_in'], IN_SPLITS)
    q_m = rope_partial(q_m.reshape(b, t, MOBA_HEADS, HEAD_DIM), pos)
    k_m = rope_partial(k_m.reshape(b, t, MOBA_KV_HEADS, HEAD_DIM), pos)
    v_m = v_m.reshape(b, t, MOBA_KV_HEADS, HEAD_DIM)
    if k_past is None:
        k_all, v_all = k_m, v_m
    else:
        k_all = jnp.concatenate([k_past.astype(dt), k_m], axis=1)
        v_all = jnp.concatenate([v_past.astype(dt), v_m], axis=1)
    o_m = rmsnorm(moba_attend(q_m, pos, k_all, v_all), lw['moba_norm']).reshape(b, t, MOBA_WIDTH)
    qkv, conv_new = causal_conv(jnp.concatenate([q_d, k_d, v_d], axis=-1), conv_buf, lw['gdn_conv_w'])
    q_d, k_d, v_d = split_cols(qkv, (GDN_WIDTH, GDN_WIDTH, GDN_WIDTH))
    q_d = l2norm(q_d.reshape(b, t, GDN_HEADS, HEAD_DIM)) * (HEAD_DIM ** -0.5)
    k_d = l2norm(k_d.reshape(b, t, GDN_HEADS, HEAD_DIM))
    v_d = v_d.reshape(b, t, GDN_HEADS, HEAD_DIM)
    beta = jax.nn.sigmoid(b_d.astype(f32))
    g = -jnp.exp(lw['gdn_a_log'].astype(f32)) * jax.nn.softplus(a_d.astype(f32) + lw['gdn_dt_bias'].astype(f32))
    o_d, s_gdn_new = gated_delta_chunked(q_d, k_d, v_d, g, beta, s_gdn)
    o_d = rmsnorm(o_d.astype(dt), lw['gdn_norm']) * jax.nn.silu(z_d.reshape(b, t, GDN_HEADS, HEAD_DIM))
    o_d = o_d.reshape(b, t, GDN_WIDTH)
    log_a = jax.nn.log_sigmoid((f_l @ lw['gla_w_gate'] + lw['gla_b_gate']).astype(f32)) / GLA_TAU
    o_l, s_gla_new = gla_chunked(q_l.reshape(b, t, GLA_HEADS, GLA_DK) * (GLA_DK ** -0.5),
                                 k_l.reshape(b, t, GLA_HEADS, GLA_DK),
                                 v_l.reshape(b, t, GLA_HEADS, GLA_DV),
                                 log_a.reshape(b, t, GLA_HEADS, GLA_DK), s_gla)
    o_l = rmsnorm(o_l.astype(dt), lw['gla_norm']) * jax.nn.silu(r_l.reshape(b, t, GLA_HEADS, GLA_DV))
    o_l = o_l.reshape(b, t, GLA_WIDTH)
    y = jnp.concatenate([o_m, o_d, o_l], axis=-1) @ lw['w_out']
    return y, (k_m, v_m, s_gdn_new, conv_new, s_gla_new)


def trunk_layer(x, c, pos, lw, k_past, v_past, s_gdn, conv_buf, s_gla):
    mod = jax.nn.silu(c) @ lw['ada_w'] + lw['ada_b']
    sh1, sc1, g1, sh2, sc2, g2, sh3, sc3, g3 = jnp.split(mod, N_MOD, axis=-1)
    x = x + 0.5 * g1[:, None, :] * swiglu(modulate(x, lw['norm_ffn_a'], sh1, sc1),
                                         lw['ffn_a_wg'], lw['ffn_a_wu'], lw['ffn_a_wd'])
    y, st = mixer(modulate(x, lw['norm_mix'], sh2, sc2), pos, lw, k_past, v_past, s_gdn, conv_buf, s_gla)
    x = x + g2[:, None, :] * y
    x = x + 0.5 * g3[:, None, :] * swiglu(modulate(x, lw['norm_ffn_b'], sh3, sc3),
                                         lw['ffn_b_wg'], lw['ffn_b_wu'], lw['ffn_b_wd'])
    return x, st


def setup_inputs(seed: int = 0) -> dict:
    key = jax.random.key(seed)
    ks = iter(jax.random.split(key, 48))
    f32 = jnp.float32

    def nrm(shape, std):
        return std * jax.random.normal(next(ks), shape, f32)

    n_pages = PAST_LEN // PAGE_SIZE
    n_pool = (5 * DEC_BATCH * n_pages) // 4
    page_table = jax.random.permutation(next(ks), n_pool)[:DEC_BATCH * n_pages]
    page_table = page_table.reshape(DEC_BATCH, n_pages).astype(jnp.int32)
    dt0 = jnp.exp(jax.random.uniform(next(ks), (DEPTH, GDN_HEADS), f32, math.log(1e-3), math.log(1e-1)))
    return {
        'x_prompt': nrm((BATCH, SEQ, D_MODEL), 1.0),
        'x_sample': nrm((DEC_BATCH, DEC_SEQ, D_MODEL), 1.0),
        'c_prompt': nrm((BATCH, D_MODEL), 1.0),
        'c_sample': nrm((DEC_BATCH, D_MODEL), 1.0),
        'cache_k': nrm((DEPTH, n_pool, PAGE_SIZE, MOBA_KV_HEADS, HEAD_DIM), 1.0),
        'cache_v': nrm((DEPTH, n_pool, PAGE_SIZE, MOBA_KV_HEADS, HEAD_DIM), 1.0),
        'page_table': page_table,
        'state_gdn': nrm((DEPTH, DEC_BATCH, GDN_HEADS, HEAD_DIM, HEAD_DIM), HEAD_DIM ** -0.5),
        'state_gdn_conv': nrm((DEPTH, DEC_BATCH, GDN_CONV - 1, GDN_CONV_CH), 1.0),
        'state_gla': nrm((DEPTH, DEC_BATCH, GLA_HEADS, GLA_DK, GLA_DV), 0.5),
        'ada_w': nrm((DEPTH, D_MODEL, N_MOD * D_MODEL), 0.3 * D_MODEL ** -0.5),
        'ada_b': nrm((DEPTH, N_MOD * D_MODEL), 0.02),
        'norm_ffn_a': 1.0 + nrm((DEPTH, D_MODEL), 0.02),
        'ffn_a_wg': nrm((DEPTH, D_MODEL, D_FF), D_MODEL ** -0.5),
        'ffn_a_wu': nrm((DEPTH, D_MODEL, D_FF), D_MODEL ** -0.5),
        'ffn_a_wd': nrm((DEPTH, D_FF, D_MODEL), D_FF ** -0.5),
        'norm_mix': 1.0 + nrm((DEPTH, D_MODEL), 0.02),
        'w_in': nrm((DEPTH, D_MODEL, IN_WIDTH), D_MODEL ** -0.5),
        'moba_norm': 1.0 + nrm((DEPTH, HEAD_DIM), 0.02),
        'gdn_conv_w': nrm((DEPTH, GDN_CONV, GDN_CONV_CH), GDN_CONV ** -0.5),
        'gdn_a_log': jnp.log(jax.random.uniform(next(ks), (DEPTH, GDN_HEADS), f32, 1.0, 16.0)),
        'gdn_dt_bias': dt0 + jnp.log(-jnp.expm1(-dt0)),
        'gdn_norm': 1.0 + nrm((DEPTH, HEAD_DIM), 0.02),
        'gla_w_gate': nrm((DEPTH, GLA_GATE_RANK, GLA_KW), GLA_GATE_RANK ** -0.5),
        'gla_b_gate': nrm((DEPTH, GLA_KW), 0.1),
        'gla_norm': 1.0 + nrm((DEPTH, GLA_DV), 0.02),
        'w_out': nrm((DEPTH, MIX_WIDTH, D_MODEL), MIX_WIDTH ** -0.5),
        'norm_ffn_b': 1.0 + nrm((DEPTH, D_MODEL), 0.02),
        'ffn_b_wg': nrm((DEPTH, D_MODEL, D_FF), D_MODEL ** -0.5),
        'ffn_b_wu': nrm((DEPTH, D_MODEL, D_FF), D_MODEL ** -0.5),
        'ffn_b_wd': nrm((DEPTH, D_FF, D_MODEL), D_FF ** -0.5),
        'final_norm': 1.0 + nrm((D_MODEL,), 0.02),
    }


def reference(x_prompt, x_sample, c_prompt, c_sample, cache_k, cache_v, page_table,
              state_gdn, state_gdn_conv, state_gla, ada_w, ada_b, norm_ffn_a,
              ffn_a_wg, ffn_a_wu, ffn_a_wd, norm_mix, w_in, moba_norm, gdn_conv_w,
              gdn_a_log, gdn_dt_bias, gdn_norm, gla_w_gate, gla_b_gate, gla_norm, w_out,
              norm_ffn_b, ffn_b_wg, ffn_b_wu, ffn_b_wd, final_norm):
    bp, s_len, _ = x_prompt.shape
    bd, t_dec, _ = x_sample.shape
    n_pages = page_table.shape[1]
    past_len = n_pages * PAGE_SIZE
    pos_p = jnp.arange(s_len, dtype=jnp.int32)
    pos_s = past_len + jnp.arange(t_dec, dtype=jnp.int32)
    hp, hs = x_prompt, x_sample
    kp_l, vp_l, ks_l, vs_l = [], [], [], []
    gp_l, gs_l, cp_l, cs_l, lp_l, ls_l = [], [], [], [], [], []
    for l in range(DEPTH):
        lw = {'ada_w': ada_w[l], 'ada_b': ada_b[l], 'norm_ffn_a': norm_ffn_a[l],
              'ffn_a_wg': ffn_a_wg[l], 'ffn_a_wu': ffn_a_wu[l], 'ffn_a_wd': ffn_a_wd[l],
              'norm_mix': norm_mix[l], 'w_in': w_in[l], 'moba_norm': moba_norm[l],
              'gdn_conv_w': gdn_conv_w[l], 'gdn_a_log': gdn_a_log[l], 'gdn_dt_bias': gdn_dt_bias[l],
              'gdn_norm': gdn_norm[l], 'gla_w_gate': gla_w_gate[l], 'gla_b_gate': gla_b_gate[l],
              'gla_norm': gla_norm[l], 'w_out': w_out[l], 'norm_ffn_b': norm_ffn_b[l],
              'ffn_b_wg': ffn_b_wg[l], 'ffn_b_wu': ffn_b_wu[l], 'ffn_b_wd': ffn_b_wd[l]}
        hp, (kp, vp, gp, cp, lp) = trunk_layer(
            hp, c_prompt, pos_p, lw, None, None,
            jnp.zeros((bp, GDN_HEADS, HEAD_DIM, HEAD_DIM), jnp.float32),
            jnp.zeros((bp, GDN_CONV - 1, GDN_CONV_CH), x_prompt.dtype),
            jnp.zeros((bp, GLA_HEADS, GLA_DK, GLA_DV), jnp.float32))
        k_past = cache_k[l][page_table].reshape(bd, past_len, MOBA_KV_HEADS, HEAD_DIM)
        v_past = cache_v[l][page_table].reshape(bd, past_len, MOBA_KV_HEADS, HEAD_DIM)
        hs, (ks_, vs_, gs, cs, ls) = trunk_layer(
            hs, c_sample, pos_s, lw, k_past, v_past, state_gdn[l], state_gdn_conv[l], state_gla[l])
        kp_l.append(kp); vp_l.append(vp); ks_l.append(ks_); vs_l.append(vs_)
        gp_l.append(gp); gs_l.append(gs); cp_l.append(cp); cs_l.append(cs)
        lp_l.append(lp); ls_l.append(ls)
    y_prompt = rmsnorm(hp, final_norm)
    y_sample = rmsnorm(hs, final_norm)
    return (y_prompt, y_sample,
            jnp.stack(kp_l), jnp.stack(vp_l), jnp.stack(ks_l), jnp.stack(vs_l),
            jnp.stack(gp_l), jnp.stack(gs_l), jnp.stack(cp_l), jnp.stack(cs_l),
            jnp.stack(lp_l), jnp.stack(ls_l))
```

```python
import functools
import math

import jax
import jax.numpy as jnp
from jax import lax
from jax.experimental import pallas as pl
from jax.experimental.pallas import tpu as pltpu

f32 = jnp.float32
bf16 = jnp.bfloat16

HEAD_DIM = 128
MOBA_BLOCK = 256
MOBA_TOPK = 3
ROPE_THETA = 500000.0
ROPE_DIM = HEAD_DIM // 4
GDN_CONV = 4
GLA_DK = HEAD_DIM // 2
GLA_GATE_RANK = 16
GLA_TAU = 16.0
GLA_SUB = 16
INV_BLK = 16
CHUNK = 64
PAGE_SIZE = 128
N_MOD = 9
EPS = 1e-6

LANES = 128
SUBLANES = 8
TM = 256
SUB = 32
V7X_VMEM_BYTES = 64 * 1024 * 1024
VMEM_BUDGET = 52 * 1024 * 1024


def _cparams(sem, need_bytes):
    limit = int(min(max(need_bytes * 1.25 + (4 << 20), 16 << 20), VMEM_BUDGET))
    return pltpu.CompilerParams(dimension_semantics=sem, vmem_limit_bytes=limit)


def _pick(n, prefs):
    for p in prefs:
        if n % p == 0:
            return p
    raise ValueError(f"no tile in {prefs} divides {n}")


def _bdot(a, b):
    return jnp.dot(a.astype(bf16), b.astype(bf16), preferred_element_type=f32)


def _bdot_nt(a, b):
    return lax.dot_general(a.astype(bf16), b.astype(bf16), (((1,), (1,)), ((), ())),
                           preferred_element_type=f32)


def _bdot_tn(a, b):
    return lax.dot_general(a.astype(bf16), b.astype(bf16), (((0,), (0,)), ((), ())),
                           preferred_element_type=f32)


def _split2(a):
    hi = a.astype(bf16)
    lo = (a - hi.astype(f32)).astype(bf16)
    return hi, lo


def _split3(a):
    hi = a.astype(bf16)
    r = a - hi.astype(f32)
    mid = r.astype(bf16)
    lo = (r - mid.astype(f32)).astype(bf16)
    return hi, mid, lo


def _dot3(a, b):
    ah, al = _split2(a)
    bh, bl = _split2(b)
    d = functools.partial(jnp.dot, preferred_element_type=f32)
    return d(ah, bh) + (d(ah, bl) + d(al, bh))


def _dot3_nt(a, b):
    ah, al = _split2(a)
    bh, bl = _split2(b)
    d = functools.partial(lax.dot_general, dimension_numbers=(((1,), (1,)), ((), ())),
                          preferred_element_type=f32)
    return d(ah, bh) + (d(ah, bl) + d(al, bh))


def _sel_dot(sel, b):
    bh, bm, bl = _split3(b)
    d = functools.partial(jnp.dot, preferred_element_type=f32)
    return d(sel, bh) + (d(sel, bm) + d(sel, bl))


def _sel_dot_nt(sel, b):
    bh, bm, bl = _split3(b)
    d = functools.partial(lax.dot_general, dimension_numbers=(((1,), (1,)), ((), ())),
                          preferred_element_type=f32)
    return d(sel, bh) + (d(sel, bm) + d(sel, bl))


def _silu(x):
    return x * jax.nn.sigmoid(x)


def _softplus(x):
    return jnp.maximum(x, 0.0) + jnp.log(1.0 + jnp.exp(-jnp.abs(x)))


def _iota(shape, dim):
    return lax.broadcasted_iota(jnp.int32, shape, dim)


def _rms_heads(o, g):
    return o * lax.rsqrt(jnp.mean(o * o, axis=-1, keepdims=True) + EPS) * g


def _mod_kernel(c_ref, w_ref, b_ref, o_ref):
    c = c_ref[...]
    o_ref[...] = _bdot(_silu(c), w_ref[...]) + b_ref[...]


def _mod_call(c16, ada_w, ada_b):
    depth, d, n = ada_w.shape
    tn = _pick(n, (512, 256, 128))
    need = 2 * d * tn * 4 + d * tn * 2 + 4 * 16 * tn * 4 + 2 * 16 * d * 4
    return pl.pallas_call(
        _mod_kernel,
        grid=(depth, n // tn),
        in_specs=[pl.BlockSpec((16, d), lambda l, j: (0, 0)),
                  pl.BlockSpec((None, d, tn), lambda l, j: (l, 0, j)),
                  pl.BlockSpec((None, 1, tn), lambda l, j: (l, 0, j))],
        out_specs=pl.BlockSpec((None, 16, tn), lambda l, j: (l, 0, j)),
        out_shape=jax.ShapeDtypeStruct((depth, 16, n), f32),
        compiler_params=_cparams(("arbitrary", "arbitrary"), need),
        name="adaln_mod",
    )(c16, ada_w, ada_b.reshape(depth, 1, n))


def _premod_kernel(x_ref, g_ref, sh_ref, sc_ref, o_ref):
    g = g_ref[...]
    for s in range(TM // SUB):
        rows = slice(s * SUB, (s + 1) * SUB)
        xs = x_ref[rows, :]
        y = xs * lax.rsqrt(jnp.mean(xs * xs, axis=-1, keepdims=True) + EPS) * g
        o_ref[rows, :] = (y * (1.0 + sc_ref[s:s + 1, :]) + sh_ref[s:s + 1, :]).astype(o_ref.dtype)


def _premod_call(x, g, mod8, v_shift, v_scale):
    m, d = x.shape
    nsub = TM // SUB
    need = 2 * TM * d * 4 + 2 * TM * d * 2 + 6 * nsub * d * 4
    return pl.pallas_call(
        _premod_kernel,
        grid=(m // TM,),
        in_specs=[pl.BlockSpec((TM, d), lambda i: (i, 0)),
                  pl.BlockSpec((1, d), lambda i: (0, 0)),
                  pl.BlockSpec((nsub, d), lambda i: (i, v_shift)),
                  pl.BlockSpec((nsub, d), lambda i: (i, v_scale))],
        out_specs=pl.BlockSpec((TM, d), lambda i: (i, 0)),
        out_shape=jax.ShapeDtypeStruct((m, d), bf16),
        compiler_params=_cparams(("arbitrary",), need),
        name="modulate",
    )(x, g.reshape(1, d), mod8, mod8)


def _rms_kernel(x_ref, g_ref, o_ref):
    x = x_ref[...]
    o_ref[...] = x * lax.rsqrt(jnp.mean(x * x, axis=-1, keepdims=True) + EPS) * g_ref[...]


def _rms_call(x, g):
    m, d = x.shape
    return pl.pallas_call(
        _rms_kernel,
        grid=(m // TM,),
        in_specs=[pl.BlockSpec((TM, d), lambda i: (i, 0)), pl.BlockSpec((1, d), lambda i: (0, 0))],
        out_specs=pl.BlockSpec((TM, d), lambda i: (i, 0)),
        out_shape=jax.ShapeDtypeStruct((m, d), f32),
        compiler_params=_cparams(("arbitrary",), 4 * TM * d * 4),
        name="final_norm",
    )(x, g.reshape(1, d))


def _mm_tm(m):
    return _pick(m, (768, 512, 256))


def _mm_plain_kernel(a_ref, w_ref, o_ref, wb_ref):
    @pl.when(pl.program_id(1) == 0)
    def _():
        wb_ref[...] = w_ref[...].astype(bf16)

    o_ref[...] = jnp.dot(a_ref[...], wb_ref[...], preferred_element_type=f32).astype(o_ref.dtype)


def _mm_plain_call(a, w3, layer, n_cols, name):
    m, k = a.shape
    tm = _mm_tm(m)
    tn = _pick(n_cols, (512, 256, 128))
    need = 2 * k * tn * 4 + k * tn * 2 + 2 * tm * k * 2 + 2 * tm * tn * 4
    return pl.pallas_call(
        _mm_plain_kernel,
        grid=(n_cols // tn, m // tm),
        in_specs=[pl.BlockSpec((tm, k), lambda j, i: (i, 0)),
                  pl.BlockSpec((None, k, tn), lambda j, i: (layer, 0, j))],
        out_specs=pl.BlockSpec((tm, tn), lambda j, i: (i, j)),
        out_shape=jax.ShapeDtypeStruct((m, n_cols), f32),
        scratch_shapes=[pltpu.VMEM((k, tn), bf16)],
        compiler_params=_cparams(("arbitrary", "arbitrary"), need),
        name=name,
    )(a, w3)


def _mm_up_kernel(a_ref, wg_ref, wu_ref, o_ref, wgb_ref, wub_ref):
    @pl.when(pl.program_id(1) == 0)
    def _():
        wgb_ref[...] = wg_ref[...].astype(bf16)
        wub_ref[...] = wu_ref[...].astype(bf16)

    a = a_ref[...]
    g = jnp.dot(a, wgb_ref[...], preferred_element_type=f32)
    u = jnp.dot(a, wub_ref[...], preferred_element_type=f32)
    o_ref[...] = (_silu(g) * u).astype(o_ref.dtype)


def _mm_up_call(a, wg3, wu3, layer):
    m, k = a.shape
    f = wg3.shape[2]
    tm = _mm_tm(m)
    tn = _pick(f, (256, 128))
    need = 4 * k * tn * 4 + 2 * k * tn * 2 + 2 * tm * k * 2 + 2 * tm * tn * 2 + 3 * tm * tn * 4
    wspec = pl.BlockSpec((None, k, tn), lambda j, i: (layer, 0, j))
    return pl.pallas_call(
        _mm_up_kernel,
        grid=(f // tn, m // tm),
        in_specs=[pl.BlockSpec((tm, k), lambda j, i: (i, 0)), wspec, wspec],
        out_specs=pl.BlockSpec((tm, tn), lambda j, i: (i, j)),
        out_shape=jax.ShapeDtypeStruct((m, f), bf16),
        scratch_shapes=[pltpu.VMEM((k, tn), bf16), pltpu.VMEM((k, tn), bf16)],
        compiler_params=_cparams(("arbitrary", "arbitrary"), need),
        name="ffn_up",
    )(a, wg3, wu3)


def _mm_res_kernel(a_ref, w_ref, r_ref, gate_ref, o_ref, wb_ref, *, scale):
    @pl.when(pl.program_id(1) == 0)
    def _():
        wb_ref[...] = w_ref[...].astype(bf16)

    acc = jnp.dot(a_ref[...], wb_ref[...], preferred_element_type=f32)
    for s in range(acc.shape[0] // SUB):
        rows = slice(s * SUB, (s + 1) * SUB)
        o_ref[rows, :] = r_ref[rows, :] + (scale * gate_ref[s:s + 1, :]) * acc[rows, :]


def _mm_res_call(a, w3, layer, res, mod8, v_gate, scale, name):
    m, k = a.shape
    n = w3.shape[2]
    tn = _pick(n, (512, 256, 128))
    nk = 1
    while 2 * (k // nk) * tn * 4 > (24 << 20) and (k // nk) % (2 * LANES) == 0:
        nk *= 2
    kc = k // nk
    vmem_need = lambda t: 2 * kc * tn * 4 + kc * tn * 2 + 2 * t * kc * 2 + 5 * t * tn * 4
    tm = next(t for t in (768, 512, 256) if m % t == 0 and (vmem_need(t) <= (36 << 20) or t == 256))
    nsub = tm // SUB
    need = vmem_need(tm)
    out = res
    for kb in range(nk):
        out = pl.pallas_call(
            functools.partial(_mm_res_kernel, scale=scale),
            grid=(n // tn, m // tm),
            in_specs=[pl.BlockSpec((tm, kc), lambda j, i, kb=kb: (i, kb)),
                      pl.BlockSpec((None, kc, tn), lambda j, i, kb=kb: (layer, kb, j)),
                      pl.BlockSpec((tm, tn), lambda j, i: (i, j)),
                      pl.BlockSpec((nsub, tn), lambda j, i: (i, v_gate * (n // tn) + j))],
            out_specs=pl.BlockSpec((tm, tn), lambda j, i: (i, j)),
            out_shape=jax.ShapeDtypeStruct((m, n), f32),
            scratch_shapes=[pltpu.VMEM((kc, tn), bf16)],
            compiler_params=_cparams(("arbitrary", "arbitrary"), need),
            name=f"{name}_k{kb}",
        )(a, w3, out, mod8)
    return out


def _rope_kernel(x_ref, cos_ref, sin_ref, q_ref, k_ref):
    x = x_ref[...]
    w = x.shape[1]
    nh = w // HEAD_DIM
    cosf = jnp.concatenate([cos_ref[...]] * nh, axis=1)
    sinf = jnp.concatenate([sin_ref[...]] * nh, axis=1)
    lane = _iota(x.shape, 1) % HEAD_DIM
    half = ROPE_DIM // 2
    partner = jnp.where(lane < half, pltpu.roll(x, w - half, 1), pltpu.roll(x, half, 1))
    y = x * cosf + partner * sinf
    qw = q_ref.shape[1]
    q_ref[...] = y[:, :qw]
    k_ref[...] = y[:, qw:]


def _rope_call(proj, cos_t, sin_t, qw, kw):
    m = proj.shape[0]
    w = qw + kw
    return pl.pallas_call(
        _rope_kernel,
        grid=(m // TM,),
        in_specs=[pl.BlockSpec((TM, w), lambda i: (i, 0)),
                  pl.BlockSpec((TM, HEAD_DIM), lambda i: (i, 0)),
                  pl.BlockSpec((TM, HEAD_DIM), lambda i: (i, 0))],
        out_specs=[pl.BlockSpec((TM, qw), lambda i: (i, 0)), pl.BlockSpec((TM, kw), lambda i: (i, 0))],
        out_shape=[jax.ShapeDtypeStruct((m, qw), f32), jax.ShapeDtypeStruct((m, kw), f32)],
        compiler_params=_cparams(("arbitrary",), 10 * TM * w * 4),
        name="rope",
    )(proj, cos_t, sin_t)


def _top_blocks(gate, n_valid):
    lane = _iota(gate.shape, 1)
    gate = jnp.where(lane < n_valid, gate, -jnp.inf)
    picks = []
    for kk in range(MOBA_TOPK):
        mx = jnp.max(gate, axis=-1, keepdims=True)
        idx = jnp.min(jnp.where(gate == mx, lane, LANES), axis=-1, keepdims=True)
        picks.append(jnp.where(kk < n_valid, idx, -1))
        gate = jnp.where(lane == idx, -jnp.inf, gate)
    return picks


def _moba_prompt_kernel(q_ref, k_ref, v_ref, g_ref, o_ref):
    i = pl.program_id(2)
    s_len = k_ref.shape[0]
    nb = s_len // MOBA_BLOCK
    scale = HEAD_DIM ** -0.5
    q2 = jnp.concatenate([q_ref[:, :HEAD_DIM], q_ref[:, HEAD_DIM:]], axis=0)
    rows = q2.shape[0]
    blk_row = _iota((LANES, HEAD_DIM), 0)
    kmean = jnp.zeros((LANES, HEAD_DIM), f32)
    for n in range(nb):
        mean_n = jnp.mean(k_ref[n * MOBA_BLOCK:(n + 1) * MOBA_BLOCK, :], axis=0, keepdims=True)
        kmean = jnp.where(blk_row == n, mean_n, kmean)
    gate = _dot3_nt(q2, kmean)
    picks = _top_blocks(gate, i)
    qs = (q2 * scale).astype(bf16)

    own = pl.multiple_of(i * MOBA_BLOCK, MOBA_BLOCK)
    s = _bdot_nt(qs, k_ref[pl.ds(own, MOBA_BLOCK), :])
    rq = _iota(s.shape, 0) % MOBA_BLOCK
    ck = _iota(s.shape, 1)
    s = jnp.where(ck <= rq, s, -jnp.inf)
    m0 = jnp.max(s, axis=-1, keepdims=True)
    p = jnp.exp(s - m0)
    l0 = jnp.sum(p, axis=-1, keepdims=True)
    acc0 = _bdot(p, v_ref[pl.ds(own, MOBA_BLOCK), :])

    def body(n, carry):
        m, l, acc = carry
        start = pl.multiple_of(n * MOBA_BLOCK, MOBA_BLOCK)
        sn = _bdot_nt(qs, k_ref[pl.ds(start, MOBA_BLOCK), :])
        sel = (picks[0] == n) | (picks[1] == n) | (picks[2] == n)
        sn = jnp.where(sel, sn, -jnp.inf)
        m_new = jnp.maximum(m, jnp.max(sn, axis=-1, keepdims=True))
        alpha = jnp.exp(m - m_new)
        pn = jnp.exp(sn - m_new)
        l_new = alpha * l + jnp.sum(pn, axis=-1, keepdims=True)
        acc_new = alpha * acc + _bdot(pn, v_ref[pl.ds(start, MOBA_BLOCK), :])
        return m_new, l_new, acc_new

    m, l, acc = lax.fori_loop(0, i, body, (m0, l0, acc0))
    o = _rms_heads(acc / l, g_ref[...])
    half = rows // 2
    o_ref[:, :HEAD_DIM] = o[:half].astype(o_ref.dtype)
    o_ref[:, HEAD_DIM:] = o[half:].astype(o_ref.dtype)


def _moba_prompt_call(q_rot, k_rot, proj, gnorm, b, s_len, kvh, v_col0):
    qt = s_len // MOBA_BLOCK
    vb = v_col0 // HEAD_DIM
    need = 4 * s_len * HEAD_DIM * 4 + 4 * MOBA_BLOCK * 2 * HEAD_DIM * 4 + 16 * 2 * MOBA_BLOCK * MOBA_BLOCK * 4
    return pl.pallas_call(
        _moba_prompt_kernel,
        grid=(b, kvh, qt),
        in_specs=[pl.BlockSpec((MOBA_BLOCK, 2 * HEAD_DIM), lambda bi, h, i: (bi * qt + i, h)),
                  pl.BlockSpec((s_len, HEAD_DIM), lambda bi, h, i: (bi, h)),
                  pl.BlockSpec((s_len, HEAD_DIM), lambda bi, h, i: (bi, vb + h)),
                  pl.BlockSpec((1, HEAD_DIM), lambda bi, h, i: (0, 0))],
        out_specs=pl.BlockSpec((MOBA_BLOCK, 2 * HEAD_DIM), lambda bi, h, i: (bi * qt + i, h)),
        out_shape=jax.ShapeDtypeStruct((b * s_len, 2 * kvh * HEAD_DIM), bf16),
        compiler_params=_cparams(("arbitrary", "arbitrary", "arbitrary"), need),
        name="moba_prompt",
    )(q_rot, k_rot, proj, gnorm.reshape(1, HEAD_DIM))


QROWS = 8


def _moba_sample_kernel(pt_ref, q_ref, kn_ref, vn_ref, ke_ref, ko_ref, ve_ref, vo_ref, g_ref, o_ref,
                        qs_sc, s_sc, gate_sc, idx_sc, m_sc, l_sc, acc_sc, *, nb, kvh, t_len):
    del pt_ref
    j = pl.program_id(1)
    scale = HEAD_DIM ** -0.5
    r2 = 2 * QROWS

    @pl.when(j == 0)
    def _():
        for h in range(kvh):
            q2 = jnp.concatenate([q_ref[0:QROWS, (2 * h) * HEAD_DIM:(2 * h + 1) * HEAD_DIM],
                                  q_ref[0:QROWS, (2 * h + 1) * HEAD_DIM:(2 * h + 2) * HEAD_DIM]], axis=0)
            qs_sc[h] = q2
        gate_sc[...] = jnp.zeros_like(gate_sc)

    @pl.when(j < nb)
    def _():
        kblk = jnp.concatenate([ke_ref[...], ko_ref[...]], axis=0)
        lane = _iota((r2, LANES), 1)
        for h in range(kvh):
            kh = kblk[:, h * HEAD_DIM:(h + 1) * HEAD_DIM]
            q2 = qs_sc[h]
            s_sc[j, h] = _bdot_nt(q2 * scale, kh)
            kmean = jnp.sum(kh, axis=0, keepdims=True) * (1.0 / MOBA_BLOCK)
            col = jnp.sum(q2 * kmean, axis=-1, keepdims=True)
            gate_sc[h] = jnp.where(lane == j, col, gate_sc[h])

    @pl.when(j == nb - 1)
    def _():
        for h in range(kvh):
            picks = _top_blocks(gate_sc[h], nb)
            for kk in range(MOBA_TOPK):
                idx_sc[h * MOBA_TOPK + kk] = jnp.broadcast_to(picks[kk], (r2, LANES))
            kn = jnp.concatenate([kn_ref[0:QROWS, h * HEAD_DIM:(h + 1) * HEAD_DIM],
                                  jnp.zeros((LANES - QROWS, HEAD_DIM), f32)], axis=0)
            vn = jnp.concatenate([vn_ref[0:QROWS, h * HEAD_DIM:(h + 1) * HEAD_DIM],
                                  jnp.zeros((LANES - QROWS, HEAD_DIM), f32)], axis=0)
            s = _bdot_nt(qs_sc[h] * scale, kn)
            tq = _iota(s.shape, 0) % QROWS
            ck = _iota(s.shape, 1)
            s = jnp.where((ck <= tq) & (ck < t_len), s, -jnp.inf)
            m0 = jnp.max(s, axis=-1, keepdims=True)
            p = jnp.exp(s - m0)
            m_sc[h] = jnp.broadcast_to(m0, (r2, LANES))
            l_sc[h] = jnp.broadcast_to(jnp.sum(p, axis=-1, keepdims=True), (r2, LANES))
            acc_sc[h] = _bdot(p, vn)

    @pl.when(j >= nb)
    def _():
        n = j - nb
        vblk = jnp.concatenate([ve_ref[...], vo_ref[...]], axis=0)
        for h in range(kvh):
            sel = ((idx_sc[h * MOBA_TOPK][:, 0:1] == n) | (idx_sc[h * MOBA_TOPK + 1][:, 0:1] == n)
                   | (idx_sc[h * MOBA_TOPK + 2][:, 0:1] == n))
            sn = jnp.where(sel, s_sc[n, h], -jnp.inf)
            m = m_sc[h][:, 0:1]
            m_new = jnp.maximum(m, jnp.max(sn, axis=-1, keepdims=True))
            alpha = jnp.exp(m - m_new)
            pn = jnp.exp(sn - m_new)
            l_sc[h] = jnp.broadcast_to(alpha * l_sc[h][:, 0:1] + jnp.sum(pn, axis=-1, keepdims=True), (r2, LANES))
            acc_sc[h] = alpha * acc_sc[h] + _bdot(pn, vblk[:, h * HEAD_DIM:(h + 1) * HEAD_DIM])
            m_sc[h] = jnp.broadcast_to(m_new, (r2, LANES))

    @pl.when(j == 2 * nb - 1)
    def _():
        for h in range(kvh):
            o = _rms_heads(acc_sc[h] / l_sc[h][:, 0:1], g_ref[...])
            o_ref[:, (2 * h) * HEAD_DIM:(2 * h + 1) * HEAD_DIM] = o[:QROWS]
            o_ref[:, (2 * h + 1) * HEAD_DIM:(2 * h + 2) * HEAD_DIM] = o[QROWS:]


def _moba_sample_call(page_table, q_rot, k_rot, proj, cache_k2, cache_v2, gnorm, layer, n_pool,
                      db, row0, kvh, v_col0, t_len):
    n_pages = page_table.shape[1]
    nb = n_pages * PAGE_SIZE // MOBA_BLOCK
    ppb = MOBA_BLOCK // PAGE_SIZE
    assert ppb == 2
    qw = 2 * kvh * HEAD_DIM
    kw = kvh * HEAD_DIM
    blk0 = row0 // SUB
    base = layer * n_pool

    def kmap(off):
        return lambda b, j, pt: (base + pt[b, ppb * jnp.minimum(j, nb - 1) + off], 0, 0)

    def vmap_(off):
        return lambda b, j, pt: (base + pt[b, ppb * jnp.maximum(j - nb, 0) + off], 0, 0)

    page_spec = lambda fn: pl.BlockSpec((None, PAGE_SIZE, kw), fn)
    r2 = 2 * QROWS
    need = 8 * PAGE_SIZE * kw * 4 + nb * kvh * r2 * MOBA_BLOCK * 4 + 4 * MOBA_BLOCK * kw * 4 + (8 << 20)
    grid_spec = pltpu.PrefetchScalarGridSpec(
        num_scalar_prefetch=1,
        grid=(db, 2 * nb),
        in_specs=[pl.BlockSpec((SUB, qw), lambda b, j, pt: (blk0 + b, 0)),
                  pl.BlockSpec((SUB, kw), lambda b, j, pt: (blk0 + b, 0)),
                  pl.BlockSpec((SUB, kw), lambda b, j, pt: (blk0 + b, v_col0 // kw)),
                  page_spec(kmap(0)), page_spec(kmap(1)), page_spec(vmap_(0)), page_spec(vmap_(1)),
                  pl.BlockSpec((1, HEAD_DIM), lambda b, j, pt: (0, 0))],
        out_specs=pl.BlockSpec((QROWS, qw), lambda b, j, pt: (b, 0)),
        scratch_shapes=[pltpu.VMEM((kvh, r2, HEAD_DIM), f32),
                        pltpu.VMEM((nb, kvh, r2, MOBA_BLOCK), f32),
                        pltpu.VMEM((kvh, r2, LANES), f32),
                        pltpu.VMEM((kvh * MOBA_TOPK, r2, LANES), jnp.int32),
                        pltpu.VMEM((kvh, r2, LANES), f32),
                        pltpu.VMEM((kvh, r2, LANES), f32),
                        pltpu.VMEM((kvh, r2, HEAD_DIM), f32)])
    return pl.pallas_call(
        functools.partial(_moba_sample_kernel, nb=nb, kvh=kvh, t_len=t_len),
        grid_spec=grid_spec,
        out_shape=jax.ShapeDtypeStruct((db * QROWS, qw), f32),
        compiler_params=_cparams(("arbitrary", "arbitrary"), need),
        name="moba_sample",
    )(page_table, q_rot, k_rot, proj, cache_k2, cache_k2, cache_v2, cache_v2, gnorm.reshape(1, HEAD_DIM))


def _inv_unit_lower(lmat):
    c = lmat.shape[0]
    ii = _iota((c, c), 0)
    jj = _iota((c, c), 1)
    eye = (ii == jj).astype(f32)
    same = (ii // INV_BLK) == (jj // INV_BLK)
    dmat = jnp.where(same, lmat, 0.0)
    rmat = lmat - dmat
    p = -dmat
    x = eye + p
    k = 2
    while k < INV_BLK:
        p = _dot3(p, p)
        x = x + _dot3(x, p)
        k *= 2
    nblk = c // INV_BLK
    if nblk == 1:
        return x
    mm = _dot3(x, rmat)
    y = eye - mm
    pm = mm
    k = 2
    while k < nblk:
        pm = _dot3(pm, pm)
        y = y + _dot3(y, pm)
        k *= 2
    return _dot3(y, x)


def _gdn_kernel(main_ref, ab_ref, cw_ref, prev_ref, alog_ref, dtb_ref, s0_ref, g_ref, o_ref, s_ref, carry_sc,
                *, heads, col0, t_valid):
    c_idx = pl.program_id(1)
    c = main_ref.shape[0]
    gw = heads * HEAD_DIM
    ch = 3 * gw

    @pl.when(c_idx == 0)
    def _():
        s_ref[...] = s0_ref[...]
        carry_sc[...] = prev_ref[...]

    x = main_ref[:, col0:col0 + ch]
    xp = jnp.concatenate([carry_sc[...], x], axis=0)
    y = cw_ref[GDN_CONV - 1:GDN_CONV, :] * x
    for tap in range(1, GDN_CONV):
        y = y + cw_ref[GDN_CONV - 1 - tap:GDN_CONV - tap, :] * pltpu.roll(xp, tap, 0)[SUBLANES:, :]
    carry_sc[...] = x[c - SUBLANES:, :]
    y = _silu(y)

    ab = ab_ref[...]
    lane = _iota(ab.shape, 1)
    row_ok = (c_idx * c + _iota((c, 1), 0)) < t_valid
    g_all = jnp.where(row_ok & (lane < heads), -jnp.exp(alog_ref[...]) * _softplus(ab + dtb_ref[...]), 0.0)
    beta_all = jnp.where(row_ok, jax.nn.sigmoid(ab), 0.0)
    ii = _iota((c, c), 0)
    jj = _iota((c, c), 1)
    tri = (ii >= jj).astype(bf16)
    gam_all = _sel_dot(tri, g_all)
    eye_l = (_iota((LANES, LANES), 0) == _iota((LANES, LANES), 1)).astype(bf16)
    gam_t = _sel_dot_nt(eye_l, gam_all)
    egam_all = jnp.exp(gam_all)
    gnorm = g_ref[...]

    for h in range(heads):
        hs = slice(h * HEAD_DIM, (h + 1) * HEAD_DIM)
        xq = y[:, h * HEAD_DIM:(h + 1) * HEAD_DIM]
        xk = y[:, gw + h * HEAD_DIM:gw + (h + 1) * HEAD_DIM]
        v = y[:, 2 * gw + h * HEAD_DIM:2 * gw + (h + 1) * HEAD_DIM]
        q = xq * lax.rsqrt(jnp.sum(xq * xq, axis=-1, keepdims=True) + EPS) * (HEAD_DIM ** -0.5)
        k = xk * lax.rsqrt(jnp.sum(xk * xk, axis=-1, keepdims=True) + EPS)
        beta = beta_all[:, heads + h:heads + h + 1]
        gcol = gam_all[:, h:h + 1]
        grow = gam_t[h:h + 1, :]
        egam = egam_all[:, h:h + 1]
        decay = jnp.exp(jnp.where(ii >= jj, gcol - grow, -jnp.inf))
        lmat = jnp.where(ii > jj, beta * _bdot_nt(k, k) * decay, 0.0)
        tmat = _inv_unit_lower(lmat)
        u = _bdot(tmat, beta * v)
        w = _bdot(tmat, (beta * egam) * k)
        qk = _bdot_nt(q, k) * decay
        s = s_ref[h]
        v_new = u - _bdot(w, s)
        o = _bdot(q * egam, s) + _bdot(qk, v_new)
        glast = gcol[c - 1:c, :]
        s_ref[h] = jnp.exp(glast) * s + _bdot_tn(k * jnp.exp(glast - gcol), v_new)
        z = main_ref[:, col0 + ch + h * HEAD_DIM:col0 + ch + (h + 1) * HEAD_DIM]
        o_ref[:, hs] = (_rms_heads(o, gnorm) * _silu(z)).astype(o_ref.dtype)


def _gdn_call(proj, tail, cw, prev8, alog, dtb, s0, gnorm, layer, *, nseq, rows_per_seq, row0, chunk, heads,
              col0, ab_col0, t_valid, name):
    mainw = proj.shape[1]
    gw = heads * HEAD_DIM
    ch = 3 * gw
    nch = rows_per_seq // chunk
    blk0 = row0 // chunk
    need = 2 * chunk * mainw * 4 + 4 * heads * HEAD_DIM * HEAD_DIM * 4 + 40 * chunk * ch * 4 + (4 << 20)
    return pl.pallas_call(
        functools.partial(_gdn_kernel, heads=heads, col0=col0, t_valid=t_valid),
        grid=(nseq, nch),
        in_specs=[pl.BlockSpec((chunk, mainw), lambda b, c: (blk0 + b * nch + c, 0)),
                  pl.BlockSpec((chunk, LANES), lambda b, c: (blk0 + b * nch + c, ab_col0 // LANES)),
                  pl.BlockSpec((None, GDN_CONV, ch), lambda b, c: (layer, 0, 0)),
                  pl.BlockSpec((None, SUBLANES, ch), lambda b, c: (b, 0, 0)),
                  pl.BlockSpec((1, LANES), lambda b, c: (0, 0)),
                  pl.BlockSpec((1, LANES), lambda b, c: (0, 0)),
                  pl.BlockSpec((None, heads, HEAD_DIM, HEAD_DIM), lambda b, c: (b, 0, 0, 0)),
                  pl.BlockSpec((1, HEAD_DIM), lambda b, c: (0, 0))],
        out_specs=[pl.BlockSpec((chunk, gw), lambda b, c: (b * nch + c, 0)),
                   pl.BlockSpec((None, heads, HEAD_DIM, HEAD_DIM), lambda b, c: (b, 0, 0, 0))],
        out_shape=[jax.ShapeDtypeStruct((nseq * rows_per_seq, gw), bf16),
                   jax.ShapeDtypeStruct((nseq, heads, HEAD_DIM, HEAD_DIM), f32)],
        scratch_shapes=[pltpu.VMEM((SUBLANES, ch), f32)],
        compiler_params=_cparams(("arbitrary", "arbitrary"), need),
        name=name,
    )(proj, tail, cw, prev8, alog, dtb, s0, gnorm.reshape(1, HEAD_DIM))


def _gla_kernel(tail_ref, wg_ref, bg_ref, s0_ref, g_ref, o_ref, s_ref, *, heads, ab_col0, t_valid):
    c_idx = pl.program_id(1)
    c = tail_ref.shape[0]
    kw = heads * GLA_DK
    vw = heads * HEAD_DIM

    @pl.when(c_idx == 0)
    def _():
        s_ref[...] = s0_ref[...]

    row_ok = (c_idx * c + _iota((c, 1), 0)) < t_valid
    fblk = tail_ref[:, ab_col0:ab_col0 + LANES]
    pre = _dot3(fblk, wg_ref[...]) + bg_ref[...]
    log_a = jnp.where(row_ok, -_softplus(-pre) * (1.0 / GLA_TAU), 0.0)
    ii = _iota((c, c), 0)
    jj = _iota((c, c), 1)
    tri = (ii >= jj).astype(bf16)
    bc_all = _sel_dot(tri, log_a)
    gnorm = g_ref[...]
    eye_k = _iota((GLA_DK, GLA_DK), 0) == _iota((GLA_DK, GLA_DK), 1)
    sub_i = _iota((GLA_SUB, 1), 0)
    rowid = _iota((c, 1), 0)

    for h in range(heads):
        ks = slice(h * GLA_DK, (h + 1) * GLA_DK)
        vs = slice(h * HEAD_DIM, (h + 1) * HEAD_DIM)
        q = tail_ref[:, h * GLA_DK:(h + 1) * GLA_DK] * (GLA_DK ** -0.5)
        k = jnp.where(row_ok, tail_ref[:, kw + h * GLA_DK:kw + (h + 1) * GLA_DK], 0.0)
        v = tail_ref[:, 2 * kw + h * HEAD_DIM:2 * kw + (h + 1) * HEAD_DIM]
        r = tail_ref[:, 2 * kw + vw + h * HEAD_DIM:2 * kw + vw + (h + 1) * HEAD_DIM]
        bc = bc_all[:, ks]
        s = s_ref[h]
        o_inter = _bdot(q * jnp.exp(bc), s)
        pieces = []
        for sb in range(c // GLA_SUB):
            r0 = sb * GLA_SUB
            rs = slice(r0, r0 + GLA_SUB)
            b0 = bc[r0:r0 + 1, :]
            o_sb = o_inter[rs, :]
            if sb > 0:
                q_i = q[rs, :] * jnp.exp(bc[rs, :] - b0)
                k_j = k * jnp.exp(jnp.where(rowid < r0, b0 - bc, -jnp.inf))
                o_sb = o_sb + _bdot(_bdot_nt(q_i, k_j), v)
            q_s = q[rs, :]
            bc_s = bc[rs, :]
            for j in range(GLA_SUB):
                e = jnp.exp(jnp.where(sub_i >= j, bc_s - bc[r0 + j:r0 + j + 1, :], -jnp.inf))
                col = jnp.sum(q_s * e * k[r0 + j:r0 + j + 1, :], axis=-1, keepdims=True)
                o_sb = o_sb + col * v[r0 + j:r0 + j + 1, :]
            pieces.append(o_sb)
        o = jnp.concatenate(pieces, axis=0)
        bl = bc[c - 1:c, :]
        ebl_col = jnp.sum(jnp.where(eye_k, jnp.exp(bl), 0.0), axis=-1, keepdims=True)
        s_ref[h] = ebl_col * s + _bdot_tn(k * jnp.exp(bl - bc), v)
        o_ref[:, vs] = (_rms_heads(o, gnorm) * _silu(r)).astype(o_ref.dtype)


def _gla_call(tail, wg_pad, bg, s0, gnorm, *, nseq, rows_per_seq, row0, chunk, heads, ab_col0, t_valid, name):
    tailw = tail.shape[1]
    kw = heads * GLA_DK
    vw = heads * HEAD_DIM
    nch = rows_per_seq // chunk
    blk0 = row0 // chunk
    need = 2 * chunk * tailw * 4 + 4 * heads * GLA_DK * HEAD_DIM * 4 + 2 * LANES * kw * 4 + 40 * chunk * tailw * 4
    return pl.pallas_call(
        functools.partial(_gla_kernel, heads=heads, ab_col0=ab_col0, t_valid=t_valid),
        grid=(nseq, nch),
        in_specs=[pl.BlockSpec((chunk, tailw), lambda b, c: (blk0 + b * nch + c, 0)),
                  pl.BlockSpec((LANES, kw), lambda b, c: (0, 0)),
                  pl.BlockSpec((1, kw), lambda b, c: (0, 0)),
                  pl.BlockSpec((None, heads, GLA_DK, HEAD_DIM), lambda b, c: (b, 0, 0, 0)),
                  pl.BlockSpec((1, HEAD_DIM), lambda b, c: (0, 0))],
        out_specs=[pl.BlockSpec((chunk, vw), lambda b, c: (b * nch + c, 0)),
                   pl.BlockSpec((None, heads, GLA_DK, HEAD_DIM), lambda b, c: (b, 0, 0, 0))],
        out_shape=[jax.ShapeDtypeStruct((nseq * rows_per_seq, vw), bf16),
                   jax.ShapeDtypeStruct((nseq, heads, GLA_DK, HEAD_DIM), f32)],
        compiler_params=_cparams(("arbitrary", "arbitrary"), need),
        name=name,
    )(tail, wg_pad, bg.reshape(1, kw), s0, gnorm.reshape(1, HEAD_DIM))


def kernel(x_prompt, x_sample, c_prompt, c_sample, cache_k, cache_v, page_table, state_gdn, state_gdn_conv, state_gla, ada_w, ada_b, norm_ffn_a, ffn_a_wg, ffn_a_wu, ffn_a_wd, norm_mix, w_in, moba_norm, gdn_conv_w, gdn_a_log, gdn_dt_bias, gdn_norm, gla_w_gate, gla_b_gate, gla_norm, w_out, norm_ffn_b, ffn_b_wg, ffn_b_wu, ffn_b_wd, final_norm):
    b, s_len, d = x_prompt.shape
    db, t_dec, _ = x_sample.shape
    depth = ada_w.shape[0]
    n_pool = cache_k.shape[1]
    n_pages = page_table.shape[1]
    past_len = n_pages * PAGE_SIZE

    n_heads = d // HEAD_DIM
    moba_h = n_heads // 4
    kvh = moba_h // 2
    gdn_h = (3 * n_heads) // 8
    gla_h = n_heads - moba_h - gdn_h
    qw, kvw = moba_h * HEAD_DIM, kvh * HEAD_DIM
    gdn_w, gla_kw, gla_w = gdn_h * HEAD_DIM, gla_h * GLA_DK, gla_h * HEAD_DIM
    main_w = qw + 2 * kvw + 4 * gdn_w
    small0 = main_w
    gla0 = main_w + 2 * gdn_h
    f0 = gla0 + 2 * gla_kw + 2 * gla_w
    tail_main = 2 * gla_kw + 2 * gla_w
    assert w_in.shape[2] == f0 + GLA_GATE_RANK
    assert s_len % TM == 0 and s_len % CHUNK == 0 and s_len >= MOBA_TOPK * MOBA_BLOCK
    assert (db * SUB) % TM == 0 and b + db <= 16
    assert GDN_CONV - 1 <= t_dec <= QROWS and past_len % MOBA_BLOCK == 0
    assert 2 * gdn_h + GLA_GATE_RANK <= LANES and tail_main % LANES == 0

    bs = b * s_len
    m_tot = bs + db * SUB
    n_tiles_p = bs // TM

    def pack_rows(p_rows, s_rows):
        s_pad = jnp.pad(s_rows, ((0, 0), (0, SUB - s_rows.shape[1]), (0, 0)))
        return jnp.concatenate([p_rows, s_pad.reshape(db * SUB, s_rows.shape[2])], axis=0)

    def sample_rows(a):
        return a[bs:].reshape(db, SUB, a.shape[1])[:, :t_dec]

    x = pack_rows(x_prompt.reshape(bs, d), x_sample)

    c16 = jnp.concatenate([c_prompt, c_sample, jnp.zeros((16 - b - db, d), f32)], axis=0)
    mod = _mod_call(c16, ada_w, ada_b)

    half = ROPE_DIM // 2
    inv_freq = ROPE_THETA ** (-jnp.arange(half, dtype=f32) / half)
    pos_s = past_len + jnp.minimum(jnp.arange(SUB), t_dec - 1)
    pos = jnp.concatenate([jnp.tile(jnp.arange(s_len), b), jnp.tile(pos_s, db)]).astype(f32)
    ang = pos[:, None] * inv_freq[None, :]
    ones = jnp.ones((m_tot, HEAD_DIM - ROPE_DIM), f32)
    cos_t = jnp.concatenate([jnp.cos(ang), jnp.cos(ang), ones], axis=1)
    sin_t = jnp.concatenate([-jnp.sin(ang), jnp.sin(ang), 0.0 * ones], axis=1)

    cache_k2 = cache_k.reshape(depth * n_pool, PAGE_SIZE, kvw)
    cache_v2 = cache_v.reshape(depth * n_pool, PAGE_SIZE, kvw)
    tail_w = -(-(tail_main + LANES) // 512) * 512
    zeros_prev = jnp.zeros((b, SUBLANES, 3 * gdn_w), f32)

    outs = {k: [] for k in ("kp", "vp", "ks", "vs", "gp", "gs", "cp", "cs", "lp", "ls")}
    for l in range(depth):
        mod8 = jnp.concatenate([jnp.repeat(mod[l, :b], (s_len // TM) * (TM // SUB), axis=0), mod[l, b:b + db]], axis=0)

        h = _premod_call(x, norm_ffn_a[l], mod8, 0, 1)
        a = _mm_up_call(h, ffn_a_wg, ffn_a_wu, l)
        x = _mm_res_call(a, ffn_a_wd, l, x, mod8, 2, 0.5, "ffn_a_down")

        h = _premod_call(x, norm_mix[l], mod8, 3, 4)
        proj = _mm_plain_call(h, w_in, l, main_w, "w_in_main")
        w_tail = jnp.concatenate([w_in[l][:, gla0:f0], w_in[l][:, small0:gla0], w_in[l][:, f0:],
                                  jnp.zeros((d, tail_w - tail_main - 2 * gdn_h - GLA_GATE_RANK), f32)], axis=1)
        tail = _mm_plain_call(h, w_tail[None], 0, tail_w, "w_in_tail")

        q_rot, k_rot = _rope_call(proj, cos_t, sin_t, qw, kvw)
        v_col0 = qw + kvw
        om_p = _moba_prompt_call(q_rot, k_rot, proj, moba_norm[l], b, s_len, kvh, v_col0)
        om_s = _moba_sample_call(page_table, q_rot, k_rot, proj, cache_k2, cache_v2, moba_norm[l], l, n_pool,
                                 db, bs, kvh, v_col0, t_dec)
        om_s = om_s.reshape(db, QROWS, qw)[:, :t_dec].astype(bf16)

        col0 = qw + 2 * kvw
        alog = jnp.zeros((1, LANES), f32).at[0, :gdn_h].set(gdn_a_log[l])
        dtb = jnp.zeros((1, LANES), f32).at[0, :gdn_h].set(gdn_dt_bias[l])
        od_p, gp = _gdn_call(proj, tail, gdn_conv_w, zeros_prev, alog, dtb,
                             jnp.zeros((b, gdn_h, HEAD_DIM, HEAD_DIM), f32), gdn_norm[l], l,
                             nseq=b, rows_per_seq=s_len, row0=0, chunk=CHUNK, heads=gdn_h, col0=col0,
                             ab_col0=tail_main, t_valid=s_len, name="gdn_prompt")
        prev_s = jnp.pad(state_gdn_conv[l], ((0, 0), (SUBLANES - (GDN_CONV - 1), 0), (0, 0)))
        od_s, gs = _gdn_call(proj, tail, gdn_conv_w, prev_s, alog, dtb, state_gdn[l], gdn_norm[l], l,
                             nseq=db, rows_per_seq=SUB, row0=bs, chunk=SUB, heads=gdn_h, col0=col0,
                             ab_col0=tail_main, t_valid=t_dec, name="gdn_sample")

        wg_pad = jnp.zeros((LANES, gla_kw), f32).at[2 * gdn_h:2 * gdn_h + GLA_GATE_RANK].set(gla_w_gate[l])
        ol_p, lp = _gla_call(tail, wg_pad, gla_b_gate[l], jnp.zeros((b, gla_h, GLA_DK, HEAD_DIM), f32), gla_norm[l],
                             nseq=b, rows_per_seq=s_len, row0=0, chunk=CHUNK, heads=gla_h, ab_col0=tail_main,
                             t_valid=s_len, name="gla_prompt")
        ol_s, ls = _gla_call(tail, wg_pad, gla_b_gate[l], state_gla[l], gla_norm[l],
                             nseq=db, rows_per_seq=SUB, row0=bs, chunk=SUB, heads=gla_h, ab_col0=tail_main,
                             t_valid=t_dec, name="gla_sample")

        o_p = jnp.concatenate([om_p, od_p, ol_p], axis=1)
        o_s = jnp.concatenate([om_s, od_s.reshape(db, SUB, gdn_w)[:, :t_dec], ol_s.reshape(db, SUB, gla_w)[:, :t_dec]], axis=2)
        o_mix = pack_rows(o_p, o_s)
        x = _mm_res_call(o_mix, w_out, l, x, mod8, 5, 1.0, "w_out")

        h = _premod_call(x, norm_ffn_b[l], mod8, 6, 7)
        a = _mm_up_call(h, ffn_b_wg, ffn_b_wu, l)
        x = _mm_res_call(a, ffn_b_wd, l, x, mod8, 8, 0.5, "ffn_b_down")

        conv_cols = proj[:, col0:col0 + 3 * gdn_w]
        outs["kp"].append(k_rot[:bs].reshape(b, s_len, kvh, HEAD_DIM))
        outs["vp"].append(proj[:bs, v_col0:v_col0 + kvw].reshape(b, s_len, kvh, HEAD_DIM))
        outs["ks"].append(sample_rows(k_rot).reshape(db, t_dec, kvh, HEAD_DIM))
        outs["vs"].append(sample_rows(proj[:, v_col0:v_col0 + kvw]).reshape(db, t_dec, kvh, HEAD_DIM))
        outs["gp"].append(gp)
        outs["gs"].append(gs)
        outs["cp"].append(conv_cols[:bs].reshape(b, s_len, 3 * gdn_w)[:, s_len - (GDN_CONV - 1):])
        outs["cs"].append(sample_rows(conv_cols)[:, t_dec - (GDN_CONV - 1):])
        outs["lp"].append(lp)
        outs["ls"].append(ls)

    y = _rms_call(x, final_norm)
    st = {k: jnp.stack(v) for k, v in outs.items()}
    return (y[:bs].reshape(b, s_len, d), sample_rows(y),
            st["kp"], st["vp"], st["ks"], st["vs"], st["gp"], st["gs"], st["cp"], st["cs"], st["lp"], st["ls"])
```

```python
import functools
import math

import jax
import jax.numpy as jnp
from jax import lax
from jax.experimental import pallas as pl
from jax.experimental.pallas import tpu as pltpu

f32 = jnp.float32
bf16 = jnp.bfloat16

HEAD_DIM = 128
MOBA_BLOCK = 256
MOBA_TOPK = 3
ROPE_THETA = 500000.0
ROPE_DIM = HEAD_DIM // 4
GDN_CONV = 4
GLA_DK = HEAD_DIM // 2
GLA_GATE_RANK = 16
GLA_TAU = 16.0
GLA_SUB = 16
INV_BLK = 16
CHUNK = 64
PAGE_SIZE = 128
N_MOD = 9
EPS = 1e-6

LANES = 128
SUBLANES = 8
TM = 256
SUB = 32
V7X_VMEM_BYTES = 64 * 1024 * 1024
VMEM_BUDGET = 58 * 1024 * 1024


def _cparams(sem, need_bytes):
    limit = int(min(max(need_bytes * 1.25 + (4 << 20), 16 << 20), VMEM_BUDGET))
    return pltpu.CompilerParams(dimension_semantics=sem, vmem_limit_bytes=limit)


def _pick(n, prefs):
    for p in prefs:
        if n % p == 0:
            return p
    raise ValueError(f"no tile in {prefs} divides {n}")


def _bdot(a, b):
    return jnp.dot(a.astype(bf16), b.astype(bf16), preferred_element_type=f32)


def _bdot_nt(a, b):
    return lax.dot_general(a.astype(bf16), b.astype(bf16), (((1,), (1,)), ((), ())),
                           preferred_element_type=f32)


def _bdot_tn(a, b):
    return lax.dot_general(a.astype(bf16), b.astype(bf16), (((0,), (0,)), ((), ())),
                           preferred_element_type=f32)


def _split2(a):
    hi = a.astype(bf16)
    lo = (a - hi.astype(f32)).astype(bf16)
    return hi, lo


def _split3(a):
    hi = a.astype(bf16)
    r = a - hi.astype(f32)
    mid = r.astype(bf16)
    lo = (r - mid.astype(f32)).astype(bf16)
    return hi, mid, lo


def _dot3(a, b):
    ah, al = _split2(a)
    bh, bl = _split2(b)
    d = functools.partial(jnp.dot, preferred_element_type=f32)
    return d(ah, bh) + (d(ah, bl) + d(al, bh))


def _dot3_nt(a, b):
    ah, al = _split2(a)
    bh, bl = _split2(b)
    d = functools.partial(lax.dot_general, dimension_numbers=(((1,), (1,)), ((), ())),
                          preferred_element_type=f32)
    return d(ah, bh) + (d(ah, bl) + d(al, bh))


def _sel_dot(sel, b):
    bh, bm, bl = _split3(b)
    d = functools.partial(jnp.dot, preferred_element_type=f32)
    return d(sel, bh) + (d(sel, bm) + d(sel, bl))


def _sel_dot_nt(sel, b):
    bh, bm, bl = _split3(b)
    d = functools.partial(lax.dot_general, dimension_numbers=(((1,), (1,)), ((), ())),
                          preferred_element_type=f32)
    return d(sel, bh) + (d(sel, bm) + d(sel, bl))


def _silu(x):
    return x * jax.nn.sigmoid(x)


def _softplus(x):
    return jnp.maximum(x, 0.0) + jnp.log(1.0 + jnp.exp(-jnp.abs(x)))


def _iota(shape, dim):
    return lax.broadcasted_iota(jnp.int32, shape, dim)


def _rms_heads(o, g):
    return o * lax.rsqrt(jnp.mean(o * o, axis=-1, keepdims=True) + EPS) * g


def _mod_kernel(c_ref, w_ref, b_ref, o_ref):
    c = c_ref[...]
    o_ref[...] = _bdot(_silu(c), w_ref[...]) + b_ref[...]


def _mod_call(c16, ada_w, ada_b):
    depth, d, n = ada_w.shape
    tn = _pick(n, (512, 256, 128))
    need = 2 * d * tn * 4 + d * tn * 2 + 4 * 16 * tn * 4 + 2 * 16 * d * 4
    return pl.pallas_call(
        _mod_kernel,
        grid=(depth, n // tn),
        in_specs=[pl.BlockSpec((16, d), lambda l, j: (0, 0)),
                  pl.BlockSpec((None, d, tn), lambda l, j: (l, 0, j)),
                  pl.BlockSpec((None, 1, tn), lambda l, j: (l, 0, j))],
        out_specs=pl.BlockSpec((None, 16, tn), lambda l, j: (l, 0, j)),
        out_shape=jax.ShapeDtypeStruct((depth, 16, n), f32),
        compiler_params=_cparams(("arbitrary", "arbitrary"), need),
        name="adaln_mod",
    )(c16, ada_w, ada_b.reshape(depth, 1, n))


def _premod_kernel(x_ref, g_ref, sh_ref, sc_ref, o_ref):
    g = g_ref[...]
    for s in range(TM // SUB):
        rows = slice(s * SUB, (s + 1) * SUB)
        xs = x_ref[rows, :]
        y = xs * lax.rsqrt(jnp.mean(xs * xs, axis=-1, keepdims=True) + EPS) * g
        o_ref[rows, :] = (y * (1.0 + sc_ref[s:s + 1, :]) + sh_ref[s:s + 1, :]).astype(o_ref.dtype)


def _premod_call(x, g, mod8, v_shift, v_scale):
    m, d = x.shape
    nsub = TM // SUB
    need = 2 * TM * d * 4 + 2 * TM * d * 2 + 6 * nsub * d * 4
    return pl.pallas_call(
        _premod_kernel,
        grid=(m // TM,),
        in_specs=[pl.BlockSpec((TM, d), lambda i: (i, 0)),
                  pl.BlockSpec((1, d), lambda i: (0, 0)),
                  pl.BlockSpec((nsub, d), lambda i: (i, v_shift)),
                  pl.BlockSpec((nsub, d), lambda i: (i, v_scale))],
        out_specs=pl.BlockSpec((TM, d), lambda i: (i, 0)),
        out_shape=jax.ShapeDtypeStruct((m, d), bf16),
        compiler_params=_cparams(("arbitrary",), need),
        name="modulate",
    )(x, g.reshape(1, d), mod8, mod8)


def _rms_kernel(x_ref, g_ref, o_ref):
    x = x_ref[...]
    o_ref[...] = x * lax.rsqrt(jnp.mean(x * x, axis=-1, keepdims=True) + EPS) * g_ref[...]


def _rms_call(x, g):
    m, d = x.shape
    return pl.pallas_call(
        _rms_kernel,
        grid=(m // TM,),
        in_specs=[pl.BlockSpec((TM, d), lambda i: (i, 0)), pl.BlockSpec((1, d), lambda i: (0, 0))],
        out_specs=pl.BlockSpec((TM, d), lambda i: (i, 0)),
        out_shape=jax.ShapeDtypeStruct((m, d), f32),
        compiler_params=_cparams(("arbitrary",), 4 * TM * d * 4),
        name="final_norm",
    )(x, g.reshape(1, d))


MM_VMEM_TARGET = 48 << 20


def _ws_plan(m, k, n, n_weights, tile_bytes_per_out_elem, tns=(1024, 512, 256, 128)):
    for tn in tns:
        if n % tn:
            continue
        for bufs in (2, 1):
            for tm in (768, 512, 256):
                if m % tm:
                    continue
                need = n_weights * k * tn * (4 * bufs + 2) + 2 * tm * k * 2 + tm * tn * tile_bytes_per_out_elem
                if need <= MM_VMEM_TARGET:
                    return tm, tn, bufs, need
    raise ValueError(f"no weight-stationary tiling for {(m, k, n)}")


def _wspec(shape, index_map, bufs):
    if bufs == 1:
        return pl.BlockSpec(shape, index_map, pipeline_mode=pl.Buffered(1))
    return pl.BlockSpec(shape, index_map)


def _mm_plain_kernel(a_ref, w_ref, o_ref, wb_ref):
    @pl.when(pl.program_id(1) == 0)
    def _():
        wb_ref[...] = w_ref[...].astype(bf16)

    o_ref[...] = jnp.dot(a_ref[...], wb_ref[...], preferred_element_type=f32).astype(o_ref.dtype)


def _mm_plain_call(a, w3, layer, n_cols, name):
    m, k = a.shape
    tm, tn, bufs, need = _ws_plan(m, k, n_cols, 1, 2 * 4 + 4)
    return pl.pallas_call(
        _mm_plain_kernel,
        grid=(n_cols // tn, m // tm),
        in_specs=[pl.BlockSpec((tm, k), lambda j, i: (i, 0)),
                  _wspec((None, k, tn), lambda j, i: (layer, 0, j), bufs)],
        out_specs=pl.BlockSpec((tm, tn), lambda j, i: (i, j)),
        out_shape=jax.ShapeDtypeStruct((m, n_cols), f32),
        scratch_shapes=[pltpu.VMEM((k, tn), bf16)],
        compiler_params=_cparams(("arbitrary", "arbitrary"), need),
        name=name,
    )(a, w3)


def _mm_up_kernel(a_ref, wg_ref, wu_ref, o_ref, wgb_ref, wub_ref):
    @pl.when(pl.program_id(1) == 0)
    def _():
        wgb_ref[...] = wg_ref[...].astype(bf16)
        wub_ref[...] = wu_ref[...].astype(bf16)

    a = a_ref[...]
    g = jnp.dot(a, wgb_ref[...], preferred_element_type=f32)
    u = jnp.dot(a, wub_ref[...], preferred_element_type=f32)
    o_ref[...] = (_silu(g) * u).astype(o_ref.dtype)


def _mm_up_tail_kernel(a_ref, wg_ref, wu_ref, prev_ref, o_ref, wgb_ref, wub_ref):
    del prev_ref
    _mm_up_kernel(a_ref, wg_ref, wu_ref, o_ref, wgb_ref, wub_ref)


def _mm_up_call(a, wg3, wu3, layer):
    m, k = a.shape
    f = wg3.shape[2]
    tm, tn, bufs, need = _ws_plan(m, k, LANES * 8, 2, 2 * 2 + 3 * 4, tns=(512, 256, 128))
    tn = min(tn, f)
    n_main = f // tn
    rem = f - n_main * tn
    assert rem % LANES == 0 and (rem == 0 or (n_main * tn) % rem == 0)

    def call(kern, width, col_blk0, ncols, extra_in, extra_specs, aliases, name):
        wspec = _wspec((None, k, width), lambda j, i: (layer, 0, col_blk0 + j), bufs)
        return pl.pallas_call(
            kern,
            grid=(ncols, m // tm),
            in_specs=[pl.BlockSpec((tm, k), lambda j, i: (i, 0)), wspec, wspec] + extra_specs,
            out_specs=pl.BlockSpec((tm, width), lambda j, i: (i, col_blk0 + j)),
            out_shape=jax.ShapeDtypeStruct((m, f), bf16),
            scratch_shapes=[pltpu.VMEM((k, width), bf16), pltpu.VMEM((k, width), bf16)],
            input_output_aliases=aliases,
            compiler_params=_cparams(("arbitrary", "arbitrary"), need),
            name=name,
        )(a, wg3, wu3, *extra_in)

    out = call(_mm_up_kernel, tn, 0, n_main, [], [], {}, "ffn_up")
    if rem:
        out = call(_mm_up_tail_kernel, rem, (n_main * tn) // rem, 1, [out],
                   [pl.BlockSpec(memory_space=pl.ANY)], {3: 0}, "ffn_up_tail")
    return out


def _mm_res_kernel(a_ref, w_ref, r_ref, gate_ref, o_ref, wb_ref, *, scale):
    @pl.when(pl.program_id(1) == 0)
    def _():
        wb_ref[...] = w_ref[...].astype(bf16)

    acc = jnp.dot(a_ref[...], wb_ref[...], preferred_element_type=f32)
    for s in range(acc.shape[0] // SUB):
        rows = slice(s * SUB, (s + 1) * SUB)
        o_ref[rows, :] = r_ref[rows, :] + (scale * gate_ref[s:s + 1, :]) * acc[rows, :]


def _mm_res_call(a, w3, layer, res, mod8, v_gate, scale, name):
    m, k = a.shape
    n = w3.shape[2]
    nk = 1
    while (k // nk) * 1024 * 6 > (36 << 20) and (k // nk) % (2 * LANES) == 0:
        nk *= 2
    kc = k // nk
    tm, tn, bufs, need = _ws_plan(m, kc, n, 1, 4 * 4 + 2 * 4 + 4)
    nsub = tm // SUB
    out = res
    for kb in range(nk):
        out = pl.pallas_call(
            functools.partial(_mm_res_kernel, scale=scale),
            grid=(n // tn, m // tm),
            in_specs=[pl.BlockSpec((tm, kc), lambda j, i, kb=kb: (i, kb)),
                      _wspec((None, kc, tn), lambda j, i, kb=kb: (layer, kb, j), bufs),
                      pl.BlockSpec((tm, tn), lambda j, i: (i, j)),
                      pl.BlockSpec((nsub, tn), lambda j, i: (i, v_gate * (n // tn) + j))],
            out_specs=pl.BlockSpec((tm, tn), lambda j, i: (i, j)),
            out_shape=jax.ShapeDtypeStruct((m, n), f32),
            scratch_shapes=[pltpu.VMEM((kc, tn), bf16)],
            compiler_params=_cparams(("arbitrary", "arbitrary"), need),
            name=f"{name}_k{kb}",
        )(a, w3, out, mod8)
    return out


def _rope_kernel(x_ref, cos_ref, sin_ref, q_ref, k_ref):
    x = x_ref[...]
    w = x.shape[1]
    nh = w // HEAD_DIM
    cosf = jnp.concatenate([cos_ref[...]] * nh, axis=1)
    sinf = jnp.concatenate([sin_ref[...]] * nh, axis=1)
    lane = _iota(x.shape, 1) % HEAD_DIM
    half = ROPE_DIM // 2
    partner = jnp.where(lane < half, pltpu.roll(x, w - half, 1), pltpu.roll(x, half, 1))
    y = x * cosf + partner * sinf
    qw = q_ref.shape[1]
    q_ref[...] = y[:, :qw]
    k_ref[...] = y[:, qw:]


def _rope_call(proj, cos_t, sin_t, qw, kw):
    m = proj.shape[0]
    w = qw + kw
    return pl.pallas_call(
        _rope_kernel,
        grid=(m // TM,),
        in_specs=[pl.BlockSpec((TM, w), lambda i: (i, 0)),
                  pl.BlockSpec((TM, HEAD_DIM), lambda i: (i, 0)),
                  pl.BlockSpec((TM, HEAD_DIM), lambda i: (i, 0))],
        out_specs=[pl.BlockSpec((TM, qw), lambda i: (i, 0)), pl.BlockSpec((TM, kw), lambda i: (i, 0))],
        out_shape=[jax.ShapeDtypeStruct((m, qw), f32), jax.ShapeDtypeStruct((m, kw), f32)],
        compiler_params=_cparams(("arbitrary",), 10 * TM * w * 4),
        name="rope",
    )(proj, cos_t, sin_t)


def _top_blocks(gate, n_valid):
    lane = _iota(gate.shape, 1)
    gate = jnp.where(lane < n_valid, gate, -jnp.inf)
    picks = []
    for kk in range(MOBA_TOPK):
        mx = jnp.max(gate, axis=-1, keepdims=True)
        idx = jnp.min(jnp.where(gate == mx, lane, LANES), axis=-1, keepdims=True)
        picks.append(jnp.where(kk < n_valid, idx, -1))
        gate = jnp.where(lane == idx, -jnp.inf, gate)
    return picks


def _moba_prompt_kernel(q_ref, k_ref, v_ref, g_ref, o_ref):
    i = pl.program_id(2)
    s_len = k_ref.shape[0]
    nb = s_len // MOBA_BLOCK
    scale = HEAD_DIM ** -0.5
    q2 = jnp.concatenate([q_ref[:, :HEAD_DIM], q_ref[:, HEAD_DIM:]], axis=0)
    rows = q2.shape[0]
    blk_row = _iota((LANES, HEAD_DIM), 0)
    kmean = jnp.zeros((LANES, HEAD_DIM), f32)
    for n in range(nb):
        mean_n = jnp.mean(k_ref[n * MOBA_BLOCK:(n + 1) * MOBA_BLOCK, :], axis=0, keepdims=True)
        kmean = jnp.where(blk_row == n, mean_n, kmean)
    gate = _dot3_nt(q2, kmean)
    picks = _top_blocks(gate, i)
    qs = (q2 * scale).astype(bf16)

    own = pl.multiple_of(i * MOBA_BLOCK, MOBA_BLOCK)
    s = _bdot_nt(qs, k_ref[pl.ds(own, MOBA_BLOCK), :])
    rq = _iota(s.shape, 0) % MOBA_BLOCK
    ck = _iota(s.shape, 1)
    s = jnp.where(ck <= rq, s, -jnp.inf)
    m0 = jnp.max(s, axis=-1, keepdims=True)
    p = jnp.exp(s - m0)
    l0 = jnp.sum(p, axis=-1, keepdims=True)
    acc0 = _bdot(p, v_ref[pl.ds(own, MOBA_BLOCK), :])

    def body(n, carry):
        m, l, acc = carry
        start = pl.multiple_of(n * MOBA_BLOCK, MOBA_BLOCK)
        sn = _bdot_nt(qs, k_ref[pl.ds(start, MOBA_BLOCK), :])
        sel = (picks[0] == n) | (picks[1] == n) | (picks[2] == n)
        sn = jnp.where(sel, sn, -jnp.inf)
        m_new = jnp.maximum(m, jnp.max(sn, axis=-1, keepdims=True))
        alpha = jnp.exp(m - m_new)
        pn = jnp.exp(sn - m_new)
        l_new = alpha * l + jnp.sum(pn, axis=-1, keepdims=True)
        acc_new = alpha * acc + _bdot(pn, v_ref[pl.ds(start, MOBA_BLOCK), :])
        return m_new, l_new, acc_new

    m, l, acc = lax.fori_loop(0, i, body, (m0, l0, acc0))
    o = _rms_heads(acc / l, g_ref[...])
    half = rows // 2
    o_ref[:, :HEAD_DIM] = o[:half].astype(o_ref.dtype)
    o_ref[:, HEAD_DIM:] = o[half:].astype(o_ref.dtype)


def _moba_prompt_call(q_rot, k_rot, proj, gnorm, b, s_len, kvh, v_col0):
    qt = s_len // MOBA_BLOCK
    vb = v_col0 // HEAD_DIM
    need = 4 * s_len * HEAD_DIM * 4 + 4 * MOBA_BLOCK * 2 * HEAD_DIM * 4 + 16 * 2 * MOBA_BLOCK * MOBA_BLOCK * 4
    return pl.pallas_call(
        _moba_prompt_kernel,
        grid=(b, kvh, qt),
        in_specs=[pl.BlockSpec((MOBA_BLOCK, 2 * HEAD_DIM), lambda bi, h, i: (bi * qt + i, h)),
                  pl.BlockSpec((s_len, HEAD_DIM), lambda bi, h, i: (bi, h)),
                  pl.BlockSpec((s_len, HEAD_DIM), lambda bi, h, i: (bi, vb + h)),
                  pl.BlockSpec((1, HEAD_DIM), lambda bi, h, i: (0, 0))],
        out_specs=pl.BlockSpec((MOBA_BLOCK, 2 * HEAD_DIM), lambda bi, h, i: (bi * qt + i, h)),
        out_shape=jax.ShapeDtypeStruct((b * s_len, 2 * kvh * HEAD_DIM), bf16),
        compiler_params=_cparams(("arbitrary", "arbitrary", "arbitrary"), need),
        name="moba_prompt",
    )(q_rot, k_rot, proj, gnorm.reshape(1, HEAD_DIM))


QROWS = 8


def _moba_sample_kernel(pt_ref, q_ref, kn_ref, vn_ref, *refs, nb, bps, kvh, t_len):
    del pt_ref
    npg = bps * (MOBA_BLOCK // PAGE_SIZE)
    k_pages, v_pages = refs[:npg], refs[npg:2 * npg]
    g_ref, o_ref, qs_sc, s_sc, gate_sc, idx_sc, m_sc, l_sc, acc_sc = refs[2 * npg:]
    j = pl.program_id(1)
    nsk = nb // bps
    scale = HEAD_DIM ** -0.5
    r2 = 2 * QROWS

    def block_of(pages, bb, h):
        ppb = MOBA_BLOCK // PAGE_SIZE
        return jnp.concatenate([pages[bb * ppb + t][pl.ds(h, PAGE_SIZE, stride=kvh), :] for t in range(ppb)], axis=0)

    @pl.when(j == 0)
    def _():
        for h in range(kvh):
            q2 = jnp.concatenate([q_ref[0:QROWS, (2 * h) * HEAD_DIM:(2 * h + 1) * HEAD_DIM],
                                  q_ref[0:QROWS, (2 * h + 1) * HEAD_DIM:(2 * h + 2) * HEAD_DIM]], axis=0)
            qs_sc[h] = q2
        gate_sc[...] = jnp.zeros_like(gate_sc)

    @pl.when(j < nsk)
    def _():
        lane = _iota((r2, LANES), 1)
        for bb in range(bps):
            n = j * bps + bb
            for h in range(kvh):
                kh = block_of(k_pages, bb, h)
                q2 = qs_sc[h]
                s_sc[n, h] = _bdot_nt(q2 * scale, kh)
                kmean = jnp.sum(kh, axis=0, keepdims=True) * (1.0 / MOBA_BLOCK)
                col = jnp.sum(q2 * kmean, axis=-1, keepdims=True)
                gate_sc[h] = jnp.where(lane == n, col, gate_sc[h])

    @pl.when(j == nsk - 1)
    def _():
        for h in range(kvh):
            picks = _top_blocks(gate_sc[h], nb)
            for kk in range(MOBA_TOPK):
                idx_sc[h * MOBA_TOPK + kk] = jnp.broadcast_to(picks[kk], (r2, LANES))
            kn = jnp.concatenate([kn_ref[0:QROWS, h * HEAD_DIM:(h + 1) * HEAD_DIM],
                                  jnp.zeros((LANES - QROWS, HEAD_DIM), f32)], axis=0)
            vn = jnp.concatenate([vn_ref[0:QROWS, h * HEAD_DIM:(h + 1) * HEAD_DIM],
                                  jnp.zeros((LANES - QROWS, HEAD_DIM), f32)], axis=0)
            s = _bdot_nt(qs_sc[h] * scale, kn)
            tq = _iota(s.shape, 0) % QROWS
            ck = _iota(s.shape, 1)
            s = jnp.where((ck <= tq) & (ck < t_len), s, -jnp.inf)
            m0 = jnp.max(s, axis=-1, keepdims=True)
            p = jnp.exp(s - m0)
            m_sc[h] = jnp.broadcast_to(m0, (r2, LANES))
            l_sc[h] = jnp.broadcast_to(jnp.sum(p, axis=-1, keepdims=True), (r2, LANES))
            acc_sc[h] = _bdot(p, vn)

    @pl.when(j >= nsk)
    def _():
        for h in range(kvh):
            m = m_sc[h][:, 0:1]
            l = l_sc[h][:, 0:1]
            acc = acc_sc[h]
            for bb in range(bps):
                n = (j - nsk) * bps + bb
                sel = ((idx_sc[h * MOBA_TOPK][:, 0:1] == n) | (idx_sc[h * MOBA_TOPK + 1][:, 0:1] == n)
                       | (idx_sc[h * MOBA_TOPK + 2][:, 0:1] == n))
                sn = jnp.where(sel, s_sc[n, h], -jnp.inf)
                m_new = jnp.maximum(m, jnp.max(sn, axis=-1, keepdims=True))
                alpha = jnp.exp(m - m_new)
                pn = jnp.exp(sn - m_new)
                l = alpha * l + jnp.sum(pn, axis=-1, keepdims=True)
                acc = alpha * acc + _bdot(pn, block_of(v_pages, bb, h))
                m = m_new
            m_sc[h] = jnp.broadcast_to(m, (r2, LANES))
            l_sc[h] = jnp.broadcast_to(l, (r2, LANES))
            acc_sc[h] = acc

    @pl.when(j == 2 * nsk - 1)
    def _():
        for h in range(kvh):
            o = _rms_heads(acc_sc[h] / l_sc[h][:, 0:1], g_ref[...])
            o_ref[:, (2 * h) * HEAD_DIM:(2 * h + 1) * HEAD_DIM] = o[:QROWS]
            o_ref[:, (2 * h + 1) * HEAD_DIM:(2 * h + 2) * HEAD_DIM] = o[QROWS:]


def _moba_sample_call(page_table, q_rot, k_rot, proj, cache_k, cache_v, gnorm, layer,
                      db, row0, kvh, v_col0, t_len):
    n_pages = page_table.shape[1]
    nb = n_pages * PAGE_SIZE // MOBA_BLOCK
    ppb = MOBA_BLOCK // PAGE_SIZE
    bps = _pick(nb, (4, 2, 1))
    nsk = nb // bps
    npg = bps * ppb
    qw = 2 * kvh * HEAD_DIM
    kw = kvh * HEAD_DIM
    blk0 = row0 // SUB

    def kmap(t):
        return lambda b, j, pt: (layer, pt[b, npg * jnp.minimum(j, nsk - 1) + t], 0, 0)

    def vmap_(t):
        return lambda b, j, pt: (layer, pt[b, npg * jnp.maximum(j - nsk, 0) + t], 0, 0)

    depth, n_pool = cache_k.shape[:2]
    cache_k = cache_k.reshape(depth, n_pool, PAGE_SIZE * kvh, HEAD_DIM)
    cache_v = cache_v.reshape(depth, n_pool, PAGE_SIZE * kvh, HEAD_DIM)
    page_spec = lambda fn: pl.BlockSpec((None, None, PAGE_SIZE * kvh, HEAD_DIM), fn)
    r2 = 2 * QROWS
    need = 4 * npg * PAGE_SIZE * kvh * HEAD_DIM * 4 + nb * kvh * r2 * MOBA_BLOCK * 4 + (8 << 20)
    grid_spec = pltpu.PrefetchScalarGridSpec(
        num_scalar_prefetch=1,
        grid=(db, 2 * nsk),
        in_specs=[pl.BlockSpec((SUB, qw), lambda b, j, pt: (blk0 + b, 0)),
                  pl.BlockSpec((SUB, kw), lambda b, j, pt: (blk0 + b, 0)),
                  pl.BlockSpec((SUB, kw), lambda b, j, pt: (blk0 + b, v_col0 // kw))]
                 + [page_spec(kmap(t)) for t in range(npg)] + [page_spec(vmap_(t)) for t in range(npg)]
                 + [pl.BlockSpec((1, HEAD_DIM), lambda b, j, pt: (0, 0))],
        out_specs=pl.BlockSpec((QROWS, qw), lambda b, j, pt: (b, 0)),
        scratch_shapes=[pltpu.VMEM((kvh, r2, HEAD_DIM), f32),
                        pltpu.VMEM((nb, kvh, r2, MOBA_BLOCK), f32),
                        pltpu.VMEM((kvh, r2, LANES), f32),
                        pltpu.VMEM((kvh * MOBA_TOPK, r2, LANES), jnp.int32),
                        pltpu.VMEM((kvh, r2, LANES), f32),
                        pltpu.VMEM((kvh, r2, LANES), f32),
                        pltpu.VMEM((kvh, r2, HEAD_DIM), f32)])
    return pl.pallas_call(
        functools.partial(_moba_sample_kernel, nb=nb, bps=bps, kvh=kvh, t_len=t_len),
        grid_spec=grid_spec,
        out_shape=jax.ShapeDtypeStruct((db * QROWS, qw), f32),
        compiler_params=_cparams(("arbitrary", "arbitrary"), need),
        name="moba_sample",
    )(page_table, q_rot, k_rot, proj, *([cache_k] * npg), *([cache_v] * npg), gnorm.reshape(1, HEAD_DIM))


def _inv_unit_lower(lmat, nblk):
    c = lmat.shape[0]
    ii = _iota((c, c), 0)
    jj = _iota((c, c), 1)
    eye = (ii == jj).astype(f32)
    same = (ii // INV_BLK) == (jj // INV_BLK)
    dmat = jnp.where(same, lmat, 0.0)
    rmat = lmat - dmat
    p = -dmat
    x = eye + p
    k = 2
    while k < INV_BLK:
        p = _dot3(p, p)
        x = x + _dot3(x, p)
        k *= 2
    if nblk == 1:
        return x
    mm = _dot3(x, rmat)
    y = eye - mm
    pm = mm
    k = 2
    while k < nblk:
        pm = _dot3(pm, pm)
        y = y + _dot3(y, pm)
        k *= 2
    return _dot3(y, x)


def _gdn_kernel(main_ref, ab_ref, cw_ref, prev_ref, alog_ref, dtb_ref, s0_ref, g_ref, o_ref, s_ref, carry_sc,
                *, heads, group, col0, t_valid):
    c_idx = pl.program_id(1)
    c = main_ref.shape[0]
    gw = heads * HEAD_DIM
    ch = 3 * gw

    @pl.when(c_idx == 0)
    def _():
        s_ref[...] = s0_ref[...]
        carry_sc[...] = prev_ref[...]

    x = main_ref[:, col0:col0 + ch]
    xp = jnp.concatenate([carry_sc[...], x], axis=0)
    y = cw_ref[GDN_CONV - 1:GDN_CONV, :] * x
    for tap in range(1, GDN_CONV):
        y = y + cw_ref[GDN_CONV - 1 - tap:GDN_CONV - tap, :] * pltpu.roll(xp, tap, 0)[SUBLANES:, :]
    carry_sc[...] = x[c - SUBLANES:, :]
    y = _silu(y)

    ab = ab_ref[...]
    lane = _iota(ab.shape, 1)
    row_ok = (c_idx * c + _iota((c, 1), 0)) < t_valid
    g_all = jnp.where(row_ok & (lane < heads), -jnp.exp(alog_ref[...]) * _softplus(ab + dtb_ref[...]), 0.0)
    beta_all = jnp.where(row_ok, jax.nn.sigmoid(ab), 0.0)
    tri = (_iota((c, c), 0) >= _iota((c, c), 1)).astype(bf16)
    gam_all = _sel_dot(tri, g_all)
    eye_l = (_iota((LANES, LANES), 0) == _iota((LANES, LANES), 1)).astype(bf16)
    gam_t = _sel_dot_nt(eye_l, gam_all)
    gnorm = g_ref[...]

    r = group * c
    ii = _iota((r, r), 0)
    jj = _iota((r, r), 1)
    same = (ii // c) == (jj // c)
    low = same & (ii >= jj)
    strict = same & (ii > jj)
    for g in range(heads // group):
        hs = range(g * group, (g + 1) * group)
        stack = lambda f: jnp.concatenate([f(h) for h in hs], axis=0)
        xq = stack(lambda h: y[:, h * HEAD_DIM:(h + 1) * HEAD_DIM])
        xk = stack(lambda h: y[:, gw + h * HEAD_DIM:gw + (h + 1) * HEAD_DIM])
        v = stack(lambda h: y[:, 2 * gw + h * HEAD_DIM:2 * gw + (h + 1) * HEAD_DIM])
        q = xq * lax.rsqrt(jnp.sum(xq * xq, axis=-1, keepdims=True) + EPS) * (HEAD_DIM ** -0.5)
        k = xk * lax.rsqrt(jnp.sum(xk * xk, axis=-1, keepdims=True) + EPS)
        beta = stack(lambda h: beta_all[:, heads + h:heads + h + 1])
        gcol = stack(lambda h: gam_all[:, h:h + 1])
        glast = stack(lambda h: jnp.broadcast_to(gam_all[c - 1:c, h:h + 1], (c, 1)))
        grow = jnp.concatenate([gam_t[h:h + 1, :] for h in hs], axis=1)
        egam = jnp.exp(gcol)
        decay = jnp.exp(jnp.where(low, gcol - grow, -jnp.inf))
        lmat = jnp.where(strict, beta * _bdot_nt(k, k) * decay, 0.0)
        tmat = _inv_unit_lower(lmat, c // INV_BLK)
        u = _bdot(tmat, beta * v)
        w = _bdot(tmat, (beta * egam) * k)
        qk = _bdot_nt(q, k) * decay
        qe = q * egam
        kd = k * jnp.exp(glast - gcol)
        vn_parts, oi_parts = [], []
        for a, h in enumerate(hs):
            rows = slice(a * c, (a + 1) * c)
            s = s_ref[h]
            vn_parts.append(u[rows, :] - _bdot(w[rows, :], s))
            oi_parts.append(_bdot(qe[rows, :], s))
        v_new = jnp.concatenate(vn_parts, axis=0)
        o = jnp.concatenate(oi_parts, axis=0) + _bdot(qk, v_new)
        for a, h in enumerate(hs):
            rows = slice(a * c, (a + 1) * c)
            s_ref[h] = jnp.exp(gam_all[c - 1:c, h:h + 1]) * s_ref[h] + _bdot_tn(kd[rows, :], v_new[rows, :])
            z = main_ref[:, col0 + ch + h * HEAD_DIM:col0 + ch + (h + 1) * HEAD_DIM]
            o_ref[:, h * HEAD_DIM:(h + 1) * HEAD_DIM] = (_rms_heads(o[rows, :], gnorm) * _silu(z)).astype(o_ref.dtype)


def _gdn_call(proj, tail, cw, prev8, alog, dtb, s0, gnorm, layer, *, nseq, rows_per_seq, row0, chunk, heads,
              col0, ab_col0, t_valid, name):
    mainw = proj.shape[1]
    gw = heads * HEAD_DIM
    ch = 3 * gw
    nch = rows_per_seq // chunk
    blk0 = row0 // chunk
    need = 2 * chunk * mainw * 4 + 4 * heads * HEAD_DIM * HEAD_DIM * 4 + 40 * chunk * ch * 4 + (4 << 20)
    return pl.pallas_call(
        functools.partial(_gdn_kernel, heads=heads, group=_pick(heads, (4, 3, 2, 1)), col0=col0, t_valid=t_valid),
        grid=(nseq, nch),
        in_specs=[pl.BlockSpec((chunk, mainw), lambda b, c: (blk0 + b * nch + c, 0)),
                  pl.BlockSpec((chunk, LANES), lambda b, c: (blk0 + b * nch + c, ab_col0 // LANES)),
                  pl.BlockSpec((None, GDN_CONV, ch), lambda b, c: (layer, 0, 0)),
                  pl.BlockSpec((None, SUBLANES, ch), lambda b, c: (b, 0, 0)),
                  pl.BlockSpec((1, LANES), lambda b, c: (0, 0)),
                  pl.BlockSpec((1, LANES), lambda b, c: (0, 0)),
                  pl.BlockSpec((None, heads, HEAD_DIM, HEAD_DIM), lambda b, c: (b, 0, 0, 0)),
                  pl.BlockSpec((1, HEAD_DIM), lambda b, c: (0, 0))],
        out_specs=[pl.BlockSpec((chunk, gw), lambda b, c: (b * nch + c, 0)),
                   pl.BlockSpec((None, heads, HEAD_DIM, HEAD_DIM), lambda b, c: (b, 0, 0, 0))],
        out_shape=[jax.ShapeDtypeStruct((nseq * rows_per_seq, gw), bf16),
                   jax.ShapeDtypeStruct((nseq, heads, HEAD_DIM, HEAD_DIM), f32)],
        scratch_shapes=[pltpu.VMEM((SUBLANES, ch), f32)],
        compiler_params=_cparams(("arbitrary", "arbitrary"), need),
        name=name,
    )(proj, tail, cw, prev8, alog, dtb, s0, gnorm.reshape(1, HEAD_DIM))


def _gla_kernel(tail_ref, wg_ref, bg_ref, s0_ref, g_ref, o_ref, s_ref, *, heads, ab_col0, t_valid):
    c_idx = pl.program_id(1)
    c = tail_ref.shape[0]
    kw = heads * GLA_DK
    vw = heads * HEAD_DIM

    @pl.when(c_idx == 0)
    def _():
        s_ref[...] = s0_ref[...]

    row_ok = (c_idx * c + _iota((c, 1), 0)) < t_valid
    fblk = tail_ref[:, ab_col0:ab_col0 + LANES]
    pre = _dot3(fblk, wg_ref[...]) + bg_ref[...]
    log_a = jnp.where(row_ok, -_softplus(-pre) * (1.0 / GLA_TAU), 0.0)
    tri = (_iota((c, c), 0) >= _iota((c, c), 1)).astype(bf16)
    bc_all = _sel_dot(tri, log_a)
    gnorm = g_ref[...]
    lane = _iota((c, LANES), 1)
    first = lane < GLA_DK
    ii = _iota((c, LANES), 0)
    jj = lane % GLA_DK
    rowid = _iota((c, 1), 0)
    eye_l = _iota((LANES, LANES), 0) == _iota((LANES, LANES), 1)
    seg = (((_iota((2 * LANES, LANES), 0) % LANES) < GLA_DK) == (_iota((2 * LANES, LANES), 1) < GLA_DK)).astype(bf16)
    zpad = jnp.zeros((GLA_DK - c, LANES), f32) if c < GLA_DK else None

    def stack_pair(a):
        a0 = jnp.where(first, a, 0.0)
        a1 = jnp.where(first, 0.0, a)
        parts = [a0, a1] if zpad is None else [a0, zpad, a1, zpad]
        return jnp.concatenate(parts, axis=0)

    for p in range(heads // 2):
        ls = slice(p * LANES, (p + 1) * LANES)
        q = tail_ref[:, ls] * (GLA_DK ** -0.5)
        k = jnp.where(row_ok, tail_ref[:, kw + p * LANES:kw + (p + 1) * LANES], 0.0)
        v0 = tail_ref[:, 2 * kw + (2 * p) * HEAD_DIM:2 * kw + (2 * p + 1) * HEAD_DIM]
        v1 = tail_ref[:, 2 * kw + (2 * p + 1) * HEAD_DIM:2 * kw + (2 * p + 2) * HEAD_DIM]
        r0_ = tail_ref[:, 2 * kw + vw + (2 * p) * HEAD_DIM:2 * kw + vw + (2 * p + 1) * HEAD_DIM]
        r1_ = tail_ref[:, 2 * kw + vw + (2 * p + 1) * HEAD_DIM:2 * kw + vw + (2 * p + 2) * HEAD_DIM]
        vparts = [v0, v1] if zpad is None else [v0, zpad, v1, zpad]
        v2 = jnp.concatenate(vparts, axis=0).astype(bf16)
        bc = bc_all[:, ls]
        s = s_ref[p]
        qe = q * jnp.exp(bc)
        o0 = _bdot(jnp.where(first, qe, 0.0), s)
        o1 = _bdot(jnp.where(first, 0.0, qe), s)

        pieces = [jnp.zeros((GLA_SUB, LANES), f32)]
        for sb in range(1, c // GLA_SUB):
            r0 = sb * GLA_SUB
            rs = slice(r0, r0 + GLA_SUB)
            b0 = bc[r0:r0 + 1, :]
            q_i = q[rs, :] * jnp.exp(bc[rs, :] - b0)
            k_j = k * jnp.exp(jnp.where(rowid < r0, b0 - bc, -jnp.inf))
            pieces.append(_bdot_nt(q_i, stack_pair(k_j)))
        att_off = jnp.concatenate(pieces, axis=0)

        att = jnp.zeros((c, LANES), f32)
        for dlt in range(GLA_SUB):
            k_r = pltpu.roll(k, dlt, 0) if dlt else k
            bc_r = pltpu.roll(bc, dlt, 0) if dlt else bc
            ok = (rowid % GLA_SUB) >= dlt
            pr = q * k_r * jnp.exp(jnp.where(ok, bc - bc_r, -jnp.inf))
            hi, lo = _split2(pr)
            sums = jnp.dot(jnp.concatenate([hi, lo], axis=1), seg, preferred_element_type=f32)
            att = jnp.where(jj == ii - dlt, sums, att)
        att = att + att_off

        o0 = o0 + _bdot(jnp.where(first, att, 0.0), v2)
        o1 = o1 + _bdot(jnp.where(first, 0.0, att), v2)

        bl = bc[c - 1:c, :]
        ebl_col = jnp.sum(jnp.where(eye_l, jnp.exp(bl), 0.0), axis=-1, keepdims=True)
        s_ref[p] = ebl_col * s + _bdot_tn(stack_pair(k * jnp.exp(bl - bc)), v2)
        o_ref[:, (2 * p) * HEAD_DIM:(2 * p + 1) * HEAD_DIM] = (_rms_heads(o0, gnorm) * _silu(r0_)).astype(o_ref.dtype)
        o_ref[:, (2 * p + 1) * HEAD_DIM:(2 * p + 2) * HEAD_DIM] = (_rms_heads(o1, gnorm) * _silu(r1_)).astype(o_ref.dtype)


def _gla_call(tail, wg_pad, bg, s0, gnorm, *, nseq, rows_per_seq, row0, chunk, heads, ab_col0, t_valid, name):
    tailw = tail.shape[1]
    kw = heads * GLA_DK
    vw = heads * HEAD_DIM
    nch = rows_per_seq // chunk
    blk0 = row0 // chunk
    assert heads % 2 == 0 and chunk <= GLA_DK and chunk % GLA_SUB == 0
    s0 = s0.reshape(nseq, heads // 2, 2 * GLA_DK, HEAD_DIM)
    need = 2 * chunk * tailw * 4 + 4 * heads * GLA_DK * HEAD_DIM * 4 + 2 * LANES * kw * 4 + 40 * chunk * tailw * 4
    o, s_fin = pl.pallas_call(
        functools.partial(_gla_kernel, heads=heads, ab_col0=ab_col0, t_valid=t_valid),
        grid=(nseq, nch),
        in_specs=[pl.BlockSpec((chunk, tailw), lambda b, c: (blk0 + b * nch + c, 0)),
                  pl.BlockSpec((LANES, kw), lambda b, c: (0, 0)),
                  pl.BlockSpec((1, kw), lambda b, c: (0, 0)),
                  pl.BlockSpec((None, heads // 2, 2 * GLA_DK, HEAD_DIM), lambda b, c: (b, 0, 0, 0)),
                  pl.BlockSpec((1, HEAD_DIM), lambda b, c: (0, 0))],
        out_specs=[pl.BlockSpec((chunk, vw), lambda b, c: (b * nch + c, 0)),
                   pl.BlockSpec((None, heads // 2, 2 * GLA_DK, HEAD_DIM), lambda b, c: (b, 0, 0, 0))],
        out_shape=[jax.ShapeDtypeStruct((nseq * rows_per_seq, vw), bf16),
                   jax.ShapeDtypeStruct((nseq, heads // 2, 2 * GLA_DK, HEAD_DIM), f32)],
        compiler_params=_cparams(("arbitrary", "arbitrary"), need),
        name=name,
    )(tail, wg_pad, bg.reshape(1, kw), s0, gnorm.reshape(1, HEAD_DIM))
    return o, s_fin.reshape(nseq, heads, GLA_DK, HEAD_DIM)


def kernel(x_prompt, x_sample, c_prompt, c_sample, cache_k, cache_v, page_table, state_gdn, state_gdn_conv, state_gla, ada_w, ada_b, norm_ffn_a, ffn_a_wg, ffn_a_wu, ffn_a_wd, norm_mix, w_in, moba_norm, gdn_conv_w, gdn_a_log, gdn_dt_bias, gdn_norm, gla_w_gate, gla_b_gate, gla_norm, w_out, norm_ffn_b, ffn_b_wg, ffn_b_wu, ffn_b_wd, final_norm):
    b, s_len, d = x_prompt.shape
    db, t_dec, _ = x_sample.shape
    depth = ada_w.shape[0]
    n_pages = page_table.shape[1]
    past_len = n_pages * PAGE_SIZE

    n_heads = d // HEAD_DIM
    moba_h = n_heads // 4
    kvh = moba_h // 2
    gdn_h = (3 * n_heads) // 8
    gla_h = n_heads - moba_h - gdn_h
    qw, kvw = moba_h * HEAD_DIM, kvh * HEAD_DIM
    gdn_w, gla_kw, gla_w = gdn_h * HEAD_DIM, gla_h * GLA_DK, gla_h * HEAD_DIM
    main_w = qw + 2 * kvw + 4 * gdn_w
    small0 = main_w
    gla0 = main_w + 2 * gdn_h
    f0 = gla0 + 2 * gla_kw + 2 * gla_w
    tail_main = 2 * gla_kw + 2 * gla_w
    assert w_in.shape[2] == f0 + GLA_GATE_RANK
    assert s_len % TM == 0 and s_len % CHUNK == 0 and s_len >= MOBA_TOPK * MOBA_BLOCK
    assert (db * SUB) % TM == 0 and b + db <= 16
    assert GDN_CONV - 1 <= t_dec <= QROWS and past_len % MOBA_BLOCK == 0
    assert 2 * gdn_h + GLA_GATE_RANK <= LANES and tail_main % LANES == 0

    bs = b * s_len
    m_tot = bs + db * SUB
    n_tiles_p = bs // TM

    def pack_rows(p_rows, s_rows):
        s_pad = jnp.pad(s_rows, ((0, 0), (0, SUB - s_rows.shape[1]), (0, 0)))
        return jnp.concatenate([p_rows, s_pad.reshape(db * SUB, s_rows.shape[2])], axis=0)

    def sample_rows(a):
        return a[bs:].reshape(db, SUB, a.shape[1])[:, :t_dec]

    x = pack_rows(x_prompt.reshape(bs, d), x_sample)

    c16 = jnp.concatenate([c_prompt, c_sample, jnp.zeros((16 - b - db, d), f32)], axis=0)
    mod = _mod_call(c16, ada_w, ada_b)

    half = ROPE_DIM // 2
    inv_freq = ROPE_THETA ** (-jnp.arange(half, dtype=f32) / half)
    pos_s = past_len + jnp.minimum(jnp.arange(SUB), t_dec - 1)
    pos = jnp.concatenate([jnp.tile(jnp.arange(s_len), b), jnp.tile(pos_s, db)]).astype(f32)
    ang = pos[:, None] * inv_freq[None, :]
    ones = jnp.ones((m_tot, HEAD_DIM - ROPE_DIM), f32)
    cos_t = jnp.concatenate([jnp.cos(ang), jnp.cos(ang), ones], axis=1)
    sin_t = jnp.concatenate([-jnp.sin(ang), jnp.sin(ang), 0.0 * ones], axis=1)

    tail_w = -(-(tail_main + LANES) // 512) * 512
    zeros_prev = jnp.zeros((b, SUBLANES, 3 * gdn_w), f32)

    outs = {k: [] for k in ("kp", "vp", "ks", "vs", "gp", "gs", "cp", "cs", "lp", "ls")}
    for l in range(depth):
        mod8 = jnp.concatenate([jnp.repeat(mod[l, :b], (s_len // TM) * (TM // SUB), axis=0), mod[l, b:b + db]], axis=0)

        h = _premod_call(x, norm_ffn_a[l], mod8, 0, 1)
        a = _mm_up_call(h, ffn_a_wg, ffn_a_wu, l)
        x = _mm_res_call(a, ffn_a_wd, l, x, mod8, 2, 0.5, "ffn_a_down")

        h = _premod_call(x, norm_mix[l], mod8, 3, 4)
        proj = _mm_plain_call(h, w_in, l, main_w, "w_in_main")
        w_tail = jnp.concatenate([w_in[l][:, gla0:f0], w_in[l][:, small0:gla0], w_in[l][:, f0:],
                                  jnp.zeros((d, tail_w - tail_main - 2 * gdn_h - GLA_GATE_RANK), f32)], axis=1)
        tail = _mm_plain_call(h, w_tail[None], 0, tail_w, "w_in_tail")

        q_rot, k_rot = _rope_call(proj, cos_t, sin_t, qw, kvw)
        v_col0 = qw + kvw
        om_p = _moba_prompt_call(q_rot, k_rot, proj, moba_norm[l], b, s_len, kvh, v_col0)
        om_s = _moba_sample_call(page_table, q_rot, k_rot, proj, cache_k, cache_v, moba_norm[l], l,
                                 db, bs, kvh, v_col0, t_dec)
        om_s = om_s.reshape(db, QROWS, qw)[:, :t_dec].astype(bf16)

        col0 = qw + 2 * kvw
        alog = jnp.zeros((1, LANES), f32).at[0, :gdn_h].set(gdn_a_log[l])
        dtb = jnp.zeros((1, LANES), f32).at[0, :gdn_h].set(gdn_dt_bias[l])
        od_p, gp = _gdn_call(proj, tail, gdn_conv_w, zeros_prev, alog, dtb,
                             jnp.zeros((b, gdn_h, HEAD_DIM, HEAD_DIM), f32), gdn_norm[l], l,
                             nseq=b, rows_per_seq=s_len, row0=0, chunk=CHUNK, heads=gdn_h, col0=col0,
                             ab_col0=tail_main, t_valid=s_len, name="gdn_prompt")
        prev_s = jnp.pad(state_gdn_conv[l], ((0, 0), (SUBLANES - (GDN_CONV - 1), 0), (0, 0)))
        od_s, gs = _gdn_call(proj, tail, gdn_conv_w, prev_s, alog, dtb, state_gdn[l], gdn_norm[l], l,
                             nseq=db, rows_per_seq=SUB, row0=bs, chunk=SUB, heads=gdn_h, col0=col0,
                             ab_col0=tail_main, t_valid=t_dec, name="gdn_sample")

        wg_pad = jnp.zeros((LANES, gla_kw), f32).at[2 * gdn_h:2 * gdn_h + GLA_GATE_RANK].set(gla_w_gate[l])
        ol_p, lp = _gla_call(tail, wg_pad, gla_b_gate[l], jnp.zeros((b, gla_h, GLA_DK, HEAD_DIM), f32), gla_norm[l],
                             nseq=b, rows_per_seq=s_len, row0=0, chunk=CHUNK, heads=gla_h, ab_col0=tail_main,
                             t_valid=s_len, name="gla_prompt")
        ol_s, ls = _gla_call(tail, wg_pad, gla_b_gate[l], state_gla[l], gla_norm[l],
                             nseq=db, rows_per_seq=SUB, row0=bs, chunk=SUB, heads=gla_h, ab_col0=tail_main,
                             t_valid=t_dec, name="gla_sample")

        o_p = jnp.concatenate([om_p, od_p, ol_p], axis=1)
        o_s = jnp.concatenate([om_s, od_s.reshape(db, SUB, gdn_w)[:, :t_dec], ol_s.reshape(db, SUB, gla_w)[:, :t_dec]], axis=2)
        o_mix = pack_rows(o_p, o_s)
        x = _mm_res_call(o_mix, w_out, l, x, mod8, 5, 1.0, "w_out")

        h = _premod_call(x, norm_ffn_b[l], mod8, 6, 7)
        a = _mm_up_call(h, ffn_b_wg, ffn_b_wu, l)
        x = _mm_res_call(a, ffn_b_wd, l, x, mod8, 8, 0.5, "ffn_b_down")

        conv_cols = proj[:, col0:col0 + 3 * gdn_w]
        outs["kp"].append(k_rot[:bs].reshape(b, s_len, kvh, HEAD_DIM))
        outs["vp"].append(proj[:bs, v_col0:v_col0 + kvw].reshape(b, s_len, kvh, HEAD_DIM))
        outs["ks"].append(sample_rows(k_rot).reshape(db, t_dec, kvh, HEAD_DIM))
        outs["vs"].append(sample_rows(proj[:, v_col0:v_col0 + kvw]).reshape(db, t_dec, kvh, HEAD_DIM))
        outs["gp"].append(gp)
        outs["gs"].append(gs)
        outs["cp"].append(conv_cols[:bs].reshape(b, s_len, 3 * gdn_w)[:, s_len - (GDN_CONV - 1):])
        outs["cs"].append(sample_rows(conv_cols)[:, t_dec - (GDN_CONV - 1):])
        outs["lp"].append(lp)
        outs["ls"].append(ls)

    y = _rms_call(x, final_norm)
    st = {k: jnp.stack(v) for k, v in outs.items()}
    return (y[:bs].reshape(b, s_len, d), sample_rows(y),
            st["kp"], st["vp"], st["ks"], st["vs"], st["gp"], st["gs"], st["cp"], st["cs"], st["lp"], st["ls"])
```

```python
import functools
import math

import jax
import jax.numpy as jnp
from jax import lax
from jax.experimental import pallas as pl
from jax.experimental.pallas import tpu as pltpu

f32 = jnp.float32
bf16 = jnp.bfloat16

HEAD_DIM = 128
MOBA_BLOCK = 256
MOBA_TOPK = 3
ROPE_THETA = 500000.0
ROPE_DIM = HEAD_DIM // 4
GDN_CONV = 4
GLA_DK = HEAD_DIM // 2
GLA_GATE_RANK = 16
GLA_TAU = 16.0
GLA_SUB = 16
INV_BLK = 16
CHUNK = 64
PAGE_SIZE = 128
N_MOD = 9
EPS = 1e-6

LANES = 128
SUBLANES = 8
TM = 256
SUB = 32
V7X_VMEM_BYTES = 64 * 1024 * 1024
VMEM_BUDGET = 58 * 1024 * 1024


def _cparams(sem, need_bytes):
    limit = int(min(max(need_bytes * 1.25 + (4 << 20), 16 << 20), VMEM_BUDGET))
    return pltpu.CompilerParams(dimension_semantics=sem, vmem_limit_bytes=limit)


def _pick(n, prefs):
    for p in prefs:
        if n % p == 0:
            return p
    raise ValueError(f"no tile in {prefs} divides {n}")


def _bdot(a, b):
    return jnp.dot(a.astype(bf16), b.astype(bf16), preferred_element_type=f32)


def _bdot_nt(a, b):
    return lax.dot_general(a.astype(bf16), b.astype(bf16), (((1,), (1,)), ((), ())),
                           preferred_element_type=f32)


def _bdot_tn(a, b):
    return lax.dot_general(a.astype(bf16), b.astype(bf16), (((0,), (0,)), ((), ())),
                           preferred_element_type=f32)


def _split2(a):
    hi = a.astype(bf16)
    lo = (a - hi.astype(f32)).astype(bf16)
    return hi, lo


def _split3(a):
    hi = a.astype(bf16)
    r = a - hi.astype(f32)
    mid = r.astype(bf16)
    lo = (r - mid.astype(f32)).astype(bf16)
    return hi, mid, lo


def _dot3(a, b):
    ah, al = _split2(a)
    bh, bl = _split2(b)
    d = functools.partial(jnp.dot, preferred_element_type=f32)
    return d(ah, bh) + (d(ah, bl) + d(al, bh))


def _dot3_nt(a, b):
    ah, al = _split2(a)
    bh, bl = _split2(b)
    d = functools.partial(lax.dot_general, dimension_numbers=(((1,), (1,)), ((), ())),
                          preferred_element_type=f32)
    return d(ah, bh) + (d(ah, bl) + d(al, bh))


def _sel_dot(sel, b):
    bh, bm, bl = _split3(b)
    d = functools.partial(jnp.dot, preferred_element_type=f32)
    return d(sel, bh) + (d(sel, bm) + d(sel, bl))


def _sel_dot_nt(sel, b):
    bh, bm, bl = _split3(b)
    d = functools.partial(lax.dot_general, dimension_numbers=(((1,), (1,)), ((), ())),
                          preferred_element_type=f32)
    return d(sel, bh) + (d(sel, bm) + d(sel, bl))


def _silu(x):
    return x * jax.nn.sigmoid(x)


def _softplus(x):
    return jnp.maximum(x, 0.0) + jnp.log(1.0 + jnp.exp(-jnp.abs(x)))


def _iota(shape, dim):
    return lax.broadcasted_iota(jnp.int32, shape, dim)


def _rms_heads(o, g):
    return o * lax.rsqrt(jnp.mean(o * o, axis=-1, keepdims=True) + EPS) * g


def _mod_kernel(c_ref, w_ref, b_ref, o_ref):
    c = c_ref[...]
    o_ref[...] = _bdot(_silu(c), w_ref[...]) + b_ref[...]


def _mod_call(c16, ada_w, ada_b):
    depth, d, n = ada_w.shape
    tn = _pick(n, (512, 256, 128))
    need = 2 * d * tn * 4 + d * tn * 2 + 4 * 16 * tn * 4 + 2 * 16 * d * 4
    return pl.pallas_call(
        _mod_kernel,
        grid=(depth, n // tn),
        in_specs=[pl.BlockSpec((16, d), lambda l, j: (0, 0)),
                  pl.BlockSpec((None, d, tn), lambda l, j: (l, 0, j)),
                  pl.BlockSpec((None, 1, tn), lambda l, j: (l, 0, j))],
        out_specs=pl.BlockSpec((None, 16, tn), lambda l, j: (l, 0, j)),
        out_shape=jax.ShapeDtypeStruct((depth, 16, n), f32),
        compiler_params=_cparams(("arbitrary", "arbitrary"), need),
        name="adaln_mod",
    )(c16, ada_w, ada_b.reshape(depth, 1, n))


def _premod_kernel(x_ref, g_ref, sh_ref, sc_ref, o_ref):
    g = g_ref[...]
    for s in range(TM // SUB):
        rows = slice(s * SUB, (s + 1) * SUB)
        xs = x_ref[rows, :]
        y = xs * lax.rsqrt(jnp.mean(xs * xs, axis=-1, keepdims=True) + EPS) * g
        o_ref[rows, :] = (y * (1.0 + sc_ref[s:s + 1, :]) + sh_ref[s:s + 1, :]).astype(o_ref.dtype)


def _premod_call(x, g, mod8, v_shift, v_scale):
    m, d = x.shape
    nsub = TM // SUB
    need = 2 * TM * d * 4 + 2 * TM * d * 2 + 6 * nsub * d * 4
    return pl.pallas_call(
        _premod_kernel,
        grid=(m // TM,),
        in_specs=[pl.BlockSpec((TM, d), lambda i: (i, 0)),
                  pl.BlockSpec((1, d), lambda i: (0, 0)),
                  pl.BlockSpec((nsub, d), lambda i: (i, v_shift)),
                  pl.BlockSpec((nsub, d), lambda i: (i, v_scale))],
        out_specs=pl.BlockSpec((TM, d), lambda i: (i, 0)),
        out_shape=jax.ShapeDtypeStruct((m, d), bf16),
        compiler_params=_cparams(("arbitrary",), need),
        name="modulate",
    )(x, g.reshape(1, d), mod8, mod8)


def _pack_kernel(p_ref, s_ref, o_ref, *, n_p):
    i = pl.program_id(0)

    @pl.when(i < n_p)
    def _():
        o_ref[...] = p_ref[...]

    @pl.when(i >= n_p)
    def _():
        o_ref[...] = s_ref[...]


def _pack_call(p_rows, s_rows):
    bs, d = p_rows.shape
    n_p, n_s = bs // TM, s_rows.shape[0] // TM
    return pl.pallas_call(
        functools.partial(_pack_kernel, n_p=n_p),
        grid=(n_p + n_s,),
        in_specs=[pl.BlockSpec((TM, d), lambda i: (jnp.minimum(i, n_p - 1), 0)),
                  pl.BlockSpec((TM, d), lambda i: (jnp.maximum(i - n_p, 0), 0))],
        out_specs=pl.BlockSpec((TM, d), lambda i: (i, 0)),
        out_shape=jax.ShapeDtypeStruct((bs + s_rows.shape[0], d), p_rows.dtype),
        compiler_params=_cparams(("arbitrary",), 6 * TM * d * 4),
        name="pack_tokens",
    )(p_rows, s_rows)


def _rms_kernel(x_ref, g_ref, o_ref):
    x = x_ref[...]
    o_ref[...] = x * lax.rsqrt(jnp.mean(x * x, axis=-1, keepdims=True) + EPS) * g_ref[...]


def _rms_call(x, g):
    m, d = x.shape
    return pl.pallas_call(
        _rms_kernel,
        grid=(m // TM,),
        in_specs=[pl.BlockSpec((TM, d), lambda i: (i, 0)), pl.BlockSpec((1, d), lambda i: (0, 0))],
        out_specs=pl.BlockSpec((TM, d), lambda i: (i, 0)),
        out_shape=jax.ShapeDtypeStruct((m, d), f32),
        compiler_params=_cparams(("arbitrary",), 4 * TM * d * 4),
        name="final_norm",
    )(x, g.reshape(1, d))


MM_VMEM_TARGET = 48 << 20


V7X_BF16_FLOPS = 1.15e15
V7X_HBM_BYTES_PER_S = 3.0e12
STEP_OVERHEAD_S = 0.35e-6
MXU_EFF_BY_TM = {768: 0.85, 512: 0.80, 256: 0.70}


def _ws_plan(m, k, n, n_weights, tile_bytes_per_out_elem, io_bytes_per_out_elem, tns=(1024, 512, 256, 128)):
    best = None
    for tn in tns:
        if n % tn:
            continue
        for tm in MXU_EFF_BY_TM:
            if m % tm:
                continue
            for bufs in (2, 1):
                need = n_weights * k * tn * (4 * bufs + 2) + 2 * tm * k * 2 + tm * tn * tile_bytes_per_out_elem
                if need > MM_VMEM_TARGET:
                    continue
                w_bytes = n_weights * k * n * 4
                hbm = (n // tn) * m * k * 2 + w_bytes + m * n * io_bytes_per_out_elem
                t = max(2.0 * n_weights * m * k * n / V7X_BF16_FLOPS / MXU_EFF_BY_TM[tm], hbm / V7X_HBM_BYTES_PER_S)
                t += (n // tn) * (m // tm) * STEP_OVERHEAD_S
                if bufs == 1:
                    t += w_bytes / V7X_HBM_BYTES_PER_S
                if best is None or t < best[0]:
                    best = (t, tm, tn, bufs, need)
    if best is None:
        raise ValueError(f"no weight-stationary tiling for {(m, k, n)}")
    return best[1:]


def _wspec(shape, index_map, bufs):
    if bufs == 1:
        return pl.BlockSpec(shape, index_map, pipeline_mode=pl.Buffered(1))
    return pl.BlockSpec(shape, index_map)


def _mm_plain_kernel(a_ref, w_ref, o_ref, wb_ref):
    @pl.when(pl.program_id(1) == 0)
    def _():
        wb_ref[...] = w_ref[...].astype(bf16)

    o_ref[...] = jnp.dot(a_ref[...], wb_ref[...], preferred_element_type=f32).astype(o_ref.dtype)


def _mm_plain_call(a, w3, layer, n_cols, name):
    m, k = a.shape
    tm, tn, bufs, need = _ws_plan(m, k, n_cols, 1, 2 * 4 + 4, 4)
    return pl.pallas_call(
        _mm_plain_kernel,
        grid=(n_cols // tn, m // tm),
        in_specs=[pl.BlockSpec((tm, k), lambda j, i: (i, 0)),
                  _wspec((None, k, tn), lambda j, i: (layer, 0, j), bufs)],
        out_specs=pl.BlockSpec((tm, tn), lambda j, i: (i, j)),
        out_shape=jax.ShapeDtypeStruct((m, n_cols), f32),
        scratch_shapes=[pltpu.VMEM((k, tn), bf16)],
        compiler_params=_cparams(("arbitrary", "arbitrary"), need),
        name=name,
    )(a, w3)


def _mm_up_kernel(a_ref, wg_ref, wu_ref, o_ref, wgb_ref, wub_ref):
    @pl.when(pl.program_id(1) == 0)
    def _():
        wgb_ref[...] = wg_ref[...].astype(bf16)
        wub_ref[...] = wu_ref[...].astype(bf16)

    a = a_ref[...]
    g = jnp.dot(a, wgb_ref[...], preferred_element_type=f32)
    u = jnp.dot(a, wub_ref[...], preferred_element_type=f32)
    o_ref[...] = (_silu(g) * u).astype(o_ref.dtype)


def _mm_up_call(a, wg3, wu3, layer):
    m, k = a.shape
    f = wg3.shape[2]
    tm, tn, bufs, need = _ws_plan(m, k, LANES * 8, 2, 2 * 2 + 3 * 4, 2, tns=(512, 256, 128))
    tn = min(tn, f)
    n_main = f // tn
    rem = f - n_main * tn
    assert rem % LANES == 0 and (rem == 0 or (n_main * tn) % rem == 0)

    def call(width, col_blk0, ncols, name):
        wspec = _wspec((None, k, width), lambda j, i: (layer, 0, col_blk0 + j), bufs)
        return pl.pallas_call(
            _mm_up_kernel,
            grid=(ncols, m // tm),
            in_specs=[pl.BlockSpec((tm, k), lambda j, i: (i, 0)), wspec, wspec],
            out_specs=pl.BlockSpec((tm, width), lambda j, i: (i, j)),
            out_shape=jax.ShapeDtypeStruct((m, ncols * width), bf16),
            scratch_shapes=[pltpu.VMEM((k, width), bf16), pltpu.VMEM((k, width), bf16)],
            compiler_params=_cparams(("arbitrary", "arbitrary"), need),
            name=name,
        )(a, wg3, wu3)

    outs = [call(tn, 0, n_main, "ffn_up")]
    if rem:
        outs.append(call(rem, (n_main * tn) // rem, 1, "ffn_up_tail"))
    return outs


def _mm_res_kernel(*refs, nseg, scale):
    a_refs, w_refs = refs[:nseg], refs[nseg:2 * nseg]
    r_ref, gate_ref, o_ref = refs[2 * nseg:2 * nseg + 3]
    wb_refs = refs[2 * nseg + 3:]

    @pl.when(pl.program_id(1) == 0)
    def _():
        for w_ref, wb_ref in zip(w_refs, wb_refs):
            wb_ref[...] = w_ref[...].astype(bf16)

    acc = jnp.dot(a_refs[0][...], wb_refs[0][...], preferred_element_type=f32)
    for a_ref, wb_ref in zip(a_refs[1:], wb_refs[1:]):
        acc = acc + jnp.dot(a_ref[...], wb_ref[...], preferred_element_type=f32)
    for s in range(acc.shape[0] // SUB):
        rows = slice(s * SUB, (s + 1) * SUB)
        o_ref[rows, :] = r_ref[rows, :] + (scale * gate_ref[s:s + 1, :]) * acc[rows, :]


def _mm_res_call(acts, w3, layer, res, mod8, v_gate, scale, name):
    m, k0 = acts[0].shape
    n = w3.shape[2]
    nk = 1
    while (k0 // nk) * 1024 * 6 > (36 << 20) and (k0 // nk) % (2 * LANES) == 0:
        nk *= 2
    kc = k0 // nk
    k_rest = sum(a.shape[1] for a in acts[1:])
    tm, tn, bufs, need = _ws_plan(m, kc + k_rest, n, 1, 4 * 4 + 2 * 4 + 4, 8 * nk)
    nsub = tm // SUB
    out = res
    for kb in range(nk):
        segs = [(acts[0], kc, kb, kb)]
        row0 = k0
        if kb == nk - 1:
            for a in acts[1:]:
                assert row0 % a.shape[1] == 0
                segs.append((a, a.shape[1], 0, row0 // a.shape[1]))
                row0 += a.shape[1]
        a_specs = [pl.BlockSpec((tm, w), lambda j, i, cb=cb: (i, cb)) for _, w, cb, _ in segs]
        w_specs = [_wspec((None, w, tn), lambda j, i, rb=rb: (layer, rb, j), bufs) for _, w, _, rb in segs]
        out = pl.pallas_call(
            functools.partial(_mm_res_kernel, nseg=len(segs), scale=scale),
            grid=(n // tn, m // tm),
            in_specs=a_specs + w_specs + [pl.BlockSpec((tm, tn), lambda j, i: (i, j)),
                                          pl.BlockSpec((nsub, tn), lambda j, i: (i, v_gate * (n // tn) + j))],
            out_specs=pl.BlockSpec((tm, tn), lambda j, i: (i, j)),
            out_shape=jax.ShapeDtypeStruct((m, n), f32),
            scratch_shapes=[pltpu.VMEM((w, tn), bf16) for _, w, _, _ in segs],
            compiler_params=_cparams(("arbitrary", "arbitrary"), need),
            name=f"{name}_k{kb}",
        )(*[s[0] for s in segs], *([w3] * len(segs)), out, mod8)
    return out


def _rope_kernel(x_ref, cos_ref, sin_ref, q_ref, k_ref):
    x = x_ref[...]
    w = x.shape[1]
    nh = w // HEAD_DIM
    cosf = jnp.concatenate([cos_ref[...]] * nh, axis=1)
    sinf = jnp.concatenate([sin_ref[...]] * nh, axis=1)
    lane = _iota(x.shape, 1) % HEAD_DIM
    half = ROPE_DIM // 2
    partner = jnp.where(lane < half, pltpu.roll(x, w - half, 1), pltpu.roll(x, half, 1))
    y = x * cosf + partner * sinf
    qw = q_ref.shape[1]
    q_ref[...] = y[:, :qw]
    k_ref[...] = y[:, qw:]


def _rope_call(proj, cos_t, sin_t, qw, kw):
    m = proj.shape[0]
    w = qw + kw
    return pl.pallas_call(
        _rope_kernel,
        grid=(m // TM,),
        in_specs=[pl.BlockSpec((TM, w), lambda i: (i, 0)),
                  pl.BlockSpec((TM, HEAD_DIM), lambda i: (i, 0)),
                  pl.BlockSpec((TM, HEAD_DIM), lambda i: (i, 0))],
        out_specs=[pl.BlockSpec((TM, qw), lambda i: (i, 0)), pl.BlockSpec((TM, kw), lambda i: (i, 0))],
        out_shape=[jax.ShapeDtypeStruct((m, qw), f32), jax.ShapeDtypeStruct((m, kw), f32)],
        compiler_params=_cparams(("arbitrary",), 10 * TM * w * 4),
        name="rope",
    )(proj, cos_t, sin_t)


def _top_blocks(gate, n_valid):
    lane = _iota(gate.shape, 1)
    gate = jnp.where(lane < n_valid, gate, -jnp.inf)
    picks = []
    for kk in range(MOBA_TOPK):
        mx = jnp.max(gate, axis=-1, keepdims=True)
        idx = jnp.min(jnp.where(gate == mx, lane, LANES), axis=-1, keepdims=True)
        picks.append(jnp.where(kk < n_valid, idx, -1))
        gate = jnp.where(lane == idx, -jnp.inf, gate)
    return picks


def _top_blocks_t(gate, n_valid):
    blk = _iota(gate.shape, 0)
    gate = jnp.where(blk < n_valid, gate, -jnp.inf)
    picks = []
    for kk in range(MOBA_TOPK):
        mx = jnp.max(gate, axis=0, keepdims=True)
        idx = jnp.min(jnp.where(gate == mx, blk, gate.shape[0]), axis=0, keepdims=True)
        picks.append(jnp.where(kk < n_valid, idx, -1))
        gate = jnp.where(blk == idx, -jnp.inf, gate)
    return picks


def _moba_prompt_kernel(q_ref, k_ref, v_ref, g_ref, o_ref):
    i = pl.program_id(2)
    s_len = k_ref.shape[0]
    nb = s_len // MOBA_BLOCK
    scale = HEAD_DIM ** -0.5
    q2 = jnp.concatenate([q_ref[:, :HEAD_DIM], q_ref[:, HEAD_DIM:]], axis=0)
    rows = q2.shape[0]
    nbr = -(-nb // SUBLANES) * SUBLANES
    blk_row = _iota((nbr, HEAD_DIM), 0)
    kmean = jnp.zeros((nbr, HEAD_DIM), f32)
    for n in range(nb):
        mean_n = jnp.mean(k_ref[n * MOBA_BLOCK:(n + 1) * MOBA_BLOCK, :], axis=0, keepdims=True)
        kmean = jnp.where(blk_row == n, mean_n, kmean)
    gate_t = _dot3_nt(kmean, q2)
    picks_t = _top_blocks_t(gate_t, i)
    prow = _iota((SUBLANES, rows), 0)
    pk = jnp.zeros((SUBLANES, rows), f32)
    for kk in range(MOBA_TOPK):
        pk = jnp.where(prow == kk, picks_t[kk].astype(f32), pk)
    picks = [lax.dot_general(pk, (_iota((SUBLANES, LANES), 0) == kk).astype(f32), (((0,), (0,)), ((), ())),
                             preferred_element_type=f32) for kk in range(MOBA_TOPK)]
    qs = (q2 * scale).astype(bf16)

    own = pl.multiple_of(i * MOBA_BLOCK, MOBA_BLOCK)
    s = _bdot_nt(qs, k_ref[pl.ds(own, MOBA_BLOCK), :])
    rq = _iota(s.shape, 0) % MOBA_BLOCK
    ck = _iota(s.shape, 1)
    s = jnp.where(ck <= rq, s, -jnp.inf)
    m0 = jnp.max(s, axis=-1, keepdims=True)
    p = jnp.exp(s - m0)
    l0 = jnp.sum(p, axis=-1, keepdims=True)
    acc0 = _bdot(p, v_ref[pl.ds(own, MOBA_BLOCK), :])

    def body(n, carry):
        m, l, acc = carry
        start = pl.multiple_of(n * MOBA_BLOCK, MOBA_BLOCK)
        sn = _bdot_nt(qs, k_ref[pl.ds(start, MOBA_BLOCK), :])
        nf = n.astype(f32)
        sel = (picks[0] == nf) | (picks[1] == nf) | (picks[2] == nf)
        sn = jnp.concatenate([jnp.where(sel, sn[:, t * LANES:(t + 1) * LANES], -jnp.inf)
                              for t in range(MOBA_BLOCK // LANES)], axis=1)
        m_new = jnp.maximum(m, jnp.max(sn, axis=-1, keepdims=True))
        alpha = jnp.exp(m - m_new)
        pn = jnp.exp(sn - m_new)
        l_new = alpha * l + jnp.sum(pn, axis=-1, keepdims=True)
        acc_new = alpha * acc + _bdot(pn, v_ref[pl.ds(start, MOBA_BLOCK), :])
        return m_new, l_new, acc_new

    m, l, acc = lax.fori_loop(0, i, body, (m0, l0, acc0))
    o = _rms_heads(acc / l, g_ref[...])
    half = rows // 2
    o_ref[:, :HEAD_DIM] = o[:half].astype(o_ref.dtype)
    o_ref[:, HEAD_DIM:] = o[half:].astype(o_ref.dtype)


def _moba_prompt_call(q_rot, k_rot, proj, gnorm, b, s_len, kvh, v_col0):
    qt = s_len // MOBA_BLOCK
    vb = v_col0 // HEAD_DIM
    need = 4 * s_len * HEAD_DIM * 4 + 4 * MOBA_BLOCK * 2 * HEAD_DIM * 4 + 16 * 2 * MOBA_BLOCK * MOBA_BLOCK * 4
    return pl.pallas_call(
        _moba_prompt_kernel,
        grid=(b, kvh, qt),
        in_specs=[pl.BlockSpec((MOBA_BLOCK, 2 * HEAD_DIM), lambda bi, h, i: (bi * qt + i, h)),
                  pl.BlockSpec((s_len, HEAD_DIM), lambda bi, h, i: (bi, h)),
                  pl.BlockSpec((s_len, HEAD_DIM), lambda bi, h, i: (bi, vb + h)),
                  pl.BlockSpec((1, HEAD_DIM), lambda bi, h, i: (0, 0))],
        out_specs=pl.BlockSpec((MOBA_BLOCK, 2 * HEAD_DIM), lambda bi, h, i: (bi * qt + i, h)),
        out_shape=jax.ShapeDtypeStruct((b * s_len, 2 * kvh * HEAD_DIM), bf16),
        compiler_params=_cparams(("arbitrary", "arbitrary", "arbitrary"), need),
        name="moba_prompt",
    )(q_rot, k_rot, proj, gnorm.reshape(1, HEAD_DIM))


QROWS = 8


def _moba_sample_kernel(pt_ref, q_ref, kn_ref, vn_ref, *refs, nb, bps, kvh, t_len):
    del pt_ref
    npg = bps * (MOBA_BLOCK // PAGE_SIZE)
    k_pages, v_pages = refs[:npg], refs[npg:2 * npg]
    g_ref, o_ref, qs_sc, s_sc, gate_sc, idx_sc, m_sc, l_sc, acc_sc = refs[2 * npg:]
    j = pl.program_id(1)
    nsk = nb // bps
    scale = HEAD_DIM ** -0.5
    r2 = 2 * QROWS

    def block_of(pages, bb, h):
        ppb = MOBA_BLOCK // PAGE_SIZE
        return jnp.concatenate([pages[bb * ppb + t][pl.ds(h, PAGE_SIZE, stride=kvh), :] for t in range(ppb)], axis=0)

    @pl.when(j == 0)
    def _():
        for h in range(kvh):
            q2 = jnp.concatenate([q_ref[0:QROWS, (2 * h) * HEAD_DIM:(2 * h + 1) * HEAD_DIM],
                                  q_ref[0:QROWS, (2 * h + 1) * HEAD_DIM:(2 * h + 2) * HEAD_DIM]], axis=0)
            qs_sc[h] = q2
        gate_sc[...] = jnp.zeros_like(gate_sc)

    @pl.when(j < nsk)
    def _():
        lane = _iota((r2, LANES), 1)
        pairs = [(bb, h) for bb in range(bps) for h in range(kvh)]
        q2s = [qs_sc[h] for h in range(kvh)]
        khs = [block_of(k_pages, bb, h) for bb, h in pairs]
        scores = [_bdot_nt(q2s[h] * scale, kh) for (bb, h), kh in zip(pairs, khs)]
        kmeans = [jnp.sum(kh, axis=0, keepdims=True) * (1.0 / MOBA_BLOCK) for kh in khs]
        cols = [jnp.sum(q2s[h] * km, axis=-1, keepdims=True) for (bb, h), km in zip(pairs, kmeans)]
        for (bb, h), sc in zip(pairs, scores):
            s_sc[j * bps + bb, h] = sc
        for h in range(kvh):
            gate = gate_sc[h]
            for (bb, hh), col in zip(pairs, cols):
                if hh == h:
                    gate = jnp.where(lane == j * bps + bb, col, gate)
            gate_sc[h] = gate

    @pl.when(j == nsk - 1)
    def _():
        for h in range(kvh):
            picks = _top_blocks(gate_sc[h], nb)
            for kk in range(MOBA_TOPK):
                idx_sc[h * MOBA_TOPK + kk] = jnp.broadcast_to(picks[kk], (r2, LANES))
            kn = jnp.concatenate([kn_ref[0:QROWS, h * HEAD_DIM:(h + 1) * HEAD_DIM],
                                  jnp.zeros((LANES - QROWS, HEAD_DIM), f32)], axis=0)
            vn = jnp.concatenate([vn_ref[0:QROWS, h * HEAD_DIM:(h + 1) * HEAD_DIM],
                                  jnp.zeros((LANES - QROWS, HEAD_DIM), f32)], axis=0)
            s = _bdot_nt(qs_sc[h] * scale, kn)
            tq = _iota(s.shape, 0) % QROWS
            ck = _iota(s.shape, 1)
            s = jnp.where((ck <= tq) & (ck < t_len), s, -jnp.inf)
            m0 = jnp.max(s, axis=-1, keepdims=True)
            p = jnp.exp(s - m0)
            m_sc[h] = jnp.broadcast_to(m0, (r2, LANES))
            l_sc[h] = jnp.broadcast_to(jnp.sum(p, axis=-1, keepdims=True), (r2, LANES))
            acc_sc[h] = _bdot(p, vn)

    @pl.when(j >= nsk)
    def _():
        heads_ = range(kvh)
        picks = [[idx_sc[h * MOBA_TOPK + kk][:, 0:1] for kk in range(MOBA_TOPK)] for h in heads_]
        sns = []
        for h in heads_:
            row = []
            for bb in range(bps):
                n = (j - nsk) * bps + bb
                sel = (picks[h][0] == n) | (picks[h][1] == n) | (picks[h][2] == n)
                row.append(jnp.where(sel, s_sc[n, h], -jnp.inf))
            sns.append(row)
        ms = [m_sc[h][:, 0:1] for h in heads_]
        m_news = []
        for h in heads_:
            mx = ms[h]
            for sn in sns[h]:
                mx = jnp.maximum(mx, jnp.max(sn, axis=-1, keepdims=True))
            m_news.append(mx)
        alphas = [jnp.exp(ms[h] - m_news[h]) for h in heads_]
        pns = [[jnp.exp(sn - m_news[h]) for sn in sns[h]] for h in heads_]
        pvs = [[_bdot(pns[h][bb], block_of(v_pages, bb, h)) for bb in range(bps)] for h in heads_]
        for h in heads_:
            l = alphas[h] * l_sc[h][:, 0:1]
            acc = alphas[h] * acc_sc[h]
            for bb in range(bps):
                l = l + jnp.sum(pns[h][bb], axis=-1, keepdims=True)
                acc = acc + pvs[h][bb]
            m_sc[h] = jnp.broadcast_to(m_news[h], (r2, LANES))
            l_sc[h] = jnp.broadcast_to(l, (r2, LANES))
            acc_sc[h] = acc

    @pl.when(j == 2 * nsk - 1)
    def _():
        for h in range(kvh):
            o = _rms_heads(acc_sc[h] / l_sc[h][:, 0:1], g_ref[...])
            o_ref[:, (2 * h) * HEAD_DIM:(2 * h + 1) * HEAD_DIM] = o[:QROWS]
            o_ref[:, (2 * h + 1) * HEAD_DIM:(2 * h + 2) * HEAD_DIM] = o[QROWS:]


def _moba_sample_call(page_table, q_rot, k_rot, proj, cache_k, cache_v, gnorm, layer,
                      db, row0, kvh, v_col0, t_len):
    n_pages = page_table.shape[1]
    nb = n_pages * PAGE_SIZE // MOBA_BLOCK
    ppb = MOBA_BLOCK // PAGE_SIZE
    bps = _pick(nb, (4, 2, 1))
    nsk = nb // bps
    npg = bps * ppb
    qw = 2 * kvh * HEAD_DIM
    kw = kvh * HEAD_DIM
    blk0 = row0 // SUB

    def kmap(t):
        return lambda b, j, pt: (layer, pt[b, npg * jnp.minimum(j, nsk - 1) + t], 0, 0)

    def vmap_(t):
        return lambda b, j, pt: (layer, pt[b, npg * jnp.maximum(j - nsk, 0) + t], 0, 0)

    depth, n_pool = cache_k.shape[:2]
    cache_k = cache_k.reshape(depth, n_pool, PAGE_SIZE * kvh, HEAD_DIM)
    cache_v = cache_v.reshape(depth, n_pool, PAGE_SIZE * kvh, HEAD_DIM)
    page_spec = lambda fn: pl.BlockSpec((None, None, PAGE_SIZE * kvh, HEAD_DIM), fn)
    r2 = 2 * QROWS
    need = 4 * npg * PAGE_SIZE * kvh * HEAD_DIM * 4 + nb * kvh * r2 * MOBA_BLOCK * 4 + (8 << 20)
    grid_spec = pltpu.PrefetchScalarGridSpec(
        num_scalar_prefetch=1,
        grid=(db, 2 * nsk),
        in_specs=[pl.BlockSpec((SUB, qw), lambda b, j, pt: (blk0 + b, 0)),
                  pl.BlockSpec((SUB, kw), lambda b, j, pt: (blk0 + b, 0)),
                  pl.BlockSpec((SUB, kw), lambda b, j, pt: (blk0 + b, v_col0 // kw))]
                 + [page_spec(kmap(t)) for t in range(npg)] + [page_spec(vmap_(t)) for t in range(npg)]
                 + [pl.BlockSpec((1, HEAD_DIM), lambda b, j, pt: (0, 0))],
        out_specs=pl.BlockSpec((QROWS, qw), lambda b, j, pt: (b, 0)),
        scratch_shapes=[pltpu.VMEM((kvh, r2, HEAD_DIM), f32),
                        pltpu.VMEM((nb, kvh, r2, MOBA_BLOCK), f32),
                        pltpu.VMEM((kvh, r2, LANES), f32),
                        pltpu.VMEM((kvh * MOBA_TOPK, r2, LANES), jnp.int32),
                        pltpu.VMEM((kvh, r2, LANES), f32),
                        pltpu.VMEM((kvh, r2, LANES), f32),
                        pltpu.VMEM((kvh, r2, HEAD_DIM), f32)])
    return pl.pallas_call(
        functools.partial(_moba_sample_kernel, nb=nb, bps=bps, kvh=kvh, t_len=t_len),
        grid_spec=grid_spec,
        out_shape=jax.ShapeDtypeStruct((db * QROWS, qw), f32),
        compiler_params=_cparams(("arbitrary", "arbitrary"), need),
        name="moba_sample",
    )(page_table, q_rot, k_rot, proj, *([cache_k] * npg), *([cache_v] * npg), gnorm.reshape(1, HEAD_DIM))


def _inv_unit_lower(lmats, nblk):
    c = lmats[0].shape[0]
    ii = _iota((c, c), 0)
    jj = _iota((c, c), 1)
    eye = (ii == jj).astype(f32)
    same = (ii // INV_BLK) == (jj // INV_BLK)
    dmats = [jnp.where(same, l, 0.0) for l in lmats]
    ps = [-d for d in dmats]
    xs = [eye + p for p in ps]
    k = 2
    while k < INV_BLK:
        ps = [_bdot(p, p) for p in ps]
        xs = [x + _bdot(x, p) for x, p in zip(xs, ps)]
        k *= 2
    if nblk > 1:
        mms = [_bdot(x, l - d) for x, l, d in zip(xs, lmats, dmats)]
        ys = [eye - mm for mm in mms]
        pms = mms
        k = 2
        while k < nblk:
            pms = [_bdot(pm, pm) for pm in pms]
            ys = [y + _bdot(y, pm) for y, pm in zip(ys, pms)]
            k *= 2
        xs = [_bdot(y, x) for y, x in zip(ys, xs)]
    for _ in range(2):
        rs = [eye - _dot3(eye + l, x) for l, x in zip(lmats, xs)]
        xs = [x + _bdot(x, r) for x, r in zip(xs, rs)]
    return xs


def _gdn_kernel(main_ref, ab_ref, cw_ref, prev_ref, alog_ref, dtb_ref, s0_ref, g_ref, o_ref, s_ref, carry_sc,
                *, heads, group, col0, t_valid):
    c_idx = pl.program_id(1)
    c = main_ref.shape[0]
    gw = heads * HEAD_DIM
    ch = 3 * gw

    @pl.when(c_idx == 0)
    def _():
        s_ref[...] = s0_ref[...]
        carry_sc[...] = prev_ref[...]

    x = main_ref[:, col0:col0 + ch]
    xp = jnp.concatenate([carry_sc[...], x], axis=0)
    y = cw_ref[GDN_CONV - 1:GDN_CONV, :] * x
    for tap in range(1, GDN_CONV):
        y = y + cw_ref[GDN_CONV - 1 - tap:GDN_CONV - tap, :] * pltpu.roll(xp, tap, 0)[SUBLANES:, :]
    carry_sc[...] = x[c - SUBLANES:, :]
    y = _silu(y)

    ab = ab_ref[...]
    lane = _iota(ab.shape, 1)
    row_ok = (c_idx * c + _iota((c, 1), 0)) < t_valid
    g_all = jnp.where(row_ok & (lane < heads), -jnp.exp(alog_ref[...]) * _softplus(ab + dtb_ref[...]), 0.0)
    beta_all = jnp.where(row_ok, jax.nn.sigmoid(ab), 0.0)
    tri = (_iota((c, c), 0) >= _iota((c, c), 1)).astype(bf16)
    gam_all = _sel_dot(tri, g_all)
    eye_l = (_iota((LANES, LANES), 0) == _iota((LANES, LANES), 1)).astype(bf16)
    gam_t = _sel_dot_nt(eye_l, gam_all)
    gnorm = g_ref[...]

    r = group * c
    ii = _iota((r, r), 0)
    jj = _iota((r, r), 1)
    same = (ii // c) == (jj // c)
    low = same & (ii >= jj)
    strict = same & (ii > jj)
    groups = [range(g * group, (g + 1) * group) for g in range(heads // group)]
    stack = lambda hs, f: jnp.concatenate([f(h) for h in hs], axis=0)
    qs, ks, vs, betas, gcols, decays, egams, kds = [], [], [], [], [], [], [], []
    for hs in groups:
        xq = stack(hs, lambda h: y[:, h * HEAD_DIM:(h + 1) * HEAD_DIM])
        xk = stack(hs, lambda h: y[:, gw + h * HEAD_DIM:gw + (h + 1) * HEAD_DIM])
        vs.append(stack(hs, lambda h: y[:, 2 * gw + h * HEAD_DIM:2 * gw + (h + 1) * HEAD_DIM]))
        qs.append(xq * lax.rsqrt(jnp.sum(xq * xq, axis=-1, keepdims=True) + EPS) * (HEAD_DIM ** -0.5))
        ks.append(xk * lax.rsqrt(jnp.sum(xk * xk, axis=-1, keepdims=True) + EPS))
        betas.append(stack(hs, lambda h: beta_all[:, heads + h:heads + h + 1]))
        gcol = stack(hs, lambda h: gam_all[:, h:h + 1])
        glast = stack(hs, lambda h: jnp.broadcast_to(gam_all[c - 1:c, h:h + 1], (c, 1)))
        grow = jnp.concatenate([gam_t[h:h + 1, :] for h in hs], axis=1)
        gcols.append(gcol)
        egams.append(jnp.exp(gcol))
        decays.append(jnp.exp(jnp.where(low, gcol - grow, -jnp.inf)))
        kds.append(ks[-1] * jnp.exp(glast - gcol))
    kks = [_bdot_nt(k, k) for k in ks]
    lmats = [jnp.where(strict, beta * kk * decay, 0.0) for beta, kk, decay in zip(betas, kks, decays)]
    tmats = _inv_unit_lower(lmats, c // INV_BLK)
    us = [_bdot(t, beta * v) for t, beta, v in zip(tmats, betas, vs)]
    ws = [_bdot(t, (beta * egam) * k) for t, beta, egam, k in zip(tmats, betas, egams, ks)]
    qks = [_bdot_nt(q, k) * decay for q, k, decay in zip(qs, ks, decays)]
    qes = [q * egam for q, egam in zip(qs, egams)]
    rows_of = lambda a: slice(a * c, (a + 1) * c)
    states = [[s_ref[h] for h in hs] for hs in groups]
    v_news = [jnp.concatenate([u[rows_of(a), :] - _bdot(w[rows_of(a), :], st[a]) for a in range(group)], axis=0)
              for u, w, st in zip(us, ws, states)]
    o_inters = [jnp.concatenate([_bdot(qe[rows_of(a), :], st[a]) for a in range(group)], axis=0)
                for qe, st in zip(qes, states)]
    os_ = [oi + _bdot(qk, vn) for oi, qk, vn in zip(o_inters, qks, v_news)]
    for hs, st, kd, vn, o in zip(groups, states, kds, v_news, os_):
        for a, h in enumerate(hs):
            s_ref[h] = jnp.exp(gam_all[c - 1:c, h:h + 1]) * st[a] + _bdot_tn(kd[rows_of(a), :], vn[rows_of(a), :])
            z = main_ref[:, col0 + ch + h * HEAD_DIM:col0 + ch + (h + 1) * HEAD_DIM]
            o_ref[:, h * HEAD_DIM:(h + 1) * HEAD_DIM] = (_rms_heads(o[rows_of(a), :], gnorm) * _silu(z)).astype(o_ref.dtype)


def _gdn_call(proj, tail, cw, prev8, alog, dtb, s0, gnorm, layer, *, nseq, rows_per_seq, row0, chunk, heads,
              col0, ab_col0, t_valid, name):
    mainw = proj.shape[1]
    gw = heads * HEAD_DIM
    ch = 3 * gw
    nch = rows_per_seq // chunk
    blk0 = row0 // chunk
    need = 2 * chunk * mainw * 4 + 4 * heads * HEAD_DIM * HEAD_DIM * 4 + 40 * chunk * ch * 4 + (4 << 20)
    return pl.pallas_call(
        functools.partial(_gdn_kernel, heads=heads, group=_pick(heads, (2, 1)), col0=col0, t_valid=t_valid),
        grid=(nseq, nch),
        in_specs=[pl.BlockSpec((chunk, mainw), lambda b, c: (blk0 + b * nch + c, 0)),
                  pl.BlockSpec((chunk, LANES), lambda b, c: (blk0 + b * nch + c, ab_col0 // LANES)),
                  pl.BlockSpec((None, GDN_CONV, ch), lambda b, c: (layer, 0, 0)),
                  pl.BlockSpec((None, SUBLANES, ch), lambda b, c: (b, 0, 0)),
                  pl.BlockSpec((1, LANES), lambda b, c: (0, 0)),
                  pl.BlockSpec((1, LANES), lambda b, c: (0, 0)),
                  pl.BlockSpec((None, heads, HEAD_DIM, HEAD_DIM), lambda b, c: (b, 0, 0, 0)),
                  pl.BlockSpec((1, HEAD_DIM), lambda b, c: (0, 0))],
        out_specs=[pl.BlockSpec((chunk, gw), lambda b, c: (b * nch + c, 0)),
                   pl.BlockSpec((None, heads, HEAD_DIM, HEAD_DIM), lambda b, c: (b, 0, 0, 0))],
        out_shape=[jax.ShapeDtypeStruct((nseq * rows_per_seq, gw), bf16),
                   jax.ShapeDtypeStruct((nseq, heads, HEAD_DIM, HEAD_DIM), f32)],
        scratch_shapes=[pltpu.VMEM((SUBLANES, ch), f32)],
        compiler_params=_cparams(("arbitrary", "arbitrary"), need),
        name=name,
    )(proj, tail, cw, prev8, alog, dtb, s0, gnorm.reshape(1, HEAD_DIM))


def _gla_kernel(tail_ref, wg_ref, bg_ref, s0_ref, g_ref, o_ref, s_ref, *, heads, ab_col0, t_valid):
    c_idx = pl.program_id(1)
    c = tail_ref.shape[0]
    kw = heads * GLA_DK
    vw = heads * HEAD_DIM

    @pl.when(c_idx == 0)
    def _():
        s_ref[...] = s0_ref[...]

    row_ok = (c_idx * c + _iota((c, 1), 0)) < t_valid
    fblk = tail_ref[:, ab_col0:ab_col0 + LANES]
    pre = _dot3(fblk, wg_ref[...]) + bg_ref[...]
    log_a = jnp.where(row_ok, -_softplus(-pre) * (1.0 / GLA_TAU), 0.0)
    tri = (_iota((c, c), 0) >= _iota((c, c), 1)).astype(bf16)
    bc_all = _sel_dot(tri, log_a)
    gnorm = g_ref[...]
    lane = _iota((c, LANES), 1)
    first = lane < GLA_DK
    ii = _iota((c, LANES), 0)
    jj = lane % GLA_DK
    rowid = _iota((c, 1), 0)
    eye_l = _iota((LANES, LANES), 0) == _iota((LANES, LANES), 1)
    seg = (((_iota((2 * LANES, LANES), 0) % LANES) < GLA_DK) == (_iota((2 * LANES, LANES), 1) < GLA_DK)).astype(bf16)
    zpad = jnp.zeros((GLA_DK - c, LANES), f32) if c < GLA_DK else None

    def stack_pair(a):
        a0 = jnp.where(first, a, 0.0)
        a1 = jnp.where(first, 0.0, a)
        parts = [a0, a1] if zpad is None else [a0, zpad, a1, zpad]
        return jnp.concatenate(parts, axis=0)

    for p in range(heads // 2):
        ls = slice(p * LANES, (p + 1) * LANES)
        q = tail_ref[:, ls] * (GLA_DK ** -0.5)
        k = jnp.where(row_ok, tail_ref[:, kw + p * LANES:kw + (p + 1) * LANES], 0.0)
        v0 = tail_ref[:, 2 * kw + (2 * p) * HEAD_DIM:2 * kw + (2 * p + 1) * HEAD_DIM]
        v1 = tail_ref[:, 2 * kw + (2 * p + 1) * HEAD_DIM:2 * kw + (2 * p + 2) * HEAD_DIM]
        r0_ = tail_ref[:, 2 * kw + vw + (2 * p) * HEAD_DIM:2 * kw + vw + (2 * p + 1) * HEAD_DIM]
        r1_ = tail_ref[:, 2 * kw + vw + (2 * p + 1) * HEAD_DIM:2 * kw + vw + (2 * p + 2) * HEAD_DIM]
        vparts = [v0, v1] if zpad is None else [v0, zpad, v1, zpad]
        v2 = jnp.concatenate(vparts, axis=0).astype(bf16)
        bc = bc_all[:, ls]
        s = s_ref[p]
        qe = q * jnp.exp(bc)
        o0 = _bdot(jnp.where(first, qe, 0.0), s)
        o1 = _bdot(jnp.where(first, 0.0, qe), s)

        pieces = [jnp.zeros((GLA_SUB, LANES), f32)]
        for sb in range(1, c // GLA_SUB):
            r0 = sb * GLA_SUB
            rs = slice(r0, r0 + GLA_SUB)
            b0 = bc[r0:r0 + 1, :]
            q_i = q[rs, :] * jnp.exp(bc[rs, :] - b0)
            k_j = k * jnp.exp(jnp.where(rowid < r0, b0 - bc, -jnp.inf))
            pieces.append(_bdot_nt(q_i, stack_pair(k_j)))
        att_off = jnp.concatenate(pieces, axis=0)

        att = jnp.zeros((c, LANES), f32)
        for dlt in range(GLA_SUB):
            k_r = pltpu.roll(k, dlt, 0) if dlt else k
            bc_r = pltpu.roll(bc, dlt, 0) if dlt else bc
            ok = (rowid % GLA_SUB) >= dlt
            pr = q * k_r * jnp.exp(jnp.where(ok, bc - bc_r, -jnp.inf))
            hi, lo = _split2(pr)
            sums = jnp.dot(jnp.concatenate([hi, lo], axis=1), seg, preferred_element_type=f32)
            att = jnp.where(jj == ii - dlt, sums, att)
        att = att + att_off

        o0 = o0 + _bdot(jnp.where(first, att, 0.0), v2)
        o1 = o1 + _bdot(jnp.where(first, 0.0, att), v2)

        bl = bc[c - 1:c, :]
        ebl_col = jnp.sum(jnp.where(eye_l, jnp.exp(bl), 0.0), axis=-1, keepdims=True)
        s_ref[p] = ebl_col * s + _bdot_tn(stack_pair(k * jnp.exp(bl - bc)), v2)
        o_ref[:, (2 * p) * HEAD_DIM:(2 * p + 1) * HEAD_DIM] = (_rms_heads(o0, gnorm) * _silu(r0_)).astype(o_ref.dtype)
        o_ref[:, (2 * p + 1) * HEAD_DIM:(2 * p + 2) * HEAD_DIM] = (_rms_heads(o1, gnorm) * _silu(r1_)).astype(o_ref.dtype)


def _gla_call(tail, wg_pad, bg, s0, gnorm, *, nseq, rows_per_seq, row0, chunk, heads, ab_col0, t_valid, name):
    tailw = tail.shape[1]
    kw = heads * GLA_DK
    vw = heads * HEAD_DIM
    nch = rows_per_seq // chunk
    blk0 = row0 // chunk
    assert heads % 2 == 0 and chunk <= GLA_DK and chunk % GLA_SUB == 0
    s0 = s0.reshape(nseq, heads // 2, 2 * GLA_DK, HEAD_DIM)
    need = 2 * chunk * tailw * 4 + 4 * heads * GLA_DK * HEAD_DIM * 4 + 2 * LANES * kw * 4 + 40 * chunk * tailw * 4
    o, s_fin = pl.pallas_call(
        functools.partial(_gla_kernel, heads=heads, ab_col0=ab_col0, t_valid=t_valid),
        grid=(nseq, nch),
        in_specs=[pl.BlockSpec((chunk, tailw), lambda b, c: (blk0 + b * nch + c, 0)),
                  pl.BlockSpec((LANES, kw), lambda b, c: (0, 0)),
                  pl.BlockSpec((1, kw), lambda b, c: (0, 0)),
                  pl.BlockSpec((None, heads // 2, 2 * GLA_DK, HEAD_DIM), lambda b, c: (b, 0, 0, 0)),
                  pl.BlockSpec((1, HEAD_DIM), lambda b, c: (0, 0))],
        out_specs=[pl.BlockSpec((chunk, vw), lambda b, c: (b * nch + c, 0)),
                   pl.BlockSpec((None, heads // 2, 2 * GLA_DK, HEAD_DIM), lambda b, c: (b, 0, 0, 0))],
        out_shape=[jax.ShapeDtypeStruct((nseq * rows_per_seq, vw), bf16),
                   jax.ShapeDtypeStruct((nseq, heads // 2, 2 * GLA_DK, HEAD_DIM), f32)],
        compiler_params=_cparams(("arbitrary", "arbitrary"), need),
        name=name,
    )(tail, wg_pad, bg.reshape(1, kw), s0, gnorm.reshape(1, HEAD_DIM))
    return o, s_fin.reshape(nseq, heads, GLA_DK, HEAD_DIM)


def kernel(x_prompt, x_sample, c_prompt, c_sample, cache_k, cache_v, page_table, state_gdn, state_gdn_conv, state_gla, ada_w, ada_b, norm_ffn_a, ffn_a_wg, ffn_a_wu, ffn_a_wd, norm_mix, w_in, moba_norm, gdn_conv_w, gdn_a_log, gdn_dt_bias, gdn_norm, gla_w_gate, gla_b_gate, gla_norm, w_out, norm_ffn_b, ffn_b_wg, ffn_b_wu, ffn_b_wd, final_norm):
    b, s_len, d = x_prompt.shape
    db, t_dec, _ = x_sample.shape
    depth = ada_w.shape[0]
    n_pages = page_table.shape[1]
    past_len = n_pages * PAGE_SIZE

    n_heads = d // HEAD_DIM
    moba_h = n_heads // 4
    kvh = moba_h // 2
    gdn_h = (3 * n_heads) // 8
    gla_h = n_heads - moba_h - gdn_h
    qw, kvw = moba_h * HEAD_DIM, kvh * HEAD_DIM
    gdn_w, gla_kw, gla_w = gdn_h * HEAD_DIM, gla_h * GLA_DK, gla_h * HEAD_DIM
    main_w = qw + 2 * kvw + 4 * gdn_w
    small0 = main_w
    gla0 = main_w + 2 * gdn_h
    f0 = gla0 + 2 * gla_kw + 2 * gla_w
    tail_main = 2 * gla_kw + 2 * gla_w
    assert w_in.shape[2] == f0 + GLA_GATE_RANK
    assert s_len % TM == 0 and s_len % CHUNK == 0 and s_len >= MOBA_TOPK * MOBA_BLOCK
    assert (db * SUB) % TM == 0 and b + db <= 16
    assert GDN_CONV - 1 <= t_dec <= QROWS and past_len % MOBA_BLOCK == 0
    assert 2 * gdn_h + GLA_GATE_RANK <= LANES and tail_main % LANES == 0

    bs = b * s_len
    m_tot = bs + db * SUB
    n_tiles_p = bs // TM

    def pack_rows(p_rows, s_rows):
        s_pad = jnp.pad(s_rows, ((0, 0), (0, SUB - s_rows.shape[1]), (0, 0)))
        return jnp.concatenate([p_rows, s_pad.reshape(db * SUB, s_rows.shape[2])], axis=0)

    def sample_rows(a):
        return a[bs:].reshape(db, SUB, a.shape[1])[:, :t_dec]

    x = _pack_call(x_prompt.reshape(bs, d),
                   jnp.pad(x_sample, ((0, 0), (0, SUB - t_dec), (0, 0))).reshape(db * SUB, d))

    c16 = jnp.concatenate([c_prompt, c_sample, jnp.zeros((16 - b - db, d), f32)], axis=0)
    mod = _mod_call(c16, ada_w, ada_b)

    half = ROPE_DIM // 2
    inv_freq = ROPE_THETA ** (-jnp.arange(half, dtype=f32) / half)
    pos_s = past_len + jnp.minimum(jnp.arange(SUB), t_dec - 1)
    pos = jnp.concatenate([jnp.tile(jnp.arange(s_len), b), jnp.tile(pos_s, db)]).astype(f32)
    ang = pos[:, None] * inv_freq[None, :]
    ones = jnp.ones((m_tot, HEAD_DIM - ROPE_DIM), f32)
    cos_t = jnp.concatenate([jnp.cos(ang), jnp.cos(ang), ones], axis=1)
    sin_t = jnp.concatenate([-jnp.sin(ang), jnp.sin(ang), 0.0 * ones], axis=1)

    tail_w = -(-(tail_main + LANES) // 512) * 512
    zeros_prev = jnp.zeros((b, SUBLANES, 3 * gdn_w), f32)

    outs = {k: [] for k in ("kp", "vp", "ks", "vs", "gp", "gs", "cp", "cs", "lp", "ls")}
    for l in range(depth):
        mod8 = jnp.concatenate([jnp.repeat(mod[l, :b], (s_len // TM) * (TM // SUB), axis=0), mod[l, b:b + db]], axis=0)

        h = _premod_call(x, norm_ffn_a[l], mod8, 0, 1)
        a = _mm_up_call(h, ffn_a_wg, ffn_a_wu, l)
        x = _mm_res_call(a, ffn_a_wd, l, x, mod8, 2, 0.5, "ffn_a_down")

        h = _premod_call(x, norm_mix[l], mod8, 3, 4)
        proj = _mm_plain_call(h, w_in, l, main_w, "w_in_main")
        w_tail = jnp.concatenate([w_in[l][:, gla0:f0], w_in[l][:, small0:gla0], w_in[l][:, f0:],
                                  jnp.zeros((d, tail_w - tail_main - 2 * gdn_h - GLA_GATE_RANK), f32)], axis=1)
        tail = _mm_plain_call(h, w_tail[None], 0, tail_w, "w_in_tail")

        q_rot, k_rot = _rope_call(proj, cos_t, sin_t, qw, kvw)
        v_col0 = qw + kvw
        om_p = _moba_prompt_call(q_rot, k_rot, proj, moba_norm[l], b, s_len, kvh, v_col0)
        om_s = _moba_sample_call(page_table, q_rot, k_rot, proj, cache_k, cache_v, moba_norm[l], l,
                                 db, bs, kvh, v_col0, t_dec)
        om_s = om_s.reshape(db, QROWS, qw)[:, :t_dec].astype(bf16)

        col0 = qw + 2 * kvw
        alog = jnp.zeros((1, LANES), f32).at[0, :gdn_h].set(gdn_a_log[l])
        dtb = jnp.zeros((1, LANES), f32).at[0, :gdn_h].set(gdn_dt_bias[l])
        od_p, gp = _gdn_call(proj, tail, gdn_conv_w, zeros_prev, alog, dtb,
                             jnp.zeros((b, gdn_h, HEAD_DIM, HEAD_DIM), f32), gdn_norm[l], l,
                             nseq=b, rows_per_seq=s_len, row0=0, chunk=CHUNK, heads=gdn_h, col0=col0,
                             ab_col0=tail_main, t_valid=s_len, name="gdn_prompt")
        prev_s = jnp.pad(state_gdn_conv[l], ((0, 0), (SUBLANES - (GDN_CONV - 1), 0), (0, 0)))
        od_s, gs = _gdn_call(proj, tail, gdn_conv_w, prev_s, alog, dtb, state_gdn[l], gdn_norm[l], l,
                             nseq=db, rows_per_seq=SUB, row0=bs, chunk=SUB, heads=gdn_h, col0=col0,
                             ab_col0=tail_main, t_valid=t_dec, name="gdn_sample")

        wg_pad = jnp.zeros((LANES, gla_kw), f32).at[2 * gdn_h:2 * gdn_h + GLA_GATE_RANK].set(gla_w_gate[l])
        ol_p, lp = _gla_call(tail, wg_pad, gla_b_gate[l], jnp.zeros((b, gla_h, GLA_DK, HEAD_DIM), f32), gla_norm[l],
                             nseq=b, rows_per_seq=s_len, row0=0, chunk=CHUNK, heads=gla_h, ab_col0=tail_main,
                             t_valid=s_len, name="gla_prompt")
        ol_s, ls = _gla_call(tail, wg_pad, gla_b_gate[l], state_gla[l], gla_norm[l],
                             nseq=db, rows_per_seq=SUB, row0=bs, chunk=SUB, heads=gla_h, ab_col0=tail_main,
                             t_valid=t_dec, name="gla_sample")

        o_p = jnp.concatenate([om_p, od_p, ol_p], axis=1)
        o_s = jnp.concatenate([om_s, od_s.reshape(db, SUB, gdn_w)[:, :t_dec], ol_s.reshape(db, SUB, gla_w)[:, :t_dec]], axis=2)
        o_mix = pack_rows(o_p, o_s)
        x = _mm_res_call([o_mix], w_out, l, x, mod8, 5, 1.0, "w_out")

        h = _premod_call(x, norm_ffn_b[l], mod8, 6, 7)
        a = _mm_up_call(h, ffn_b_wg, ffn_b_wu, l)
        x = _mm_res_call(a, ffn_b_wd, l, x, mod8, 8, 0.5, "ffn_b_down")

        conv_cols = proj[:, col0:col0 + 3 * gdn_w]
        outs["kp"].append(k_rot[:bs].reshape(b, s_len, kvh, HEAD_DIM))
        outs["vp"].append(proj[:bs, v_col0:v_col0 + kvw].reshape(b, s_len, kvh, HEAD_DIM))
        outs["ks"].append(sample_rows(k_rot).reshape(db, t_dec, kvh, HEAD_DIM))
        outs["vs"].append(sample_rows(proj[:, v_col0:v_col0 + kvw]).reshape(db, t_dec, kvh, HEAD_DIM))
        outs["gp"].append(gp)
        outs["gs"].append(gs)
        outs["cp"].append(conv_cols[:bs].reshape(b, s_len, 3 * gdn_w)[:, s_len - (GDN_CONV - 1):])
        outs["cs"].append(sample_rows(conv_cols)[:, t_dec - (GDN_CONV - 1):])
        outs["lp"].append(lp)
        outs["ls"].append(ls)

    y = _rms_call(x, final_norm)
    st = {k: jnp.stack(v) for k, v in outs.items()}
    return (y[:bs].reshape(b, s_len, d), sample_rows(y),
            st["kp"], st["vp"], st["ks"], st["vs"], st["gp"], st["gs"], st["cp"], st["cs"], st["lp"], st["ls"])
```

```python
import functools
import math

import jax
import jax.numpy as jnp
from jax import lax
from jax.experimental import pallas as pl
from jax.experimental.pallas import tpu as pltpu

f32 = jnp.float32
bf16 = jnp.bfloat16

HEAD_DIM = 128
MOBA_BLOCK = 256
MOBA_TOPK = 3
ROPE_THETA = 500000.0
ROPE_DIM = HEAD_DIM // 4
GDN_CONV = 4
GLA_DK = HEAD_DIM // 2
GLA_GATE_RANK = 16
GLA_TAU = 16.0
GLA_SUB = 16
INV_BLK = 16
CHUNK = 64
PAGE_SIZE = 128
N_MOD = 9
EPS = 1e-6

LANES = 128
SUBLANES = 8
TM = 256
SUB = 32
V7X_VMEM_BYTES = 64 * 1024 * 1024
VMEM_BUDGET = 58 * 1024 * 1024


def _cparams(sem, need_bytes):
    limit = int(min(max(need_bytes * 1.25 + (4 << 20), 16 << 20), VMEM_BUDGET))
    return pltpu.CompilerParams(dimension_semantics=sem, vmem_limit_bytes=limit)


def _pick(n, prefs):
    for p in prefs:
        if n % p == 0:
            return p
    raise ValueError(f"no tile in {prefs} divides {n}")


def _bdot(a, b):
    return jnp.dot(a.astype(bf16), b.astype(bf16), preferred_element_type=f32)


def _bdot_nt(a, b):
    return lax.dot_general(a.astype(bf16), b.astype(bf16), (((1,), (1,)), ((), ())),
                           preferred_element_type=f32)


def _bdot_tn(a, b):
    return lax.dot_general(a.astype(bf16), b.astype(bf16), (((0,), (0,)), ((), ())),
                           preferred_element_type=f32)


def _split2(a):
    hi = a.astype(bf16)
    lo = (a - hi.astype(f32)).astype(bf16)
    return hi, lo


def _split3(a):
    hi = a.astype(bf16)
    r = a - hi.astype(f32)
    mid = r.astype(bf16)
    lo = (r - mid.astype(f32)).astype(bf16)
    return hi, mid, lo


def _dot3(a, b):
    ah, al = _split2(a)
    bh, bl = _split2(b)
    d = functools.partial(jnp.dot, preferred_element_type=f32)
    return d(ah, bh) + (d(ah, bl) + d(al, bh))


def _dot3_nt(a, b):
    ah, al = _split2(a)
    bh, bl = _split2(b)
    d = functools.partial(lax.dot_general, dimension_numbers=(((1,), (1,)), ((), ())),
                          preferred_element_type=f32)
    return d(ah, bh) + (d(ah, bl) + d(al, bh))


def _sel_dot(sel, b):
    bh, bm, bl = _split3(b)
    d = functools.partial(jnp.dot, preferred_element_type=f32)
    return d(sel, bh) + (d(sel, bm) + d(sel, bl))


def _sel_dot_nt(sel, b):
    bh, bm, bl = _split3(b)
    d = functools.partial(lax.dot_general, dimension_numbers=(((1,), (1,)), ((), ())),
                          preferred_element_type=f32)
    return d(sel, bh) + (d(sel, bm) + d(sel, bl))


def _silu(x):
    return x * jax.nn.sigmoid(x)


def _softplus(x):
    return jnp.maximum(x, 0.0) + jnp.log(1.0 + jnp.exp(-jnp.abs(x)))


def _iota(shape, dim):
    return lax.broadcasted_iota(jnp.int32, shape, dim)


def _rms_heads(o, g):
    return o * lax.rsqrt(jnp.mean(o * o, axis=-1, keepdims=True) + EPS) * g


def _mod_kernel(c_ref, w_ref, b_ref, o_ref):
    c = c_ref[...]
    o_ref[...] = _bdot(_silu(c), w_ref[...]) + b_ref[...]


def _mod_call(c16, ada_w, ada_b):
    depth, d, n = ada_w.shape
    tn = _pick(n, (512, 256, 128))
    need = 2 * d * tn * 4 + d * tn * 2 + 4 * 16 * tn * 4 + 2 * 16 * d * 4
    return pl.pallas_call(
        _mod_kernel,
        grid=(depth, n // tn),
        in_specs=[pl.BlockSpec((16, d), lambda l, j: (0, 0)),
                  pl.BlockSpec((None, d, tn), lambda l, j: (l, 0, j)),
                  pl.BlockSpec((None, 1, tn), lambda l, j: (l, 0, j))],
        out_specs=pl.BlockSpec((None, 16, tn), lambda l, j: (l, 0, j)),
        out_shape=jax.ShapeDtypeStruct((depth, 16, n), f32),
        compiler_params=_cparams(("arbitrary", "arbitrary"), need),
        name="adaln_mod",
    )(c16, ada_w, ada_b.reshape(depth, 1, n))


def _premod_kernel(x_ref, g_ref, sh_ref, sc_ref, o_ref):
    g = g_ref[...]
    for s in range(TM // SUB):
        rows = slice(s * SUB, (s + 1) * SUB)
        xs = x_ref[rows, :]
        y = xs * lax.rsqrt(jnp.mean(xs * xs, axis=-1, keepdims=True) + EPS) * g
        o_ref[rows, :] = (y * (1.0 + sc_ref[s:s + 1, :]) + sh_ref[s:s + 1, :]).astype(o_ref.dtype)


def _premod_call(x, g, mod8, v_shift, v_scale):
    m, d = x.shape
    nsub = TM // SUB
    need = 2 * TM * d * 4 + 2 * TM * d * 2 + 6 * nsub * d * 4
    return pl.pallas_call(
        _premod_kernel,
        grid=(m // TM,),
        in_specs=[pl.BlockSpec((TM, d), lambda i: (i, 0)),
                  pl.BlockSpec((1, d), lambda i: (0, 0)),
                  pl.BlockSpec((nsub, d), lambda i: (i, v_shift)),
                  pl.BlockSpec((nsub, d), lambda i: (i, v_scale))],
        out_specs=pl.BlockSpec((TM, d), lambda i: (i, 0)),
        out_shape=jax.ShapeDtypeStruct((m, d), bf16),
        compiler_params=_cparams(("arbitrary",), need),
        name="modulate",
    )(x, g.reshape(1, d), mod8, mod8)


def _pack_kernel(p_ref, s_ref, o_ref, *, n_p):
    i = pl.program_id(0)

    @pl.when(i < n_p)
    def _():
        o_ref[...] = p_ref[...]

    @pl.when(i >= n_p)
    def _():
        o_ref[...] = s_ref[...]


def _pack_call(p_rows, s_rows):
    bs, d = p_rows.shape
    n_p, n_s = bs // TM, s_rows.shape[0] // TM
    return pl.pallas_call(
        functools.partial(_pack_kernel, n_p=n_p),
        grid=(n_p + n_s,),
        in_specs=[pl.BlockSpec((TM, d), lambda i: (jnp.minimum(i, n_p - 1), 0)),
                  pl.BlockSpec((TM, d), lambda i: (jnp.maximum(i - n_p, 0), 0))],
        out_specs=pl.BlockSpec((TM, d), lambda i: (i, 0)),
        out_shape=jax.ShapeDtypeStruct((bs + s_rows.shape[0], d), p_rows.dtype),
        compiler_params=_cparams(("arbitrary",), 6 * TM * d * 4),
        name="pack_tokens",
    )(p_rows, s_rows)


def _rms_kernel(x_ref, g_ref, o_ref):
    x = x_ref[...]
    o_ref[...] = x * lax.rsqrt(jnp.mean(x * x, axis=-1, keepdims=True) + EPS) * g_ref[...]


def _rms_call(x, g, row0, nrows):
    d = x.shape[1]
    blk0 = row0 // TM
    return pl.pallas_call(
        _rms_kernel,
        grid=(nrows // TM,),
        in_specs=[pl.BlockSpec((TM, d), lambda i: (blk0 + i, 0)), pl.BlockSpec((1, d), lambda i: (0, 0))],
        out_specs=pl.BlockSpec((TM, d), lambda i: (i, 0)),
        out_shape=jax.ShapeDtypeStruct((nrows, d), f32),
        compiler_params=_cparams(("arbitrary",), 4 * TM * d * 4),
        name="final_norm",
    )(x, g.reshape(1, d))


MM_VMEM_TARGET = 48 << 20


V7X_BF16_FLOPS = 1.15e15
V7X_HBM_BYTES_PER_S = 3.0e12
STEP_OVERHEAD_S = 0.35e-6
MXU_EFF_BY_TM = {1408: 0.90, 768: 0.85, 512: 0.80, 256: 0.70}


def _ws_plan(m, k, n, n_weights, tile_bytes_per_out_elem, io_bytes_per_out_elem, tns=(1024, 512, 256, 128),
             tms=(768, 512, 256)):
    best = None
    for tn in tns:
        if n % tn:
            continue
        for tm in tms:
            if m % tm:
                continue
            for bufs in (2, 1):
                need = n_weights * k * tn * (4 * bufs + 2) + 2 * tm * k * 2 + tm * tn * tile_bytes_per_out_elem
                if need > MM_VMEM_TARGET:
                    continue
                w_bytes = n_weights * k * n * 4
                hbm = (n // tn) * m * k * 2 + w_bytes + m * n * io_bytes_per_out_elem
                t = max(2.0 * n_weights * m * k * n / V7X_BF16_FLOPS / MXU_EFF_BY_TM[tm], hbm / V7X_HBM_BYTES_PER_S)
                t += (n // tn) * (m // tm) * STEP_OVERHEAD_S
                if bufs == 1:
                    t += w_bytes / V7X_HBM_BYTES_PER_S
                if best is None or t < best[0]:
                    best = (t, tm, tn, bufs, need)
    if best is None:
        raise ValueError(f"no weight-stationary tiling for {(m, k, n)}")
    return best[1:]


def _wspec(shape, index_map, bufs):
    if bufs == 1:
        return pl.BlockSpec(shape, index_map, pipeline_mode=pl.Buffered(1))
    return pl.BlockSpec(shape, index_map)


def _mm_plain_kernel(a_ref, w_ref, o_ref, wb_ref):
    @pl.when(pl.program_id(1) == 0)
    def _():
        wb_ref[...] = w_ref[...].astype(bf16)

    o_ref[...] = jnp.dot(a_ref[...], wb_ref[...], preferred_element_type=f32).astype(o_ref.dtype)


def _mm_plain_call(a, w3, layer, n_cols, name):
    m, k = a.shape
    tm, tn, bufs, need = _ws_plan(m, k, n_cols, 1, 2 * 4 + 4, 4)
    return pl.pallas_call(
        _mm_plain_kernel,
        grid=(n_cols // tn, m // tm),
        in_specs=[pl.BlockSpec((tm, k), lambda j, i: (i, 0)),
                  _wspec((None, k, tn), lambda j, i: (layer, 0, j), bufs)],
        out_specs=pl.BlockSpec((tm, tn), lambda j, i: (i, j)),
        out_shape=jax.ShapeDtypeStruct((m, n_cols), f32),
        scratch_shapes=[pltpu.VMEM((k, tn), bf16)],
        compiler_params=_cparams(("arbitrary", "arbitrary"), need),
        name=name,
    )(a, w3)


def _mm_up_kernel(a_ref, wg_ref, wu_ref, o_ref, wgb_ref, wub_ref):
    @pl.when(pl.program_id(1) == 0)
    def _():
        wgb_ref[...] = wg_ref[...].astype(bf16)
        wub_ref[...] = wu_ref[...].astype(bf16)

    a = a_ref[...]
    g = jnp.dot(a, wgb_ref[...], preferred_element_type=f32)
    u = jnp.dot(a, wub_ref[...], preferred_element_type=f32)
    o_ref[...] = (_silu(g) * u).astype(o_ref.dtype)


def _mm_up_call(a, wg3, wu3, layer):
    m, k = a.shape
    f = wg3.shape[2]
    tm, tn, bufs, need = _ws_plan(m, k, LANES * 8, 2, 2 * 2 + 3 * 4, 2, tns=(512, 256, 128),
                                  tms=(1408, 768, 512, 256))
    tn = min(tn, f)
    n_main = f // tn
    rem = f - n_main * tn
    assert rem % LANES == 0 and (rem == 0 or (n_main * tn) % rem == 0)

    def call(width, col_blk0, ncols, name):
        wspec = _wspec((None, k, width), lambda j, i: (layer, 0, col_blk0 + j), bufs)
        return pl.pallas_call(
            _mm_up_kernel,
            grid=(ncols, m // tm),
            in_specs=[pl.BlockSpec((tm, k), lambda j, i: (i, 0)), wspec, wspec],
            out_specs=pl.BlockSpec((tm, width), lambda j, i: (i, j)),
            out_shape=jax.ShapeDtypeStruct((m, ncols * width), bf16),
            scratch_shapes=[pltpu.VMEM((k, width), bf16), pltpu.VMEM((k, width), bf16)],
            compiler_params=_cparams(("arbitrary", "arbitrary"), need),
            name=name,
        )(a, wg3, wu3)

    outs = [call(tn, 0, n_main, "ffn_up")]
    if rem:
        outs.append(call(rem, (n_main * tn) // rem, 1, "ffn_up_tail"))
    return outs


def _mm_res_kernel(*refs, nseg, scale):
    a_refs, w_refs = refs[:nseg], refs[nseg:2 * nseg]
    r_ref, gate_ref, o_ref = refs[2 * nseg:2 * nseg + 3]
    wb_refs = refs[2 * nseg + 3:]

    @pl.when(pl.program_id(1) == 0)
    def _():
        for w_ref, wb_ref in zip(w_refs, wb_refs):
            wb_ref[...] = w_ref[...].astype(bf16)

    acc = jnp.dot(a_refs[0][...], wb_refs[0][...], preferred_element_type=f32)
    for a_ref, wb_ref in zip(a_refs[1:], wb_refs[1:]):
        acc = acc + jnp.dot(a_ref[...], wb_ref[...], preferred_element_type=f32)
    for s in range(acc.shape[0] // SUB):
        rows = slice(s * SUB, (s + 1) * SUB)
        o_ref[rows, :] = r_ref[rows, :] + (scale * gate_ref[s:s + 1, :]) * acc[rows, :]


def _mm_res_call(acts, w3, layer, res, mod8, v_gate, scale, name):
    m, k0 = acts[0].shape
    n = w3.shape[2]
    nk = 1
    while (k0 // nk) * 1024 * 6 > (36 << 20) and (k0 // nk) % (2 * LANES) == 0:
        nk *= 2
    kc = k0 // nk
    k_rest = sum(a.shape[1] for a in acts[1:])
    tm, tn, bufs, need = _ws_plan(m, kc + k_rest, n, 1, 4 * 4 + 2 * 4 + 4, 8 * nk)
    nsub = tm // SUB
    out = res
    for kb in range(nk):
        segs = [(acts[0], kc, kb, kb)]
        row0 = k0
        if kb == nk - 1:
            for a in acts[1:]:
                assert row0 % a.shape[1] == 0
                segs.append((a, a.shape[1], 0, row0 // a.shape[1]))
                row0 += a.shape[1]
        a_specs = [pl.BlockSpec((tm, w), lambda j, i, cb=cb: (i, cb)) for _, w, cb, _ in segs]
        w_specs = [_wspec((None, w, tn), lambda j, i, rb=rb: (layer, rb, j), bufs) for _, w, _, rb in segs]
        out = pl.pallas_call(
            functools.partial(_mm_res_kernel, nseg=len(segs), scale=scale),
            grid=(n // tn, m // tm),
            in_specs=a_specs + w_specs + [pl.BlockSpec((tm, tn), lambda j, i: (i, j)),
                                          pl.BlockSpec((nsub, tn), lambda j, i: (i, v_gate * (n // tn) + j))],
            out_specs=pl.BlockSpec((tm, tn), lambda j, i: (i, j)),
            out_shape=jax.ShapeDtypeStruct((m, n), f32),
            scratch_shapes=[pltpu.VMEM((w, tn), bf16) for _, w, _, _ in segs],
            compiler_params=_cparams(("arbitrary", "arbitrary"), need),
            name=f"{name}_k{kb}",
        )(*[s[0] for s in segs], *([w3] * len(segs)), out, mod8)
    return out


def _rope_kernel(x_ref, cos_ref, sin_ref, q_ref, k_ref):
    x = x_ref[...]
    w = x.shape[1]
    nh = w // HEAD_DIM
    cosf = jnp.concatenate([cos_ref[...]] * nh, axis=1)
    sinf = jnp.concatenate([sin_ref[...]] * nh, axis=1)
    lane = _iota(x.shape, 1) % HEAD_DIM
    half = ROPE_DIM // 2
    partner = jnp.where(lane < half, pltpu.roll(x, w - half, 1), pltpu.roll(x, half, 1))
    y = x * cosf + partner * sinf
    qw = q_ref.shape[1]
    q_ref[...] = y[:, :qw]
    k_ref[...] = y[:, qw:]


def _rope_call(proj, cos_t, sin_t, qw, kw):
    m = proj.shape[0]
    w = qw + kw
    return pl.pallas_call(
        _rope_kernel,
        grid=(m // TM,),
        in_specs=[pl.BlockSpec((TM, w), lambda i: (i, 0)),
                  pl.BlockSpec((TM, HEAD_DIM), lambda i: (i, 0)),
                  pl.BlockSpec((TM, HEAD_DIM), lambda i: (i, 0))],
        out_specs=[pl.BlockSpec((TM, qw), lambda i: (i, 0)), pl.BlockSpec((TM, kw), lambda i: (i, 0))],
        out_shape=[jax.ShapeDtypeStruct((m, qw), f32), jax.ShapeDtypeStruct((m, kw), f32)],
        compiler_params=_cparams(("arbitrary",), 10 * TM * w * 4),
        name="rope",
    )(proj, cos_t, sin_t)


def _top_blocks(gate, n_valid):
    lane = _iota(gate.shape, 1)
    gate = jnp.where(lane < n_valid, gate, -jnp.inf)
    picks = []
    for kk in range(MOBA_TOPK):
        mx = jnp.max(gate, axis=-1, keepdims=True)
        idx = jnp.min(jnp.where(gate == mx, lane, LANES), axis=-1, keepdims=True)
        picks.append(jnp.where(kk < n_valid, idx, -1))
        gate = jnp.where(lane == idx, -jnp.inf, gate)
    return picks


def _top_blocks_t(gate, n_valid):
    blk = _iota(gate.shape, 0)
    gate = jnp.where(blk < n_valid, gate, -jnp.inf)
    picks = []
    for kk in range(MOBA_TOPK):
        mx = jnp.max(gate, axis=0, keepdims=True)
        idx = jnp.min(jnp.where(gate == mx, blk, gate.shape[0]), axis=0, keepdims=True)
        picks.append(jnp.where(kk < n_valid, idx, -1))
        gate = jnp.where(blk == idx, -jnp.inf, gate)
    return picks


def _moba_prompt_kernel(q_ref, k_ref, v_ref, g_ref, o_ref, kmean_sc):
    i = pl.program_id(2)
    s_len = k_ref.shape[0]
    nb = s_len // MOBA_BLOCK
    scale = HEAD_DIM ** -0.5
    q2 = jnp.concatenate([q_ref[:, :HEAD_DIM], q_ref[:, HEAD_DIM:]], axis=0)
    rows = q2.shape[0]
    @pl.when(i == 0)
    def _():
        blk_row = _iota(kmean_sc.shape, 0)
        kmean = jnp.zeros(kmean_sc.shape, f32)
        for n in range(nb):
            mean_n = jnp.mean(k_ref[n * MOBA_BLOCK:(n + 1) * MOBA_BLOCK, :], axis=0, keepdims=True)
            kmean = jnp.where(blk_row == n, mean_n, kmean)
        kmean_sc[...] = kmean

    gate_t = _dot3_nt(kmean_sc[...], q2)
    picks_t = _top_blocks_t(gate_t, i)
    prow = _iota((SUBLANES, rows), 0)
    pk = jnp.zeros((SUBLANES, rows), f32)
    for kk in range(MOBA_TOPK):
        pk = jnp.where(prow == kk, picks_t[kk].astype(f32), pk)
    picks = [lax.dot_general(pk, (_iota((SUBLANES, LANES), 0) == kk).astype(f32), (((0,), (0,)), ((), ())),
                             preferred_element_type=f32) for kk in range(MOBA_TOPK)]
    qs = (q2 * scale).astype(bf16)

    own = pl.multiple_of(i * MOBA_BLOCK, MOBA_BLOCK)
    s = _bdot_nt(qs, k_ref[pl.ds(own, MOBA_BLOCK), :])
    rq = _iota(s.shape, 0) % MOBA_BLOCK
    ck = _iota(s.shape, 1)
    s = jnp.where(ck <= rq, s, -jnp.inf)
    m0 = jnp.max(s, axis=-1, keepdims=True)
    p = jnp.exp(s - m0)
    l0 = jnp.sum(p, axis=-1, keepdims=True)
    acc0 = _bdot(p, v_ref[pl.ds(own, MOBA_BLOCK), :])

    def body(n, carry):
        m, l, acc = carry
        start = pl.multiple_of(n * MOBA_BLOCK, MOBA_BLOCK)
        sn = _bdot_nt(qs, k_ref[pl.ds(start, MOBA_BLOCK), :])
        nf = n.astype(f32)
        sel = (picks[0] == nf) | (picks[1] == nf) | (picks[2] == nf)
        sn = jnp.concatenate([jnp.where(sel, sn[:, t * LANES:(t + 1) * LANES], -jnp.inf)
                              for t in range(MOBA_BLOCK // LANES)], axis=1)
        m_new = jnp.maximum(m, jnp.max(sn, axis=-1, keepdims=True))
        alpha = jnp.exp(m - m_new)
        pn = jnp.exp(sn - m_new)
        l_new = alpha * l + jnp.sum(pn, axis=-1, keepdims=True)
        acc_new = alpha * acc + _bdot(pn, v_ref[pl.ds(start, MOBA_BLOCK), :])
        return m_new, l_new, acc_new

    m, l, acc = lax.fori_loop(0, i, body, (m0, l0, acc0))
    o = _rms_heads(acc / l, g_ref[...])
    half = rows // 2
    o_ref[:, :HEAD_DIM] = o[:half].astype(o_ref.dtype)
    o_ref[:, HEAD_DIM:] = o[half:].astype(o_ref.dtype)


def _moba_prompt_call(q_rot, k_rot, proj, gnorm, b, s_len, kvh, v_col0):
    qt = s_len // MOBA_BLOCK
    vb = v_col0 // HEAD_DIM
    need = 4 * s_len * HEAD_DIM * 4 + 4 * MOBA_BLOCK * 2 * HEAD_DIM * 4 + 16 * 2 * MOBA_BLOCK * MOBA_BLOCK * 4
    return pl.pallas_call(
        _moba_prompt_kernel,
        grid=(b, kvh, qt),
        in_specs=[pl.BlockSpec((MOBA_BLOCK, 2 * HEAD_DIM), lambda bi, h, i: (bi * qt + i, h)),
                  pl.BlockSpec((s_len, HEAD_DIM), lambda bi, h, i: (bi, h)),
                  pl.BlockSpec((s_len, HEAD_DIM), lambda bi, h, i: (bi, vb + h)),
                  pl.BlockSpec((1, HEAD_DIM), lambda bi, h, i: (0, 0))],
        out_specs=pl.BlockSpec((MOBA_BLOCK, 2 * HEAD_DIM), lambda bi, h, i: (bi * qt + i, h)),
        out_shape=jax.ShapeDtypeStruct((b * s_len, 2 * kvh * HEAD_DIM), bf16),
        scratch_shapes=[pltpu.VMEM((-(-qt // SUBLANES) * SUBLANES, HEAD_DIM), f32)],
        compiler_params=_cparams(("arbitrary", "arbitrary", "arbitrary"), need),
        name="moba_prompt",
    )(q_rot, k_rot, proj, gnorm.reshape(1, HEAD_DIM))


QROWS = 8


def _moba_sample_kernel(pt_ref, q_ref, kn_ref, vn_ref, *refs, nb, bps, kvh, t_len):
    del pt_ref
    npg = bps * (MOBA_BLOCK // PAGE_SIZE)
    k_pages, v_pages = refs[:npg], refs[npg:2 * npg]
    g_ref, o_ref, qs_sc, s_sc, gate_sc, idx_sc, m_sc, l_sc, acc_sc = refs[2 * npg:]
    j = pl.program_id(1)
    nsk = nb // bps
    scale = HEAD_DIM ** -0.5
    r2 = 2 * QROWS

    def block_of(pages, bb, h):
        ppb = MOBA_BLOCK // PAGE_SIZE
        return jnp.concatenate([pages[bb * ppb + t][pl.ds(h, PAGE_SIZE, stride=kvh), :] for t in range(ppb)], axis=0)

    @pl.when(j == 0)
    def _():
        for h in range(kvh):
            q2 = jnp.concatenate([q_ref[0:QROWS, (2 * h) * HEAD_DIM:(2 * h + 1) * HEAD_DIM],
                                  q_ref[0:QROWS, (2 * h + 1) * HEAD_DIM:(2 * h + 2) * HEAD_DIM]], axis=0)
            qs_sc[h] = q2
        gate_sc[...] = jnp.zeros_like(gate_sc)

    @pl.when(j < nsk)
    def _():
        lane = _iota((r2, LANES), 1)
        pairs = [(bb, h) for bb in range(bps) for h in range(kvh)]
        q2s = [qs_sc[h] for h in range(kvh)]
        khs = [block_of(k_pages, bb, h) for bb, h in pairs]
        scores = [_bdot_nt(q2s[h] * scale, kh) for (bb, h), kh in zip(pairs, khs)]
        kmeans = [jnp.sum(kh, axis=0, keepdims=True) * (1.0 / MOBA_BLOCK) for kh in khs]
        cols = [jnp.sum(q2s[h] * km, axis=-1, keepdims=True) for (bb, h), km in zip(pairs, kmeans)]
        for (bb, h), sc in zip(pairs, scores):
            s_sc[j * bps + bb, h] = sc
        for h in range(kvh):
            gate = gate_sc[h]
            for (bb, hh), col in zip(pairs, cols):
                if hh == h:
                    gate = jnp.where(lane == j * bps + bb, col, gate)
            gate_sc[h] = gate

    @pl.when(j == nsk - 1)
    def _():
        for h in range(kvh):
            picks = _top_blocks(gate_sc[h], nb)
            for kk in range(MOBA_TOPK):
                idx_sc[h * MOBA_TOPK + kk] = jnp.broadcast_to(picks[kk], (r2, LANES))
            kn = jnp.concatenate([kn_ref[0:QROWS, h * HEAD_DIM:(h + 1) * HEAD_DIM],
                                  jnp.zeros((LANES - QROWS, HEAD_DIM), f32)], axis=0)
            vn = jnp.concatenate([vn_ref[0:QROWS, h * HEAD_DIM:(h + 1) * HEAD_DIM],
                                  jnp.zeros((LANES - QROWS, HEAD_DIM), f32)], axis=0)
            s = _bdot_nt(qs_sc[h] * scale, kn)
            tq = _iota(s.shape, 0) % QROWS
            ck = _iota(s.shape, 1)
            s = jnp.where((ck <= tq) & (ck < t_len), s, -jnp.inf)
            m0 = jnp.max(s, axis=-1, keepdims=True)
            p = jnp.exp(s - m0)
            m_sc[h] = jnp.broadcast_to(m0, (r2, LANES))
            l_sc[h] = jnp.broadcast_to(jnp.sum(p, axis=-1, keepdims=True), (r2, LANES))
            acc_sc[h] = _bdot(p, vn)

    @pl.when(j >= nsk)
    def _():
        heads_ = range(kvh)
        picks = [[idx_sc[h * MOBA_TOPK + kk][:, 0:1] for kk in range(MOBA_TOPK)] for h in heads_]
        sns = []
        for h in heads_:
            row = []
            for bb in range(bps):
                n = (j - nsk) * bps + bb
                sel = (picks[h][0] == n) | (picks[h][1] == n) | (picks[h][2] == n)
                row.append(jnp.where(sel, s_sc[n, h], -jnp.inf))
            sns.append(row)
        ms = [m_sc[h][:, 0:1] for h in heads_]
        m_news = []
        for h in heads_:
            mx = ms[h]
            for sn in sns[h]:
                mx = jnp.maximum(mx, jnp.max(sn, axis=-1, keepdims=True))
            m_news.append(mx)
        alphas = [jnp.exp(ms[h] - m_news[h]) for h in heads_]
        pns = [[jnp.exp(sn - m_news[h]) for sn in sns[h]] for h in heads_]
        pvs = [[_bdot(pns[h][bb], block_of(v_pages, bb, h)) for bb in range(bps)] for h in heads_]
        for h in heads_:
            l = alphas[h] * l_sc[h][:, 0:1]
            acc = alphas[h] * acc_sc[h]
            for bb in range(bps):
                l = l + jnp.sum(pns[h][bb], axis=-1, keepdims=True)
                acc = acc + pvs[h][bb]
            m_sc[h] = jnp.broadcast_to(m_news[h], (r2, LANES))
            l_sc[h] = jnp.broadcast_to(l, (r2, LANES))
            acc_sc[h] = acc

    @pl.when(j == 2 * nsk - 1)
    def _():
        for h in range(kvh):
            o = _rms_heads(acc_sc[h] / l_sc[h][:, 0:1], g_ref[...])
            o_ref[:, (2 * h) * HEAD_DIM:(2 * h + 1) * HEAD_DIM] = o[:QROWS]
            o_ref[:, (2 * h + 1) * HEAD_DIM:(2 * h + 2) * HEAD_DIM] = o[QROWS:]


def _moba_sample_call(page_table, q_rot, k_rot, proj, cache_k, cache_v, gnorm, layer,
                      db, row0, kvh, v_col0, t_len):
    n_pages = page_table.shape[1]
    nb = n_pages * PAGE_SIZE // MOBA_BLOCK
    ppb = MOBA_BLOCK // PAGE_SIZE
    bps = _pick(nb, (8, 4, 2, 1))
    nsk = nb // bps
    npg = bps * ppb
    qw = 2 * kvh * HEAD_DIM
    kw = kvh * HEAD_DIM
    blk0 = row0 // SUB

    def kmap(t):
        return lambda b, j, pt: (layer, pt[b, npg * jnp.minimum(j, nsk - 1) + t], 0, 0)

    def vmap_(t):
        return lambda b, j, pt: (layer, pt[b, npg * jnp.maximum(j - nsk, 0) + t], 0, 0)

    depth, n_pool = cache_k.shape[:2]
    cache_k = cache_k.reshape(depth, n_pool, PAGE_SIZE * kvh, HEAD_DIM)
    cache_v = cache_v.reshape(depth, n_pool, PAGE_SIZE * kvh, HEAD_DIM)
    page_spec = lambda fn: pl.BlockSpec((None, None, PAGE_SIZE * kvh, HEAD_DIM), fn)
    r2 = 2 * QROWS
    need = 4 * npg * PAGE_SIZE * kvh * HEAD_DIM * 4 + nb * kvh * r2 * MOBA_BLOCK * 4 + (8 << 20)
    grid_spec = pltpu.PrefetchScalarGridSpec(
        num_scalar_prefetch=1,
        grid=(db, 2 * nsk),
        in_specs=[pl.BlockSpec((SUB, qw), lambda b, j, pt: (blk0 + b, 0)),
                  pl.BlockSpec((SUB, kw), lambda b, j, pt: (blk0 + b, 0)),
                  pl.BlockSpec((SUB, kw), lambda b, j, pt: (blk0 + b, v_col0 // kw))]
                 + [page_spec(kmap(t)) for t in range(npg)] + [page_spec(vmap_(t)) for t in range(npg)]
                 + [pl.BlockSpec((1, HEAD_DIM), lambda b, j, pt: (0, 0))],
        out_specs=pl.BlockSpec((QROWS, qw), lambda b, j, pt: (b, 0)),
        scratch_shapes=[pltpu.VMEM((kvh, r2, HEAD_DIM), f32),
                        pltpu.VMEM((nb, kvh, r2, MOBA_BLOCK), f32),
                        pltpu.VMEM((kvh, r2, LANES), f32),
                        pltpu.VMEM((kvh * MOBA_TOPK, r2, LANES), jnp.int32),
                        pltpu.VMEM((kvh, r2, LANES), f32),
                        pltpu.VMEM((kvh, r2, LANES), f32),
                        pltpu.VMEM((kvh, r2, HEAD_DIM), f32)])
    return pl.pallas_call(
        functools.partial(_moba_sample_kernel, nb=nb, bps=bps, kvh=kvh, t_len=t_len),
        grid_spec=grid_spec,
        out_shape=jax.ShapeDtypeStruct((db * QROWS, qw), f32),
        compiler_params=_cparams(("arbitrary", "arbitrary"), need),
        name="moba_sample",
    )(page_table, q_rot, k_rot, proj, *([cache_k] * npg), *([cache_v] * npg), gnorm.reshape(1, HEAD_DIM))


def _inv_unit_lower(lmats, nblk):
    c = lmats[0].shape[0]
    ii = _iota((c, c), 0)
    jj = _iota((c, c), 1)
    eye = (ii == jj).astype(f32)
    same = (ii // INV_BLK) == (jj // INV_BLK)
    dmats = [jnp.where(same, l, 0.0) for l in lmats]
    ps = [-d for d in dmats]
    xs = [eye + p for p in ps]
    k = 2
    while k < INV_BLK:
        ps = [_bdot(p, p) for p in ps]
        xs = [x + _bdot(x, p) for x, p in zip(xs, ps)]
        k *= 2
    if nblk > 1:
        mms = [_bdot(x, l - d) for x, l, d in zip(xs, lmats, dmats)]
        ys = [eye - mm for mm in mms]
        pms = mms
        k = 2
        while k < nblk:
            pms = [_bdot(pm, pm) for pm in pms]
            ys = [y + _bdot(y, pm) for y, pm in zip(ys, pms)]
            k *= 2
        xs = [_bdot(y, x) for y, x in zip(ys, xs)]
    for _ in range(2):
        rs = [eye - _dot3(eye + l, x) for l, x in zip(lmats, xs)]
        xs = [x + _bdot(x, r) for x, r in zip(xs, rs)]
    return xs


def _gdn_kernel(main_ref, ab_ref, cw_ref, prev_ref, alog_ref, dtb_ref, s0_ref, g_ref, o_ref, s_ref, carry_sc,
                *, heads, group, col0, t_valid):
    c_idx = pl.program_id(1)
    c = main_ref.shape[0]
    gw = heads * HEAD_DIM
    ch = 3 * gw

    @pl.when(c_idx == 0)
    def _():
        s_ref[...] = s0_ref[...]
        carry_sc[0:SUBLANES, :] = prev_ref[...]

    x = main_ref[:, col0:col0 + ch]
    carry_sc[SUBLANES:, :] = x
    y = cw_ref[GDN_CONV - 1:GDN_CONV, :] * x
    for tap in range(1, GDN_CONV):
        y = y + cw_ref[GDN_CONV - 1 - tap:GDN_CONV - tap, :] * carry_sc[SUBLANES - tap:SUBLANES - tap + c, :]
    carry_sc[0:SUBLANES, :] = x[c - SUBLANES:, :]
    y = _silu(y)

    ab = ab_ref[...]
    lane = _iota(ab.shape, 1)
    row_ok = (c_idx * c + _iota((c, 1), 0)) < t_valid
    g_all = jnp.where(row_ok & (lane < heads), -jnp.exp(alog_ref[...]) * _softplus(ab + dtb_ref[...]), 0.0)
    beta_all = jnp.where(row_ok, jax.nn.sigmoid(ab), 0.0)
    tri = (_iota((c, c), 0) >= _iota((c, c), 1)).astype(bf16)
    gam_all = _sel_dot(tri, g_all)
    eye_l = (_iota((LANES, LANES), 0) == _iota((LANES, LANES), 1)).astype(bf16)
    gam_t = _sel_dot_nt(eye_l, gam_all)
    gnorm = g_ref[...]

    r = group * c
    ii = _iota((r, r), 0)
    jj = _iota((r, r), 1)
    same = (ii // c) == (jj // c)
    low = same & (ii >= jj)
    strict = same & (ii > jj)
    groups = [range(g * group, (g + 1) * group) for g in range(heads // group)]
    stack = lambda hs, f: jnp.concatenate([f(h) for h in hs], axis=0)
    qs, ks, vs, betas, gcols, decays, egams, kds = [], [], [], [], [], [], [], []
    for hs in groups:
        xq = stack(hs, lambda h: y[:, h * HEAD_DIM:(h + 1) * HEAD_DIM])
        xk = stack(hs, lambda h: y[:, gw + h * HEAD_DIM:gw + (h + 1) * HEAD_DIM])
        vs.append(stack(hs, lambda h: y[:, 2 * gw + h * HEAD_DIM:2 * gw + (h + 1) * HEAD_DIM]))
        qs.append(xq * lax.rsqrt(jnp.sum(xq * xq, axis=-1, keepdims=True) + EPS) * (HEAD_DIM ** -0.5))
        ks.append(xk * lax.rsqrt(jnp.sum(xk * xk, axis=-1, keepdims=True) + EPS))
        betas.append(stack(hs, lambda h: beta_all[:, heads + h:heads + h + 1]))
        gcol = stack(hs, lambda h: gam_all[:, h:h + 1])
        glast = stack(hs, lambda h: jnp.broadcast_to(gam_all[c - 1:c, h:h + 1], (c, 1)))
        grow = jnp.concatenate([gam_t[h:h + 1, :] for h in hs], axis=1)
        gcols.append(gcol)
        egams.append(jnp.exp(gcol))
        decays.append(jnp.exp(jnp.where(low, gcol - grow, -jnp.inf)))
        kds.append(ks[-1] * jnp.exp(glast - gcol))
    kks = [_bdot_nt(k, k) for k in ks]
    lmats = [jnp.where(strict, beta * kk * decay, 0.0) for beta, kk, decay in zip(betas, kks, decays)]
    tmats = _inv_unit_lower(lmats, c // INV_BLK)
    us = [_bdot(t, beta * v) for t, beta, v in zip(tmats, betas, vs)]
    ws = [_bdot(t, (beta * egam) * k) for t, beta, egam, k in zip(tmats, betas, egams, ks)]
    qks = [_bdot_nt(q, k) * decay for q, k, decay in zip(qs, ks, decays)]
    qes = [q * egam for q, egam in zip(qs, egams)]
    rows_of = lambda a: slice(a * c, (a + 1) * c)
    states = [[s_ref[h] for h in hs] for hs in groups]
    v_news = [jnp.concatenate([u[rows_of(a), :] - _bdot(w[rows_of(a), :], st[a]) for a in range(group)], axis=0)
              for u, w, st in zip(us, ws, states)]
    o_inters = [jnp.concatenate([_bdot(qe[rows_of(a), :], st[a]) for a in range(group)], axis=0)
                for qe, st in zip(qes, states)]
    os_ = [oi + _bdot(qk, vn) for oi, qk, vn in zip(o_inters, qks, v_news)]
    for hs, st, kd, vn, o in zip(groups, states, kds, v_news, os_):
        for a, h in enumerate(hs):
            s_ref[h] = jnp.exp(gam_all[c - 1:c, h:h + 1]) * st[a] + _bdot_tn(kd[rows_of(a), :], vn[rows_of(a), :])
            z = main_ref[:, col0 + ch + h * HEAD_DIM:col0 + ch + (h + 1) * HEAD_DIM]
            o_ref[:, h * HEAD_DIM:(h + 1) * HEAD_DIM] = (_rms_heads(o[rows_of(a), :], gnorm) * _silu(z)).astype(o_ref.dtype)


def _gdn_call(proj, tail, cw, prev8, alog, dtb, s0, gnorm, layer, *, nseq, rows_per_seq, row0, chunk, heads,
              col0, ab_col0, t_valid, name):
    mainw = proj.shape[1]
    gw = heads * HEAD_DIM
    ch = 3 * gw
    nch = rows_per_seq // chunk
    blk0 = row0 // chunk
    need = 2 * chunk * mainw * 4 + 4 * heads * HEAD_DIM * HEAD_DIM * 4 + 40 * chunk * ch * 4 + (4 << 20)
    return pl.pallas_call(
        functools.partial(_gdn_kernel, heads=heads, group=_pick(heads, (2, 1)), col0=col0, t_valid=t_valid),
        grid=(nseq, nch),
        in_specs=[pl.BlockSpec((chunk, mainw), lambda b, c: (blk0 + b * nch + c, 0)),
                  pl.BlockSpec((chunk, LANES), lambda b, c: (blk0 + b * nch + c, ab_col0 // LANES)),
                  pl.BlockSpec((None, GDN_CONV, ch), lambda b, c: (layer, 0, 0)),
                  pl.BlockSpec((None, SUBLANES, ch), lambda b, c: (b, 0, 0)),
                  pl.BlockSpec((1, LANES), lambda b, c: (0, 0)),
                  pl.BlockSpec((1, LANES), lambda b, c: (0, 0)),
                  pl.BlockSpec((None, heads, HEAD_DIM, HEAD_DIM), lambda b, c: (b, 0, 0, 0)),
                  pl.BlockSpec((1, HEAD_DIM), lambda b, c: (0, 0))],
        out_specs=[pl.BlockSpec((chunk, gw), lambda b, c: (b * nch + c, 0)),
                   pl.BlockSpec((None, heads, HEAD_DIM, HEAD_DIM), lambda b, c: (b, 0, 0, 0))],
        out_shape=[jax.ShapeDtypeStruct((nseq * rows_per_seq, gw), bf16),
                   jax.ShapeDtypeStruct((nseq, heads, HEAD_DIM, HEAD_DIM), f32)],
        scratch_shapes=[pltpu.VMEM((SUBLANES + chunk, ch), f32)],
        compiler_params=_cparams(("arbitrary", "arbitrary"), need),
        name=name,
    )(proj, tail, cw, prev8, alog, dtb, s0, gnorm.reshape(1, HEAD_DIM))


def _gla_kernel(tail_ref, wg_ref, bg_ref, s0_ref, g_ref, o_ref, s_ref, *, heads, ab_col0, t_valid):
    c_idx = pl.program_id(1)
    c = tail_ref.shape[0]
    kw = heads * GLA_DK
    vw = heads * HEAD_DIM

    @pl.when(c_idx == 0)
    def _():
        s_ref[...] = s0_ref[...]

    row_ok = (c_idx * c + _iota((c, 1), 0)) < t_valid
    fblk = tail_ref[:, ab_col0:ab_col0 + LANES]
    pre = _dot3(fblk, wg_ref[...]) + bg_ref[...]
    log_a = jnp.where(row_ok, -_softplus(-pre) * (1.0 / GLA_TAU), 0.0)
    tri = (_iota((c, c), 0) >= _iota((c, c), 1)).astype(bf16)
    bc_all = _sel_dot(tri, log_a)
    gnorm = g_ref[...]
    lane = _iota((c, LANES), 1)
    first = lane < GLA_DK
    ii = _iota((c, LANES), 0)
    jj = lane % GLA_DK
    rowid = _iota((c, 1), 0)
    eye_l = _iota((LANES, LANES), 0) == _iota((LANES, LANES), 1)
    seg = (((_iota((2 * LANES, LANES), 0) % LANES) < GLA_DK) == (_iota((2 * LANES, LANES), 1) < GLA_DK)).astype(bf16)
    zpad = jnp.zeros((GLA_DK - c, LANES), f32) if c < GLA_DK else None

    def stack_pair(a):
        a0 = jnp.where(first, a, 0.0)
        a1 = jnp.where(first, 0.0, a)
        parts = [a0, a1] if zpad is None else [a0, zpad, a1, zpad]
        return jnp.concatenate(parts, axis=0)

    for p in range(heads // 2):
        ls = slice(p * LANES, (p + 1) * LANES)
        q = tail_ref[:, ls] * (GLA_DK ** -0.5)
        k = jnp.where(row_ok, tail_ref[:, kw + p * LANES:kw + (p + 1) * LANES], 0.0)
        v0 = tail_ref[:, 2 * kw + (2 * p) * HEAD_DIM:2 * kw + (2 * p + 1) * HEAD_DIM]
        v1 = tail_ref[:, 2 * kw + (2 * p + 1) * HEAD_DIM:2 * kw + (2 * p + 2) * HEAD_DIM]
        r0_ = tail_ref[:, 2 * kw + vw + (2 * p) * HEAD_DIM:2 * kw + vw + (2 * p + 1) * HEAD_DIM]
        r1_ = tail_ref[:, 2 * kw + vw + (2 * p + 1) * HEAD_DIM:2 * kw + vw + (2 * p + 2) * HEAD_DIM]
        vparts = [v0, v1] if zpad is None else [v0, zpad, v1, zpad]
        v2 = jnp.concatenate(vparts, axis=0).astype(bf16)
        bc = bc_all[:, ls]
        s = s_ref[p]
        qe = q * jnp.exp(bc)
        o0 = _bdot(jnp.where(first, qe, 0.0), s)
        o1 = _bdot(jnp.where(first, 0.0, qe), s)

        pieces = [jnp.zeros((GLA_SUB, LANES), f32)]
        for sb in range(1, c // GLA_SUB):
            r0 = sb * GLA_SUB
            rs = slice(r0, r0 + GLA_SUB)
            b0 = bc[r0:r0 + 1, :]
            q_i = q[rs, :] * jnp.exp(bc[rs, :] - b0)
            k_j = k * jnp.exp(jnp.where(rowid < r0, b0 - bc, -jnp.inf))
            pieces.append(_bdot_nt(q_i, stack_pair(k_j)))
        att_off = jnp.concatenate(pieces, axis=0)

        att = jnp.zeros((c, LANES), f32)
        for dlt in range(GLA_SUB):
            k_r = pltpu.roll(k, dlt, 0) if dlt else k
            bc_r = pltpu.roll(bc, dlt, 0) if dlt else bc
            ok = (rowid % GLA_SUB) >= dlt
            pr = q * k_r * jnp.exp(jnp.where(ok, bc - bc_r, -jnp.inf))
            hi, lo = _split2(pr)
            sums = jnp.dot(jnp.concatenate([hi, lo], axis=1), seg, preferred_element_type=f32)
            att = jnp.where(jj == ii - dlt, sums, att)
        att = att + att_off

        o0 = o0 + _bdot(jnp.where(first, att, 0.0), v2)
        o1 = o1 + _bdot(jnp.where(first, 0.0, att), v2)

        bl = bc[c - 1:c, :]
        ebl_col = jnp.sum(jnp.where(eye_l, jnp.exp(bl), 0.0), axis=-1, keepdims=True)
        s_ref[p] = ebl_col * s + _bdot_tn(stack_pair(k * jnp.exp(bl - bc)), v2)
        o_ref[:, (2 * p) * HEAD_DIM:(2 * p + 1) * HEAD_DIM] = (_rms_heads(o0, gnorm) * _silu(r0_)).astype(o_ref.dtype)
        o_ref[:, (2 * p + 1) * HEAD_DIM:(2 * p + 2) * HEAD_DIM] = (_rms_heads(o1, gnorm) * _silu(r1_)).astype(o_ref.dtype)


def _gla_call(tail, wg_pad, bg, s0, gnorm, *, nseq, rows_per_seq, row0, chunk, heads, ab_col0, t_valid, name):
    tailw = tail.shape[1]
    kw = heads * GLA_DK
    vw = heads * HEAD_DIM
    nch = rows_per_seq // chunk
    blk0 = row0 // chunk
    assert heads % 2 == 0 and chunk <= GLA_DK and chunk % GLA_SUB == 0
    s0 = s0.reshape(nseq, heads // 2, 2 * GLA_DK, HEAD_DIM)
    need = 2 * chunk * tailw * 4 + 4 * heads * GLA_DK * HEAD_DIM * 4 + 2 * LANES * kw * 4 + 40 * chunk * tailw * 4
    o, s_fin = pl.pallas_call(
        functools.partial(_gla_kernel, heads=heads, ab_col0=ab_col0, t_valid=t_valid),
        grid=(nseq, nch),
        in_specs=[pl.BlockSpec((chunk, tailw), lambda b, c: (blk0 + b * nch + c, 0)),
                  pl.BlockSpec((LANES, kw), lambda b, c: (0, 0)),
                  pl.BlockSpec((1, kw), lambda b, c: (0, 0)),
                  pl.BlockSpec((None, heads // 2, 2 * GLA_DK, HEAD_DIM), lambda b, c: (b, 0, 0, 0)),
                  pl.BlockSpec((1, HEAD_DIM), lambda b, c: (0, 0))],
        out_specs=[pl.BlockSpec((chunk, vw), lambda b, c: (b * nch + c, 0)),
                   pl.BlockSpec((None, heads // 2, 2 * GLA_DK, HEAD_DIM), lambda b, c: (b, 0, 0, 0))],
        out_shape=[jax.ShapeDtypeStruct((nseq * rows_per_seq, vw), bf16),
                   jax.ShapeDtypeStruct((nseq, heads // 2, 2 * GLA_DK, HEAD_DIM), f32)],
        compiler_params=_cparams(("arbitrary", "arbitrary"), need),
        name=name,
    )(tail, wg_pad, bg.reshape(1, kw), s0, gnorm.reshape(1, HEAD_DIM))
    return o, s_fin.reshape(nseq, heads, GLA_DK, HEAD_DIM)


def kernel(x_prompt, x_sample, c_prompt, c_sample, cache_k, cache_v, page_table, state_gdn, state_gdn_conv, state_gla, ada_w, ada_b, norm_ffn_a, ffn_a_wg, ffn_a_wu, ffn_a_wd, norm_mix, w_in, moba_norm, gdn_conv_w, gdn_a_log, gdn_dt_bias, gdn_norm, gla_w_gate, gla_b_gate, gla_norm, w_out, norm_ffn_b, ffn_b_wg, ffn_b_wu, ffn_b_wd, final_norm):
    b, s_len, d = x_prompt.shape
    db, t_dec, _ = x_sample.shape
    depth = ada_w.shape[0]
    n_pages = page_table.shape[1]
    past_len = n_pages * PAGE_SIZE

    n_heads = d // HEAD_DIM
    moba_h = n_heads // 4
    kvh = moba_h // 2
    gdn_h = (3 * n_heads) // 8
    gla_h = n_heads - moba_h - gdn_h
    qw, kvw = moba_h * HEAD_DIM, kvh * HEAD_DIM
    gdn_w, gla_kw, gla_w = gdn_h * HEAD_DIM, gla_h * GLA_DK, gla_h * HEAD_DIM
    main_w = qw + 2 * kvw + 4 * gdn_w
    small0 = main_w
    gla0 = main_w + 2 * gdn_h
    f0 = gla0 + 2 * gla_kw + 2 * gla_w
    tail_main = 2 * gla_kw + 2 * gla_w
    assert w_in.shape[2] == f0 + GLA_GATE_RANK
    assert s_len % TM == 0 and s_len % CHUNK == 0 and s_len >= MOBA_TOPK * MOBA_BLOCK
    assert (db * SUB) % TM == 0 and b + db <= 16
    assert GDN_CONV - 1 <= t_dec <= QROWS and past_len % MOBA_BLOCK == 0
    assert 2 * gdn_h + GLA_GATE_RANK <= LANES and tail_main % LANES == 0

    bs = b * s_len
    m_tot = bs + db * SUB
    n_tiles_p = bs // TM

    def pack_rows(p_rows, s_rows):
        s_pad = jnp.pad(s_rows, ((0, 0), (0, SUB - s_rows.shape[1]), (0, 0)))
        return jnp.concatenate([p_rows, s_pad.reshape(db * SUB, s_rows.shape[2])], axis=0)

    def sample_rows(a):
        return a[bs:].reshape(db, SUB, a.shape[1])[:, :t_dec]

    x = _pack_call(x_prompt.reshape(bs, d),
                   jnp.pad(x_sample, ((0, 0), (0, SUB - t_dec), (0, 0))).reshape(db * SUB, d))

    c16 = jnp.concatenate([c_prompt, c_sample, jnp.zeros((16 - b - db, d), f32)], axis=0)
    mod = _mod_call(c16, ada_w, ada_b)

    half = ROPE_DIM // 2
    inv_freq = ROPE_THETA ** (-jnp.arange(half, dtype=f32) / half)
    pos_s = past_len + jnp.minimum(jnp.arange(SUB), t_dec - 1)
    pos = jnp.concatenate([jnp.tile(jnp.arange(s_len), b), jnp.tile(pos_s, db)]).astype(f32)
    ang = pos[:, None] * inv_freq[None, :]
    ones = jnp.ones((m_tot, HEAD_DIM - ROPE_DIM), f32)
    cos_t = jnp.concatenate([jnp.cos(ang), jnp.cos(ang), ones], axis=1)
    sin_t = jnp.concatenate([-jnp.sin(ang), jnp.sin(ang), 0.0 * ones], axis=1)

    tail_w = -(-(tail_main + LANES) // 512) * 512
    w_main = w_in[:, :, :main_w]
    w_tail = jnp.concatenate([w_in[:, :, gla0:f0], w_in[:, :, small0:gla0], w_in[:, :, f0:],
                              jnp.zeros((depth, d, tail_w - tail_main - 2 * gdn_h - GLA_GATE_RANK), f32)], axis=2)
    zeros_prev = jnp.zeros((b, SUBLANES, 3 * gdn_w), f32)

    outs = {k: [] for k in ("kp", "vp", "ks", "vs", "gp", "gs", "cp", "cs", "lp", "ls")}
    for l in range(depth):
        mod8 = jnp.concatenate([jnp.repeat(mod[l, :b], (s_len // TM) * (TM // SUB), axis=0), mod[l, b:b + db]], axis=0)

        h = _premod_call(x, norm_ffn_a[l], mod8, 0, 1)
        a = _mm_up_call(h, ffn_a_wg, ffn_a_wu, l)
        x = _mm_res_call(a, ffn_a_wd, l, x, mod8, 2, 0.5, "ffn_a_down")

        h = _premod_call(x, norm_mix[l], mod8, 3, 4)
        proj = _mm_plain_call(h, w_main, l, main_w, "w_in_main")
        tail = _mm_plain_call(h, w_tail, l, tail_w, "w_in_tail")

        q_rot, k_rot = _rope_call(proj, cos_t, sin_t, qw, kvw)
        v_col0 = qw + kvw
        om_p = _moba_prompt_call(q_rot, k_rot, proj, moba_norm[l], b, s_len, kvh, v_col0)
        om_s = _moba_sample_call(page_table, q_rot, k_rot, proj, cache_k, cache_v, moba_norm[l], l,
                                 db, bs, kvh, v_col0, t_dec)
        om_s = om_s.reshape(db, QROWS, qw)[:, :t_dec].astype(bf16)

        col0 = qw + 2 * kvw
        alog = jnp.zeros((1, LANES), f32).at[0, :gdn_h].set(gdn_a_log[l])
        dtb = jnp.zeros((1, LANES), f32).at[0, :gdn_h].set(gdn_dt_bias[l])
        od_p, gp = _gdn_call(proj, tail, gdn_conv_w, zeros_prev, alog, dtb,
                             jnp.zeros((b, gdn_h, HEAD_DIM, HEAD_DIM), f32), gdn_norm[l], l,
                             nseq=b, rows_per_seq=s_len, row0=0, chunk=CHUNK, heads=gdn_h, col0=col0,
                             ab_col0=tail_main, t_valid=s_len, name="gdn_prompt")
        prev_s = jnp.pad(state_gdn_conv[l], ((0, 0), (SUBLANES - (GDN_CONV - 1), 0), (0, 0)))
        od_s, gs = _gdn_call(proj, tail, gdn_conv_w, prev_s, alog, dtb, state_gdn[l], gdn_norm[l], l,
                             nseq=db, rows_per_seq=SUB, row0=bs, chunk=SUB, heads=gdn_h, col0=col0,
                             ab_col0=tail_main, t_valid=t_dec, name="gdn_sample")

        wg_pad = jnp.zeros((LANES, gla_kw), f32).at[2 * gdn_h:2 * gdn_h + GLA_GATE_RANK].set(gla_w_gate[l])
        ol_p, lp = _gla_call(tail, wg_pad, gla_b_gate[l], jnp.zeros((b, gla_h, GLA_DK, HEAD_DIM), f32), gla_norm[l],
                             nseq=b, rows_per_seq=s_len, row0=0, chunk=CHUNK, heads=gla_h, ab_col0=tail_main,
                             t_valid=s_len, name="gla_prompt")
        ol_s, ls = _gla_call(tail, wg_pad, gla_b_gate[l], state_gla[l], gla_norm[l],
                             nseq=db, rows_per_seq=SUB, row0=bs, chunk=SUB, heads=gla_h, ab_col0=tail_main,
                             t_valid=t_dec, name="gla_sample")

        o_p = jnp.concatenate([om_p, od_p, ol_p], axis=1)
        o_s = jnp.concatenate([om_s, od_s.reshape(db, SUB, gdn_w)[:, :t_dec], ol_s.reshape(db, SUB, gla_w)[:, :t_dec]], axis=2)
        o_mix = pack_rows(o_p, o_s)
        x = _mm_res_call([o_mix], w_out, l, x, mod8, 5, 1.0, "w_out")

        h = _premod_call(x, norm_ffn_b[l], mod8, 6, 7)
        a = _mm_up_call(h, ffn_b_wg, ffn_b_wu, l)
        x = _mm_res_call(a, ffn_b_wd, l, x, mod8, 8, 0.5, "ffn_b_down")

        proj_p = proj[:bs].reshape(b, s_len, main_w)
        proj_s = proj[bs:].reshape(db, SUB, main_w)
        outs["kp"].append(k_rot[:bs].reshape(b, s_len, kvh, HEAD_DIM))
        outs["vp"].append(proj_p[:, :, v_col0:v_col0 + kvw].reshape(b, s_len, kvh, HEAD_DIM))
        outs["ks"].append(sample_rows(k_rot).reshape(db, t_dec, kvh, HEAD_DIM))
        outs["vs"].append(proj_s[:, :t_dec, v_col0:v_col0 + kvw].reshape(db, t_dec, kvh, HEAD_DIM))
        outs["gp"].append(gp)
        outs["gs"].append(gs)
        outs["cp"].append(proj_p[:, s_len - (GDN_CONV - 1):, col0:col0 + 3 * gdn_w])
        outs["cs"].append(proj_s[:, t_dec - (GDN_CONV - 1):t_dec, col0:col0 + 3 * gdn_w])
        outs["lp"].append(lp)
        outs["ls"].append(ls)

    y_p = _rms_call(x, final_norm, 0, bs)
    y_s = _rms_call(x, final_norm, bs, db * SUB)
    st = {k: jnp.stack(v) for k, v in outs.items()}
    return (y_p.reshape(b, s_len, d), y_s.reshape(db, SUB, d)[:, :t_dec],
            st["kp"], st["vp"], st["ks"], st["vs"], st["gp"], st["gs"], st["cp"], st["cs"], st["lp"], st["ls"])
```

```python
import functools
import math

import jax
import jax.numpy as jnp
from jax import lax
from jax.experimental import pallas as pl
from jax.experimental.pallas import tpu as pltpu

f32 = jnp.float32
bf16 = jnp.bfloat16

HEAD_DIM = 128
MOBA_BLOCK = 256
MOBA_TOPK = 3
ROPE_THETA = 500000.0
ROPE_DIM = HEAD_DIM // 4
GDN_CONV = 4
GLA_DK = HEAD_DIM // 2
GLA_GATE_RANK = 16
GLA_TAU = 16.0
GLA_SUB = 16
INV_BLK = 16
CHUNK = 64
PAGE_SIZE = 128
N_MOD = 9
EPS = 1e-6

LANES = 128
SUBLANES = 8
TM = 256
SUB = 32
V7X_VMEM_BYTES = 64 * 1024 * 1024
VMEM_BUDGET = 58 * 1024 * 1024


def _cparams(sem, need_bytes):
    limit = int(min(max(need_bytes * 1.25 + (4 << 20), 16 << 20), VMEM_BUDGET))
    return pltpu.CompilerParams(dimension_semantics=sem, vmem_limit_bytes=limit)


def _pick(n, prefs):
    for p in prefs:
        if n % p == 0:
            return p
    raise ValueError(f"no tile in {prefs} divides {n}")


def _bdot(a, b):
    return jnp.dot(a.astype(bf16), b.astype(bf16), preferred_element_type=f32)


def _bdot_nt(a, b):
    return lax.dot_general(a.astype(bf16), b.astype(bf16), (((1,), (1,)), ((), ())),
                           preferred_element_type=f32)


def _bdot_tn(a, b):
    return lax.dot_general(a.astype(bf16), b.astype(bf16), (((0,), (0,)), ((), ())),
                           preferred_element_type=f32)


def _split2(a):
    hi = a.astype(bf16)
    lo = (a - hi.astype(f32)).astype(bf16)
    return hi, lo


def _split3(a):
    hi = a.astype(bf16)
    r = a - hi.astype(f32)
    mid = r.astype(bf16)
    lo = (r - mid.astype(f32)).astype(bf16)
    return hi, mid, lo


def _dot3(a, b):
    ah, al = _split2(a)
    bh, bl = _split2(b)
    d = functools.partial(jnp.dot, preferred_element_type=f32)
    return d(ah, bh) + (d(ah, bl) + d(al, bh))


def _dot3_nt(a, b):
    ah, al = _split2(a)
    bh, bl = _split2(b)
    d = functools.partial(lax.dot_general, dimension_numbers=(((1,), (1,)), ((), ())),
                          preferred_element_type=f32)
    return d(ah, bh) + (d(ah, bl) + d(al, bh))


def _sel_dot(sel, b):
    bh, bm, bl = _split3(b)
    d = functools.partial(jnp.dot, preferred_element_type=f32)
    return d(sel, bh) + (d(sel, bm) + d(sel, bl))


def _sel_dot_nt(sel, b):
    bh, bm, bl = _split3(b)
    d = functools.partial(lax.dot_general, dimension_numbers=(((1,), (1,)), ((), ())),
                          preferred_element_type=f32)
    return d(sel, bh) + (d(sel, bm) + d(sel, bl))


def _silu(x):
    return x * jax.nn.sigmoid(x)


def _softplus(x):
    return jnp.maximum(x, 0.0) + jnp.log(1.0 + jnp.exp(-jnp.abs(x)))


def _iota(shape, dim):
    return lax.broadcasted_iota(jnp.int32, shape, dim)


def _rms_heads(o, g):
    return o * lax.rsqrt(jnp.mean(o * o, axis=-1, keepdims=True) + EPS) * g


def _mod_kernel(c_ref, w_ref, b_ref, o_ref):
    c = c_ref[...]
    o_ref[...] = _bdot(_silu(c), w_ref[...]) + b_ref[...]


def _mod_call(c16, ada_w, ada_b):
    depth, d, n = ada_w.shape
    tn = _pick(n, (512, 256, 128))
    need = 2 * d * tn * 4 + d * tn * 2 + 4 * 16 * tn * 4 + 2 * 16 * d * 4
    return pl.pallas_call(
        _mod_kernel,
        grid=(depth, n // tn),
        in_specs=[pl.BlockSpec((16, d), lambda l, j: (0, 0)),
                  pl.BlockSpec((None, d, tn), lambda l, j: (l, 0, j)),
                  pl.BlockSpec((None, 1, tn), lambda l, j: (l, 0, j))],
        out_specs=pl.BlockSpec((None, 16, tn), lambda l, j: (l, 0, j)),
        out_shape=jax.ShapeDtypeStruct((depth, 16, n), f32),
        compiler_params=_cparams(("arbitrary", "arbitrary"), need),
        name="adaln_mod",
    )(c16, ada_w, ada_b.reshape(depth, 1, n))


def _premod_kernel(x_ref, g_ref, sh_ref, sc_ref, o_ref):
    g = g_ref[...]
    for s in range(TM // SUB):
        rows = slice(s * SUB, (s + 1) * SUB)
        xs = x_ref[rows, :]
        y = xs * lax.rsqrt(jnp.mean(xs * xs, axis=-1, keepdims=True) + EPS) * g
        o_ref[rows, :] = (y * (1.0 + sc_ref[s:s + 1, :]) + sh_ref[s:s + 1, :]).astype(o_ref.dtype)


def _premod_call(x, g, mod8, v_shift, v_scale):
    m, d = x.shape
    nsub = TM // SUB
    need = 2 * TM * d * 4 + 2 * TM * d * 2 + 6 * nsub * d * 4
    return pl.pallas_call(
        _premod_kernel,
        grid=(m // TM,),
        in_specs=[pl.BlockSpec((TM, d), lambda i: (i, 0)),
                  pl.BlockSpec((1, d), lambda i: (0, 0)),
                  pl.BlockSpec((nsub, d), lambda i: (i, v_shift)),
                  pl.BlockSpec((nsub, d), lambda i: (i, v_scale))],
        out_specs=pl.BlockSpec((TM, d), lambda i: (i, 0)),
        out_shape=jax.ShapeDtypeStruct((m, d), bf16),
        compiler_params=_cparams(("arbitrary",), need),
        name="modulate",
    )(x, g.reshape(1, d), mod8, mod8)


def _pack_kernel(p_ref, s_ref, o_ref, *, n_p):
    i = pl.program_id(0)

    @pl.when(i < n_p)
    def _():
        o_ref[...] = p_ref[...]

    @pl.when(i >= n_p)
    def _():
        o_ref[...] = s_ref[...]


def _pack_call(p_rows, s_rows):
    bs, d = p_rows.shape
    n_p, n_s = bs // TM, s_rows.shape[0] // TM
    return pl.pallas_call(
        functools.partial(_pack_kernel, n_p=n_p),
        grid=(n_p + n_s,),
        in_specs=[pl.BlockSpec((TM, d), lambda i: (jnp.minimum(i, n_p - 1), 0)),
                  pl.BlockSpec((TM, d), lambda i: (jnp.maximum(i - n_p, 0), 0))],
        out_specs=pl.BlockSpec((TM, d), lambda i: (i, 0)),
        out_shape=jax.ShapeDtypeStruct((bs + s_rows.shape[0], d), p_rows.dtype),
        compiler_params=_cparams(("arbitrary",), 6 * TM * d * 4),
        name="pack_tokens",
    )(p_rows, s_rows)


def _rms_kernel(x_ref, g_ref, o_ref):
    x = x_ref[...]
    o_ref[...] = x * lax.rsqrt(jnp.mean(x * x, axis=-1, keepdims=True) + EPS) * g_ref[...]


def _rms_call(x, g, row0, nrows):
    d = x.shape[1]
    blk0 = row0 // TM
    return pl.pallas_call(
        _rms_kernel,
        grid=(nrows // TM,),
        in_specs=[pl.BlockSpec((TM, d), lambda i: (blk0 + i, 0)), pl.BlockSpec((1, d), lambda i: (0, 0))],
        out_specs=pl.BlockSpec((TM, d), lambda i: (i, 0)),
        out_shape=jax.ShapeDtypeStruct((nrows, d), f32),
        compiler_params=_cparams(("arbitrary",), 4 * TM * d * 4),
        name="final_norm",
    )(x, g.reshape(1, d))


MM_VMEM_TARGET = 48 << 20


V7X_BF16_FLOPS = 1.15e15
V7X_HBM_BYTES_PER_S = 3.0e12
STEP_OVERHEAD_S = 0.35e-6
MXU_EFF_BY_TM = {1408: 0.90, 768: 0.85, 512: 0.80, 256: 0.70}


def _ws_plan(m, k, n, n_weights, tile_bytes_per_out_elem, io_bytes_per_out_elem, tns=(1024, 512, 256, 128),
             tms=(768, 512, 256)):
    best = None
    for tn in tns:
        if n % tn:
            continue
        for tm in tms:
            if m % tm:
                continue
            for bufs in (2, 1):
                need = n_weights * k * tn * (4 * bufs + 2) + 2 * tm * k * 2 + tm * tn * tile_bytes_per_out_elem
                if need > MM_VMEM_TARGET:
                    continue
                w_bytes = n_weights * k * n * 4
                hbm = (n // tn) * m * k * 2 + w_bytes + m * n * io_bytes_per_out_elem
                t = max(2.0 * n_weights * m * k * n / V7X_BF16_FLOPS / MXU_EFF_BY_TM[tm], hbm / V7X_HBM_BYTES_PER_S)
                t += (n // tn) * (m // tm) * STEP_OVERHEAD_S
                if bufs == 1:
                    t += w_bytes / V7X_HBM_BYTES_PER_S
                if best is None or t < best[0]:
                    best = (t, tm, tn, bufs, need)
    if best is None:
        raise ValueError(f"no weight-stationary tiling for {(m, k, n)}")
    return best[1:]


def _wspec(shape, index_map, bufs):
    if bufs == 1:
        return pl.BlockSpec(shape, index_map, pipeline_mode=pl.Buffered(1))
    return pl.BlockSpec(shape, index_map)


def _mm_plain_t_kernel(a_ref, w_ref, o_ref, wb_ref):
    @pl.when(pl.program_id(1) == 0)
    def _():
        wb_ref[...] = w_ref[...].astype(bf16)

    o_ref[...] = lax.dot_general(a_ref[...], wb_ref[...], (((1,), (1,)), ((), ())),
                                 preferred_element_type=f32).astype(o_ref.dtype)


def _mm_plain_call(a, wt3, layer, n_cols, name):
    m, k = a.shape
    tm, tn, bufs, need = _ws_plan(m, k, n_cols, 1, 2 * 4 + 4, 4)
    return pl.pallas_call(
        _mm_plain_t_kernel,
        grid=(n_cols // tn, m // tm),
        in_specs=[pl.BlockSpec((tm, k), lambda j, i: (i, 0)),
                  _wspec((None, tn, k), lambda j, i: (layer, j, 0), bufs)],
        out_specs=pl.BlockSpec((tm, tn), lambda j, i: (i, j)),
        out_shape=jax.ShapeDtypeStruct((m, n_cols), f32),
        scratch_shapes=[pltpu.VMEM((tn, k), bf16)],
        compiler_params=_cparams(("arbitrary", "arbitrary"), need),
        name=name,
    )(a, wt3)


def _mm_up_kernel(a_ref, wg_ref, wu_ref, o_ref, wgb_ref, wub_ref):
    @pl.when(pl.program_id(1) == 0)
    def _():
        wgb_ref[...] = wg_ref[...].astype(bf16)
        wub_ref[...] = wu_ref[...].astype(bf16)

    a = a_ref[...]
    g = jnp.dot(a, wgb_ref[...], preferred_element_type=f32)
    u = jnp.dot(a, wub_ref[...], preferred_element_type=f32)
    o_ref[...] = (_silu(g) * u).astype(o_ref.dtype)


def _mm_up_call(a, wg3, wu3, layer):
    m, k = a.shape
    f = wg3.shape[2]
    tm, tn, bufs, need = _ws_plan(m, k, LANES * 8, 2, 2 * 2 + 3 * 4, 2, tns=(512, 256, 128),
                                  tms=(1408, 768, 512, 256))
    tn = min(tn, f)
    n_main = f // tn
    rem = f - n_main * tn
    assert rem % LANES == 0 and (rem == 0 or (n_main * tn) % rem == 0)

    def call(width, col_blk0, ncols, name):
        wspec = _wspec((None, k, width), lambda j, i: (layer, 0, col_blk0 + j), bufs)
        return pl.pallas_call(
            _mm_up_kernel,
            grid=(ncols, m // tm),
            in_specs=[pl.BlockSpec((tm, k), lambda j, i: (i, 0)), wspec, wspec],
            out_specs=pl.BlockSpec((tm, width), lambda j, i: (i, j)),
            out_shape=jax.ShapeDtypeStruct((m, ncols * width), bf16),
            scratch_shapes=[pltpu.VMEM((k, width), bf16), pltpu.VMEM((k, width), bf16)],
            compiler_params=_cparams(("arbitrary", "arbitrary"), need),
            name=name,
        )(a, wg3, wu3)

    outs = [call(tn, 0, n_main, "ffn_up")]
    if rem:
        outs.append(call(rem, (n_main * tn) // rem, 1, "ffn_up_tail"))
    return outs


def _mm_res_kernel(*refs, nseg, scale):
    a_refs, w_refs = refs[:nseg], refs[nseg:2 * nseg]
    r_ref, gate_ref, o_ref = refs[2 * nseg:2 * nseg + 3]
    wb_refs = refs[2 * nseg + 3:]

    @pl.when(pl.program_id(1) == 0)
    def _():
        for w_ref, wb_ref in zip(w_refs, wb_refs):
            wb_ref[...] = w_ref[...].astype(bf16)

    acc = jnp.dot(a_refs[0][...], wb_refs[0][...], preferred_element_type=f32)
    for a_ref, wb_ref in zip(a_refs[1:], wb_refs[1:]):
        acc = acc + jnp.dot(a_ref[...], wb_ref[...], preferred_element_type=f32)
    for s in range(acc.shape[0] // SUB):
        rows = slice(s * SUB, (s + 1) * SUB)
        o_ref[rows, :] = r_ref[rows, :] + (scale * gate_ref[s:s + 1, :]) * acc[rows, :]


def _mm_res_call(acts, w3, layer, res, mod8, v_gate, scale, name):
    m, k0 = acts[0].shape
    n = w3.shape[2]
    nk = 1
    while (k0 // nk) * 1024 * 6 > (36 << 20) and (k0 // nk) % (2 * LANES) == 0:
        nk *= 2
    kc = k0 // nk
    k_rest = sum(a.shape[1] for a in acts[1:])
    tm, tn, bufs, need = _ws_plan(m, kc + k_rest, n, 1, 4 * 4 + 2 * 4 + 4, 8 * nk)
    nsub = tm // SUB
    out = res
    for kb in range(nk):
        segs = [(acts[0], kc, kb, kb)]
        row0 = k0
        if kb == nk - 1:
            for a in acts[1:]:
                assert row0 % a.shape[1] == 0
                segs.append((a, a.shape[1], 0, row0 // a.shape[1]))
                row0 += a.shape[1]
        a_specs = [pl.BlockSpec((tm, w), lambda j, i, cb=cb: (i, cb)) for _, w, cb, _ in segs]
        w_specs = [_wspec((None, w, tn), lambda j, i, rb=rb: (layer, rb, j), bufs) for _, w, _, rb in segs]
        out = pl.pallas_call(
            functools.partial(_mm_res_kernel, nseg=len(segs), scale=scale),
            grid=(n // tn, m // tm),
            in_specs=a_specs + w_specs + [pl.BlockSpec((tm, tn), lambda j, i: (i, j)),
                                          pl.BlockSpec((nsub, tn), lambda j, i: (i, v_gate * (n // tn) + j))],
            out_specs=pl.BlockSpec((tm, tn), lambda j, i: (i, j)),
            out_shape=jax.ShapeDtypeStruct((m, n), f32),
            scratch_shapes=[pltpu.VMEM((w, tn), bf16) for _, w, _, _ in segs],
            compiler_params=_cparams(("arbitrary", "arbitrary"), need),
            name=f"{name}_k{kb}",
        )(*[s[0] for s in segs], *([w3] * len(segs)), out, mod8)
    return out


def _rope_kernel(x_ref, cos_ref, sin_ref, q_ref, k_ref):
    x = x_ref[...]
    w = x.shape[1]
    nh = w // HEAD_DIM
    cosf = jnp.concatenate([cos_ref[...]] * nh, axis=1)
    sinf = jnp.concatenate([sin_ref[...]] * nh, axis=1)
    lane = _iota(x.shape, 1) % HEAD_DIM
    half = ROPE_DIM // 2
    partner = jnp.where(lane < half, pltpu.roll(x, w - half, 1), pltpu.roll(x, half, 1))
    y = x * cosf + partner * sinf
    qw = q_ref.shape[1]
    q_ref[...] = y[:, :qw]
    k_ref[...] = y[:, qw:]


def _rope_call(proj, cos_t, sin_t, qw, kw):
    m = proj.shape[0]
    w = qw + kw
    return pl.pallas_call(
        _rope_kernel,
        grid=(m // TM,),
        in_specs=[pl.BlockSpec((TM, w), lambda i: (i, 0)),
                  pl.BlockSpec((TM, HEAD_DIM), lambda i: (i, 0)),
                  pl.BlockSpec((TM, HEAD_DIM), lambda i: (i, 0))],
        out_specs=[pl.BlockSpec((TM, qw), lambda i: (i, 0)), pl.BlockSpec((TM, kw), lambda i: (i, 0))],
        out_shape=[jax.ShapeDtypeStruct((m, qw), f32), jax.ShapeDtypeStruct((m, kw), f32)],
        compiler_params=_cparams(("arbitrary",), 10 * TM * w * 4),
        name="rope",
    )(proj, cos_t, sin_t)


def _top_blocks(gate, n_valid):
    lane = _iota(gate.shape, 1)
    gate = jnp.where(lane < n_valid, gate, -jnp.inf)
    picks = []
    for kk in range(MOBA_TOPK):
        mx = jnp.max(gate, axis=-1, keepdims=True)
        idx = jnp.min(jnp.where(gate == mx, lane, LANES), axis=-1, keepdims=True)
        picks.append(jnp.where(kk < n_valid, idx, -1))
        gate = jnp.where(lane == idx, -jnp.inf, gate)
    return picks


def _top_blocks_t(gate, n_valid):
    blk = _iota(gate.shape, 0)
    gate = jnp.where(blk < n_valid, gate, -jnp.inf)
    picks = []
    for kk in range(MOBA_TOPK):
        mx = jnp.max(gate, axis=0, keepdims=True)
        idx = jnp.min(jnp.where(gate == mx, blk, gate.shape[0]), axis=0, keepdims=True)
        picks.append(jnp.where(kk < n_valid, idx, -1))
        gate = jnp.where(blk == idx, -jnp.inf, gate)
    return picks


def _moba_prompt_kernel(q_ref, k_ref, v_ref, g_ref, o_ref, kmean_sc):
    i = pl.program_id(2)
    s_len = k_ref.shape[0]
    nb = s_len // MOBA_BLOCK
    scale = HEAD_DIM ** -0.5
    q2 = jnp.concatenate([q_ref[:, :HEAD_DIM], q_ref[:, HEAD_DIM:]], axis=0)
    rows = q2.shape[0]
    @pl.when(i == 0)
    def _():
        blk_row = _iota(kmean_sc.shape, 0)
        kmean = jnp.zeros(kmean_sc.shape, f32)
        for n in range(nb):
            mean_n = jnp.mean(k_ref[n * MOBA_BLOCK:(n + 1) * MOBA_BLOCK, :], axis=0, keepdims=True)
            kmean = jnp.where(blk_row == n, mean_n, kmean)
        kmean_sc[...] = kmean

    gate_t = _dot3_nt(kmean_sc[...], q2)
    picks_t = _top_blocks_t(gate_t, i)
    prow = _iota((SUBLANES, rows), 0)
    pk = jnp.zeros((SUBLANES, rows), f32)
    for kk in range(MOBA_TOPK):
        pk = jnp.where(prow == kk, picks_t[kk].astype(f32), pk)
    picks = [lax.dot_general(pk, (_iota((SUBLANES, LANES), 0) == kk).astype(f32), (((0,), (0,)), ((), ())),
                             preferred_element_type=f32) for kk in range(MOBA_TOPK)]
    qs = (q2 * scale).astype(bf16)

    own = pl.multiple_of(i * MOBA_BLOCK, MOBA_BLOCK)
    s = _bdot_nt(qs, k_ref[pl.ds(own, MOBA_BLOCK), :])
    rq = _iota(s.shape, 0) % MOBA_BLOCK
    ck = _iota(s.shape, 1)
    s = jnp.where(ck <= rq, s, -jnp.inf)
    m0 = jnp.max(s, axis=-1, keepdims=True)
    p = jnp.exp(s - m0)
    l0 = jnp.sum(p, axis=-1, keepdims=True)
    acc0 = _bdot(p, v_ref[pl.ds(own, MOBA_BLOCK), :])

    def body(n, carry):
        m, l, acc = carry
        start = pl.multiple_of(n * MOBA_BLOCK, MOBA_BLOCK)
        sn = _bdot_nt(qs, k_ref[pl.ds(start, MOBA_BLOCK), :])
        nf = n.astype(f32)
        sel = (picks[0] == nf) | (picks[1] == nf) | (picks[2] == nf)
        sn = jnp.concatenate([jnp.where(sel, sn[:, t * LANES:(t + 1) * LANES], -jnp.inf)
                              for t in range(MOBA_BLOCK // LANES)], axis=1)
        m_new = jnp.maximum(m, jnp.max(sn, axis=-1, keepdims=True))
        alpha = jnp.exp(m - m_new)
        pn = jnp.exp(sn - m_new)
        l_new = alpha * l + jnp.sum(pn, axis=-1, keepdims=True)
        acc_new = alpha * acc + _bdot(pn, v_ref[pl.ds(start, MOBA_BLOCK), :])
        return m_new, l_new, acc_new

    m, l, acc = lax.fori_loop(0, i, body, (m0, l0, acc0))
    o = _rms_heads(acc / l, g_ref[...])
    half = rows // 2
    o_ref[:, :HEAD_DIM] = o[:half].astype(o_ref.dtype)
    o_ref[:, HEAD_DIM:] = o[half:].astype(o_ref.dtype)


def _moba_prompt_call(q_rot, k_rot, proj, gnorm, b, s_len, kvh, v_col0):
    qt = s_len // MOBA_BLOCK
    vb = v_col0 // HEAD_DIM
    need = 4 * s_len * HEAD_DIM * 4 + 4 * MOBA_BLOCK * 2 * HEAD_DIM * 4 + 16 * 2 * MOBA_BLOCK * MOBA_BLOCK * 4
    return pl.pallas_call(
        _moba_prompt_kernel,
        grid=(b, kvh, qt),
        in_specs=[pl.BlockSpec((MOBA_BLOCK, 2 * HEAD_DIM), lambda bi, h, i: (bi * qt + i, h)),
                  pl.BlockSpec((s_len, HEAD_DIM), lambda bi, h, i: (bi, h)),
                  pl.BlockSpec((s_len, HEAD_DIM), lambda bi, h, i: (bi, vb + h)),
                  pl.BlockSpec((1, HEAD_DIM), lambda bi, h, i: (0, 0))],
        out_specs=pl.BlockSpec((MOBA_BLOCK, 2 * HEAD_DIM), lambda bi, h, i: (bi * qt + i, h)),
        out_shape=jax.ShapeDtypeStruct((b * s_len, 2 * kvh * HEAD_DIM), bf16),
        scratch_shapes=[pltpu.VMEM((-(-qt // SUBLANES) * SUBLANES, HEAD_DIM), f32)],
        compiler_params=_cparams(("arbitrary", "arbitrary", "arbitrary"), need),
        name="moba_prompt",
    )(q_rot, k_rot, proj, gnorm.reshape(1, HEAD_DIM))


QROWS = 8


def _moba_sample_kernel(pt_ref, q_ref, kn_ref, vn_ref, *refs, nb, bps, kvh, t_len):
    del pt_ref
    npg = bps * (MOBA_BLOCK // PAGE_SIZE)
    k_pages, v_pages = refs[:npg], refs[npg:2 * npg]
    g_ref, o_ref, qs_sc, s_sc, gate_sc, idx_sc, m_sc, l_sc, acc_sc = refs[2 * npg:]
    j = pl.program_id(1)
    nsk = nb // bps
    scale = HEAD_DIM ** -0.5
    r2 = 2 * QROWS

    def block_of(pages, bb, h):
        ppb = MOBA_BLOCK // PAGE_SIZE
        return jnp.concatenate([pages[bb * ppb + t][pl.ds(h, PAGE_SIZE, stride=kvh), :] for t in range(ppb)], axis=0)

    @pl.when(j == 0)
    def _():
        for h in range(kvh):
            q2 = jnp.concatenate([q_ref[0:QROWS, (2 * h) * HEAD_DIM:(2 * h + 1) * HEAD_DIM],
                                  q_ref[0:QROWS, (2 * h + 1) * HEAD_DIM:(2 * h + 2) * HEAD_DIM]], axis=0)
            qs_sc[h] = q2
        gate_sc[...] = jnp.zeros_like(gate_sc)

    @pl.when(j < nsk)
    def _():
        lane = _iota((r2, LANES), 1)
        pairs = [(bb, h) for bb in range(bps) for h in range(kvh)]
        q2s = [qs_sc[h] for h in range(kvh)]
        khs = [block_of(k_pages, bb, h) for bb, h in pairs]
        scores = [_bdot_nt(q2s[h] * scale, kh) for (bb, h), kh in zip(pairs, khs)]
        kmeans = [jnp.sum(kh, axis=0, keepdims=True) * (1.0 / MOBA_BLOCK) for kh in khs]
        cols = [jnp.sum(q2s[h] * km, axis=-1, keepdims=True) for (bb, h), km in zip(pairs, kmeans)]
        for (bb, h), sc in zip(pairs, scores):
            s_sc[j * bps + bb, h] = sc
        for h in range(kvh):
            gate = gate_sc[h]
            for (bb, hh), col in zip(pairs, cols):
                if hh == h:
                    gate = jnp.where(lane == j * bps + bb, col, gate)
            gate_sc[h] = gate

    @pl.when(j == nsk - 1)
    def _():
        for h in range(kvh):
            picks = _top_blocks(gate_sc[h], nb)
            for kk in range(MOBA_TOPK):
                idx_sc[h * MOBA_TOPK + kk] = jnp.broadcast_to(picks[kk], (r2, LANES))
            kn = jnp.concatenate([kn_ref[0:QROWS, h * HEAD_DIM:(h + 1) * HEAD_DIM],
                                  jnp.zeros((LANES - QROWS, HEAD_DIM), f32)], axis=0)
            vn = jnp.concatenate([vn_ref[0:QROWS, h * HEAD_DIM:(h + 1) * HEAD_DIM],
                                  jnp.zeros((LANES - QROWS, HEAD_DIM), f32)], axis=0)
            s = _bdot_nt(qs_sc[h] * scale, kn)
            tq = _iota(s.shape, 0) % QROWS
            ck = _iota(s.shape, 1)
            s = jnp.where((ck <= tq) & (ck < t_len), s, -jnp.inf)
            m0 = jnp.max(s, axis=-1, keepdims=True)
            p = jnp.exp(s - m0)
            m_sc[h] = jnp.broadcast_to(m0, (r2, LANES))
            l_sc[h] = jnp.broadcast_to(jnp.sum(p, axis=-1, keepdims=True), (r2, LANES))
            acc_sc[h] = _bdot(p, vn)

    @pl.when(j >= nsk)
    def _():
        heads_ = range(kvh)
        picks = [[idx_sc[h * MOBA_TOPK + kk][:, 0:1] for kk in range(MOBA_TOPK)] for h in heads_]
        sns = []
        for h in heads_:
            row = []
            for bb in range(bps):
                n = (j - nsk) * bps + bb
                sel = (picks[h][0] == n) | (picks[h][1] == n) | (picks[h][2] == n)
                row.append(jnp.where(sel, s_sc[n, h], -jnp.inf))
            sns.append(row)
        ms = [m_sc[h][:, 0:1] for h in heads_]
        m_news = []
        for h in heads_:
            mx = ms[h]
            for sn in sns[h]:
                mx = jnp.maximum(mx, jnp.max(sn, axis=-1, keepdims=True))
            m_news.append(mx)
        alphas = [jnp.exp(ms[h] - m_news[h]) for h in heads_]
        pns = [[jnp.exp(sn - m_news[h]) for sn in sns[h]] for h in heads_]
        pvs = [[_bdot(pns[h][bb], block_of(v_pages, bb, h)) for bb in range(bps)] for h in heads_]
        for h in heads_:
            l = alphas[h] * l_sc[h][:, 0:1]
            acc = alphas[h] * acc_sc[h]
            for bb in range(bps):
                l = l + jnp.sum(pns[h][bb], axis=-1, keepdims=True)
                acc = acc + pvs[h][bb]
            m_sc[h] = jnp.broadcast_to(m_news[h], (r2, LANES))
            l_sc[h] = jnp.broadcast_to(l, (r2, LANES))
            acc_sc[h] = acc

    @pl.when(j == 2 * nsk - 1)
    def _():
        for h in range(kvh):
            o = _rms_heads(acc_sc[h] / l_sc[h][:, 0:1], g_ref[...])
            o_ref[:, (2 * h) * HEAD_DIM:(2 * h + 1) * HEAD_DIM] = o[:QROWS]
            o_ref[:, (2 * h + 1) * HEAD_DIM:(2 * h + 2) * HEAD_DIM] = o[QROWS:]


def _moba_sample_call(page_table, q_rot, k_rot, proj, cache_k, cache_v, gnorm, layer,
                      db, row0, kvh, v_col0, t_len):
    n_pages = page_table.shape[1]
    nb = n_pages * PAGE_SIZE // MOBA_BLOCK
    ppb = MOBA_BLOCK // PAGE_SIZE
    bps = _pick(nb, (8, 4, 2, 1))
    nsk = nb // bps
    npg = bps * ppb
    qw = 2 * kvh * HEAD_DIM
    kw = kvh * HEAD_DIM
    blk0 = row0 // SUB

    def kmap(t):
        return lambda b, j, pt: (layer, pt[b, npg * jnp.minimum(j, nsk - 1) + t], 0, 0)

    def vmap_(t):
        return lambda b, j, pt: (layer, pt[b, npg * jnp.maximum(j - nsk, 0) + t], 0, 0)

    depth, n_pool = cache_k.shape[:2]
    cache_k = cache_k.reshape(depth, n_pool, PAGE_SIZE * kvh, HEAD_DIM)
    cache_v = cache_v.reshape(depth, n_pool, PAGE_SIZE * kvh, HEAD_DIM)
    page_spec = lambda fn: pl.BlockSpec((None, None, PAGE_SIZE * kvh, HEAD_DIM), fn)
    r2 = 2 * QROWS
    need = 4 * npg * PAGE_SIZE * kvh * HEAD_DIM * 4 + nb * kvh * r2 * MOBA_BLOCK * 4 + (8 << 20)
    grid_spec = pltpu.PrefetchScalarGridSpec(
        num_scalar_prefetch=1,
        grid=(db, 2 * nsk),
        in_specs=[pl.BlockSpec((SUB, qw), lambda b, j, pt: (blk0 + b, 0)),
                  pl.BlockSpec((SUB, kw), lambda b, j, pt: (blk0 + b, 0)),
                  pl.BlockSpec((SUB, kw), lambda b, j, pt: (blk0 + b, v_col0 // kw))]
                 + [page_spec(kmap(t)) for t in range(npg)] + [page_spec(vmap_(t)) for t in range(npg)]
                 + [pl.BlockSpec((1, HEAD_DIM), lambda b, j, pt: (0, 0))],
        out_specs=pl.BlockSpec((QROWS, qw), lambda b, j, pt: (b, 0)),
        scratch_shapes=[pltpu.VMEM((kvh, r2, HEAD_DIM), f32),
                        pltpu.VMEM((nb, kvh, r2, MOBA_BLOCK), f32),
                        pltpu.VMEM((kvh, r2, LANES), f32),
                        pltpu.VMEM((kvh * MOBA_TOPK, r2, LANES), jnp.int32),
                        pltpu.VMEM((kvh, r2, LANES), f32),
                        pltpu.VMEM((kvh, r2, LANES), f32),
                        pltpu.VMEM((kvh, r2, HEAD_DIM), f32)])
    return pl.pallas_call(
        functools.partial(_moba_sample_kernel, nb=nb, bps=bps, kvh=kvh, t_len=t_len),
        grid_spec=grid_spec,
        out_shape=jax.ShapeDtypeStruct((db * QROWS, qw), f32),
        compiler_params=_cparams(("arbitrary", "arbitrary"), need),
        name="moba_sample",
    )(page_table, q_rot, k_rot, proj, *([cache_k] * npg), *([cache_v] * npg), gnorm.reshape(1, HEAD_DIM))


def _inv_unit_lower(lmats, nblk):
    c = lmats[0].shape[0]
    ii = _iota((c, c), 0)
    jj = _iota((c, c), 1)
    eye = (ii == jj).astype(f32)
    same = (ii // INV_BLK) == (jj // INV_BLK)
    dmats = [jnp.where(same, l, 0.0) for l in lmats]
    ps = [-d for d in dmats]
    xs = [eye + p for p in ps]
    k = 2
    while k < INV_BLK:
        ps = [_bdot(p, p) for p in ps]
        xs = [x + _bdot(x, p) for x, p in zip(xs, ps)]
        k *= 2
    if nblk > 1:
        mms = [_bdot(x, l - d) for x, l, d in zip(xs, lmats, dmats)]
        ys = [eye - mm for mm in mms]
        pms = mms
        k = 2
        while k < nblk:
            pms = [_bdot(pm, pm) for pm in pms]
            ys = [y + _bdot(y, pm) for y, pm in zip(ys, pms)]
            k *= 2
        xs = [_bdot(y, x) for y, x in zip(ys, xs)]
    for _ in range(2):
        rs = [eye - _dot3(eye + l, x) for l, x in zip(lmats, xs)]
        xs = [x + _bdot(x, r) for x, r in zip(xs, rs)]
    return xs


def _gdn_kernel(main_ref, ab_ref, cw_ref, prev_ref, alog_ref, dtb_ref, s0_ref, g_ref, o_ref, s_ref, carry_sc,
                *, heads, group, col0, t_valid):
    c_idx = pl.program_id(1)
    c = main_ref.shape[0]
    gw = heads * HEAD_DIM
    ch = 3 * gw

    @pl.when(c_idx == 0)
    def _():
        s_ref[...] = s0_ref[...]
        carry_sc[0:SUBLANES, :] = prev_ref[...]

    x = main_ref[:, col0:col0 + ch]
    carry_sc[SUBLANES:, :] = x
    y = cw_ref[GDN_CONV - 1:GDN_CONV, :] * x
    for tap in range(1, GDN_CONV):
        y = y + cw_ref[GDN_CONV - 1 - tap:GDN_CONV - tap, :] * carry_sc[SUBLANES - tap:SUBLANES - tap + c, :]
    carry_sc[0:SUBLANES, :] = x[c - SUBLANES:, :]
    y = _silu(y)

    ab = ab_ref[...]
    lane = _iota(ab.shape, 1)
    row_ok = (c_idx * c + _iota((c, 1), 0)) < t_valid
    g_all = jnp.where(row_ok & (lane < heads), -jnp.exp(alog_ref[...]) * _softplus(ab + dtb_ref[...]), 0.0)
    beta_all = jnp.where(row_ok, jax.nn.sigmoid(ab), 0.0)
    tri = (_iota((c, c), 0) >= _iota((c, c), 1)).astype(bf16)
    gam_all = _sel_dot(tri, g_all)
    eye_l = (_iota((LANES, LANES), 0) == _iota((LANES, LANES), 1)).astype(bf16)
    gam_t = _sel_dot_nt(eye_l, gam_all)
    gnorm = g_ref[...]

    r = group * c
    ii = _iota((r, r), 0)
    jj = _iota((r, r), 1)
    same = (ii // c) == (jj // c)
    low = same & (ii >= jj)
    strict = same & (ii > jj)
    groups = [range(g * group, (g + 1) * group) for g in range(heads // group)]
    stack = lambda hs, f: jnp.concatenate([f(h) for h in hs], axis=0)
    qs, ks, vs, betas, gcols, decays, egams, kds = [], [], [], [], [], [], [], []
    for hs in groups:
        xq = stack(hs, lambda h: y[:, h * HEAD_DIM:(h + 1) * HEAD_DIM])
        xk = stack(hs, lambda h: y[:, gw + h * HEAD_DIM:gw + (h + 1) * HEAD_DIM])
        vs.append(stack(hs, lambda h: y[:, 2 * gw + h * HEAD_DIM:2 * gw + (h + 1) * HEAD_DIM]))
        qs.append(xq * lax.rsqrt(jnp.sum(xq * xq, axis=-1, keepdims=True) + EPS) * (HEAD_DIM ** -0.5))
        ks.append(xk * lax.rsqrt(jnp.sum(xk * xk, axis=-1, keepdims=True) + EPS))
        betas.append(stack(hs, lambda h: beta_all[:, heads + h:heads + h + 1]))
        gcol = stack(hs, lambda h: gam_all[:, h:h + 1])
        glast = stack(hs, lambda h: jnp.broadcast_to(gam_all[c - 1:c, h:h + 1], (c, 1)))
        grow = jnp.concatenate([gam_t[h:h + 1, :] for h in hs], axis=1)
        gcols.append(gcol)
        egams.append(jnp.exp(gcol))
        decays.append(jnp.exp(jnp.where(low, gcol - grow, -jnp.inf)))
        kds.append(ks[-1] * jnp.exp(glast - gcol))
    kks = [_bdot_nt(k, k) for k in ks]
    lmats = [jnp.where(strict, beta * kk * decay, 0.0) for beta, kk, decay in zip(betas, kks, decays)]
    tmats = _inv_unit_lower(lmats, c // INV_BLK)
    us = [_bdot(t, beta * v) for t, beta, v in zip(tmats, betas, vs)]
    ws = [_bdot(t, (beta * egam) * k) for t, beta, egam, k in zip(tmats, betas, egams, ks)]
    qks = [_bdot_nt(q, k) * decay for q, k, decay in zip(qs, ks, decays)]
    qes = [q * egam for q, egam in zip(qs, egams)]
    rows_of = lambda a: slice(a * c, (a + 1) * c)
    states = [[s_ref[h] for h in hs] for hs in groups]
    v_news = [jnp.concatenate([u[rows_of(a), :] - _bdot(w[rows_of(a), :], st[a]) for a in range(group)], axis=0)
              for u, w, st in zip(us, ws, states)]
    o_inters = [jnp.concatenate([_bdot(qe[rows_of(a), :], st[a]) for a in range(group)], axis=0)
                for qe, st in zip(qes, states)]
    os_ = [oi + _bdot(qk, vn) for oi, qk, vn in zip(o_inters, qks, v_news)]
    for hs, st, kd, vn, o in zip(groups, states, kds, v_news, os_):
        for a, h in enumerate(hs):
            s_ref[h] = jnp.exp(gam_all[c - 1:c, h:h + 1]) * st[a] + _bdot_tn(kd[rows_of(a), :], vn[rows_of(a), :])
            z = main_ref[:, col0 + ch + h * HEAD_DIM:col0 + ch + (h + 1) * HEAD_DIM]
            o_ref[:, h * HEAD_DIM:(h + 1) * HEAD_DIM] = (_rms_heads(o[rows_of(a), :], gnorm) * _silu(z)).astype(o_ref.dtype)


def _gdn_call(proj, tail, cw, prev8, alog, dtb, s0, gnorm, layer, *, nseq, rows_per_seq, row0, chunk, heads,
              col0, ab_col0, t_valid, name):
    mainw = proj.shape[1]
    gw = heads * HEAD_DIM
    ch = 3 * gw
    nch = rows_per_seq // chunk
    blk0 = row0 // chunk
    need = 2 * chunk * mainw * 4 + 4 * heads * HEAD_DIM * HEAD_DIM * 4 + 40 * chunk * ch * 4 + (4 << 20)
    return pl.pallas_call(
        functools.partial(_gdn_kernel, heads=heads, group=_pick(heads, (2, 1)), col0=col0, t_valid=t_valid),
        grid=(nseq, nch),
        in_specs=[pl.BlockSpec((chunk, mainw), lambda b, c: (blk0 + b * nch + c, 0)),
                  pl.BlockSpec((chunk, LANES), lambda b, c: (blk0 + b * nch + c, ab_col0 // LANES)),
                  pl.BlockSpec((None, GDN_CONV, ch), lambda b, c: (layer, 0, 0)),
                  pl.BlockSpec((None, SUBLANES, ch), lambda b, c: (b, 0, 0)),
                  pl.BlockSpec((1, LANES), lambda b, c: (0, 0)),
                  pl.BlockSpec((1, LANES), lambda b, c: (0, 0)),
                  pl.BlockSpec((None, heads, HEAD_DIM, HEAD_DIM), lambda b, c: (b, 0, 0, 0)),
                  pl.BlockSpec((1, HEAD_DIM), lambda b, c: (0, 0))],
        out_specs=[pl.BlockSpec((chunk, gw), lambda b, c: (b * nch + c, 0)),
                   pl.BlockSpec((None, heads, HEAD_DIM, HEAD_DIM), lambda b, c: (b, 0, 0, 0))],
        out_shape=[jax.ShapeDtypeStruct((nseq * rows_per_seq, gw), bf16),
                   jax.ShapeDtypeStruct((nseq, heads, HEAD_DIM, HEAD_DIM), f32)],
        scratch_shapes=[pltpu.VMEM((SUBLANES + chunk, ch), f32)],
        compiler_params=_cparams(("arbitrary", "arbitrary"), need),
        name=name,
    )(proj, tail, cw, prev8, alog, dtb, s0, gnorm.reshape(1, HEAD_DIM))


def _gla_kernel(tail_ref, wg_ref, bg_ref, s0_ref, g_ref, o_ref, s_ref, *, heads, ab_col0, t_valid):
    c_idx = pl.program_id(1)
    c = tail_ref.shape[0]
    kw = heads * GLA_DK
    vw = heads * HEAD_DIM

    @pl.when(c_idx == 0)
    def _():
        s_ref[...] = s0_ref[...]

    row_ok = (c_idx * c + _iota((c, 1), 0)) < t_valid
    fblk = tail_ref[:, ab_col0:ab_col0 + LANES]
    pre = _dot3(fblk, wg_ref[...]) + bg_ref[...]
    log_a = jnp.where(row_ok, -_softplus(-pre) * (1.0 / GLA_TAU), 0.0)
    tri = (_iota((c, c), 0) >= _iota((c, c), 1)).astype(bf16)
    bc_all = _sel_dot(tri, log_a)
    gnorm = g_ref[...]
    lane = _iota((c, LANES), 1)
    first = lane < GLA_DK
    ii = _iota((c, LANES), 0)
    jj = lane % GLA_DK
    rowid = _iota((c, 1), 0)
    eye_l = _iota((LANES, LANES), 0) == _iota((LANES, LANES), 1)
    seg = (((_iota((2 * LANES, LANES), 0) % LANES) < GLA_DK) == (_iota((2 * LANES, LANES), 1) < GLA_DK)).astype(bf16)
    zpad = jnp.zeros((GLA_DK - c, LANES), f32) if c < GLA_DK else None

    def stack_pair(a):
        a0 = jnp.where(first, a, 0.0)
        a1 = jnp.where(first, 0.0, a)
        parts = [a0, a1] if zpad is None else [a0, zpad, a1, zpad]
        return jnp.concatenate(parts, axis=0)

    for p in range(heads // 2):
        ls = slice(p * LANES, (p + 1) * LANES)
        q = tail_ref[:, ls] * (GLA_DK ** -0.5)
        k = jnp.where(row_ok, tail_ref[:, kw + p * LANES:kw + (p + 1) * LANES], 0.0)
        v0 = tail_ref[:, 2 * kw + (2 * p) * HEAD_DIM:2 * kw + (2 * p + 1) * HEAD_DIM]
        v1 = tail_ref[:, 2 * kw + (2 * p + 1) * HEAD_DIM:2 * kw + (2 * p + 2) * HEAD_DIM]
        r0_ = tail_ref[:, 2 * kw + vw + (2 * p) * HEAD_DIM:2 * kw + vw + (2 * p + 1) * HEAD_DIM]
        r1_ = tail_ref[:, 2 * kw + vw + (2 * p + 1) * HEAD_DIM:2 * kw + vw + (2 * p + 2) * HEAD_DIM]
        vparts = [v0, v1] if zpad is None else [v0, zpad, v1, zpad]
        v2 = jnp.concatenate(vparts, axis=0).astype(bf16)
        bc = bc_all[:, ls]
        s = s_ref[p]
        qe = q * jnp.exp(bc)
        o0 = _bdot(jnp.where(first, qe, 0.0), s)
        o1 = _bdot(jnp.where(first, 0.0, qe), s)

        pieces = [jnp.zeros((GLA_SUB, LANES), f32)]
        for sb in range(1, c // GLA_SUB):
            r0 = sb * GLA_SUB
            rs = slice(r0, r0 + GLA_SUB)
            b0 = bc[r0:r0 + 1, :]
            q_i = q[rs, :] * jnp.exp(bc[rs, :] - b0)
            k_j = k * jnp.exp(jnp.where(rowid < r0, b0 - bc, -jnp.inf))
            pieces.append(_bdot_nt(q_i, stack_pair(k_j)))
        att_off = jnp.concatenate(pieces, axis=0)

        att = jnp.zeros((c, LANES), f32)
        for dlt in range(GLA_SUB):
            k_r = pltpu.roll(k, dlt, 0) if dlt else k
            bc_r = pltpu.roll(bc, dlt, 0) if dlt else bc
            ok = (rowid % GLA_SUB) >= dlt
            pr = q * k_r * jnp.exp(jnp.where(ok, bc - bc_r, -jnp.inf))
            hi, lo = _split2(pr)
            sums = jnp.dot(jnp.concatenate([hi, lo], axis=1), seg, preferred_element_type=f32)
            att = jnp.where(jj == ii - dlt, sums, att)
        att = att + att_off

        o0 = o0 + _bdot(jnp.where(first, att, 0.0), v2)
        o1 = o1 + _bdot(jnp.where(first, 0.0, att), v2)

        bl = bc[c - 1:c, :]
        ebl_col = jnp.sum(jnp.where(eye_l, jnp.exp(bl), 0.0), axis=-1, keepdims=True)
        s_ref[p] = ebl_col * s + _bdot_tn(stack_pair(k * jnp.exp(bl - bc)), v2)
        o_ref[:, (2 * p) * HEAD_DIM:(2 * p + 1) * HEAD_DIM] = (_rms_heads(o0, gnorm) * _silu(r0_)).astype(o_ref.dtype)
        o_ref[:, (2 * p + 1) * HEAD_DIM:(2 * p + 2) * HEAD_DIM] = (_rms_heads(o1, gnorm) * _silu(r1_)).astype(o_ref.dtype)


def _gla_call(tail, wg_pad, bg, s0, gnorm, *, nseq, rows_per_seq, row0, chunk, heads, ab_col0, t_valid, name):
    tailw = tail.shape[1]
    kw = heads * GLA_DK
    vw = heads * HEAD_DIM
    nch = rows_per_seq // chunk
    blk0 = row0 // chunk
    assert heads % 2 == 0 and chunk <= GLA_DK and chunk % GLA_SUB == 0
    s0 = s0.reshape(nseq, heads // 2, 2 * GLA_DK, HEAD_DIM)
    need = 2 * chunk * tailw * 4 + 4 * heads * GLA_DK * HEAD_DIM * 4 + 2 * LANES * kw * 4 + 40 * chunk * tailw * 4
    o, s_fin = pl.pallas_call(
        functools.partial(_gla_kernel, heads=heads, ab_col0=ab_col0, t_valid=t_valid),
        grid=(nseq, nch),
        in_specs=[pl.BlockSpec((chunk, tailw), lambda b, c: (blk0 + b * nch + c, 0)),
                  pl.BlockSpec((LANES, kw), lambda b, c: (0, 0)),
                  pl.BlockSpec((1, kw), lambda b, c: (0, 0)),
                  pl.BlockSpec((None, heads // 2, 2 * GLA_DK, HEAD_DIM), lambda b, c: (b, 0, 0, 0)),
                  pl.BlockSpec((1, HEAD_DIM), lambda b, c: (0, 0))],
        out_specs=[pl.BlockSpec((chunk, vw), lambda b, c: (b * nch + c, 0)),
                   pl.BlockSpec((None, heads // 2, 2 * GLA_DK, HEAD_DIM), lambda b, c: (b, 0, 0, 0))],
        out_shape=[jax.ShapeDtypeStruct((nseq * rows_per_seq, vw), bf16),
                   jax.ShapeDtypeStruct((nseq, heads // 2, 2 * GLA_DK, HEAD_DIM), f32)],
        compiler_params=_cparams(("arbitrary", "arbitrary"), need),
        name=name,
    )(tail, wg_pad, bg.reshape(1, kw), s0, gnorm.reshape(1, HEAD_DIM))
    return o, s_fin.reshape(nseq, heads, GLA_DK, HEAD_DIM)


def kernel(x_prompt, x_sample, c_prompt, c_sample, cache_k, cache_v, page_table, state_gdn, state_gdn_conv, state_gla, ada_w, ada_b, norm_ffn_a, ffn_a_wg, ffn_a_wu, ffn_a_wd, norm_mix, w_in, moba_norm, gdn_conv_w, gdn_a_log, gdn_dt_bias, gdn_norm, gla_w_gate, gla_b_gate, gla_norm, w_out, norm_ffn_b, ffn_b_wg, ffn_b_wu, ffn_b_wd, final_norm):
    b, s_len, d = x_prompt.shape
    db, t_dec, _ = x_sample.shape
    depth = ada_w.shape[0]
    n_pages = page_table.shape[1]
    past_len = n_pages * PAGE_SIZE

    n_heads = d // HEAD_DIM
    moba_h = n_heads // 4
    kvh = moba_h // 2
    gdn_h = (3 * n_heads) // 8
    gla_h = n_heads - moba_h - gdn_h
    qw, kvw = moba_h * HEAD_DIM, kvh * HEAD_DIM
    gdn_w, gla_kw, gla_w = gdn_h * HEAD_DIM, gla_h * GLA_DK, gla_h * HEAD_DIM
    main_w = qw + 2 * kvw + 4 * gdn_w
    small0 = main_w
    gla0 = main_w + 2 * gdn_h
    f0 = gla0 + 2 * gla_kw + 2 * gla_w
    tail_main = 2 * gla_kw + 2 * gla_w
    assert w_in.shape[2] == f0 + GLA_GATE_RANK
    assert s_len % TM == 0 and s_len % CHUNK == 0 and s_len >= MOBA_TOPK * MOBA_BLOCK
    assert (db * SUB) % TM == 0 and b + db <= 16
    assert GDN_CONV - 1 <= t_dec <= QROWS and past_len % MOBA_BLOCK == 0
    assert 2 * gdn_h + GLA_GATE_RANK <= LANES and tail_main % LANES == 0

    bs = b * s_len
    m_tot = bs + db * SUB
    n_tiles_p = bs // TM

    def pack_rows(p_rows, s_rows):
        s_pad = jnp.pad(s_rows, ((0, 0), (0, SUB - s_rows.shape[1]), (0, 0)))
        return jnp.concatenate([p_rows, s_pad.reshape(db * SUB, s_rows.shape[2])], axis=0)

    def sample_rows(a):
        return a[bs:].reshape(db, SUB, a.shape[1])[:, :t_dec]

    x = _pack_call(x_prompt.reshape(bs, d),
                   jnp.pad(x_sample, ((0, 0), (0, SUB - t_dec), (0, 0))).reshape(db * SUB, d))

    c16 = jnp.concatenate([c_prompt, c_sample, jnp.zeros((16 - b - db, d), f32)], axis=0)
    mod = _mod_call(c16, ada_w, ada_b)

    half = ROPE_DIM // 2
    inv_freq = ROPE_THETA ** (-jnp.arange(half, dtype=f32) / half)
    pos_s = past_len + jnp.minimum(jnp.arange(SUB), t_dec - 1)
    pos = jnp.concatenate([jnp.tile(jnp.arange(s_len), b), jnp.tile(pos_s, db)]).astype(f32)
    ang = pos[:, None] * inv_freq[None, :]
    ones = jnp.ones((m_tot, HEAD_DIM - ROPE_DIM), f32)
    cos_t = jnp.concatenate([jnp.cos(ang), jnp.cos(ang), ones], axis=1)
    sin_t = jnp.concatenate([-jnp.sin(ang), jnp.sin(ang), 0.0 * ones], axis=1)

    tail_w = -(-(tail_main + LANES) // 512) * 512
    w_in_t = jnp.swapaxes(w_in, 1, 2)
    w_tail_t = jnp.concatenate([w_in_t[:, gla0:f0], w_in_t[:, small0:gla0], w_in_t[:, f0:],
                                jnp.zeros((depth, tail_w - tail_main - 2 * gdn_h - GLA_GATE_RANK, d), f32)], axis=1)
    zeros_prev = jnp.zeros((b, SUBLANES, 3 * gdn_w), f32)

    outs = {k: [] for k in ("kp", "vp", "ks", "vs", "gp", "gs", "cp", "cs", "lp", "ls")}
    for l in range(depth):
        mod8 = jnp.concatenate([jnp.repeat(mod[l, :b], (s_len // TM) * (TM // SUB), axis=0), mod[l, b:b + db]], axis=0)

        h = _premod_call(x, norm_ffn_a[l], mod8, 0, 1)
        a = _mm_up_call(h, ffn_a_wg, ffn_a_wu, l)
        x = _mm_res_call(a, ffn_a_wd, l, x, mod8, 2, 0.5, "ffn_a_down")

        h = _premod_call(x, norm_mix[l], mod8, 3, 4)
        proj = _mm_plain_call(h, w_in_t, l, main_w, "w_in_main")
        tail = _mm_plain_call(h, w_tail_t, l, tail_w, "w_in_tail")

        q_rot, k_rot = _rope_call(proj, cos_t, sin_t, qw, kvw)
        v_col0 = qw + kvw
        om_p = _moba_prompt_call(q_rot, k_rot, proj, moba_norm[l], b, s_len, kvh, v_col0)
        om_s = _moba_sample_call(page_table, q_rot, k_rot, proj, cache_k, cache_v, moba_norm[l], l,
                                 db, bs, kvh, v_col0, t_dec)
        om_s = om_s.reshape(db, QROWS, qw)[:, :t_dec].astype(bf16)

        col0 = qw + 2 * kvw
        alog = jnp.zeros((1, LANES), f32).at[0, :gdn_h].set(gdn_a_log[l])
        dtb = jnp.zeros((1, LANES), f32).at[0, :gdn_h].set(gdn_dt_bias[l])
        od_p, gp = _gdn_call(proj, tail, gdn_conv_w, zeros_prev, alog, dtb,
                             jnp.zeros((b, gdn_h, HEAD_DIM, HEAD_DIM), f32), gdn_norm[l], l,
                             nseq=b, rows_per_seq=s_len, row0=0, chunk=CHUNK, heads=gdn_h, col0=col0,
                             ab_col0=tail_main, t_valid=s_len, name="gdn_prompt")
        prev_s = jnp.pad(state_gdn_conv[l], ((0, 0), (SUBLANES - (GDN_CONV - 1), 0), (0, 0)))
        od_s, gs = _gdn_call(proj, tail, gdn_conv_w, prev_s, alog, dtb, state_gdn[l], gdn_norm[l], l,
                             nseq=db, rows_per_seq=SUB, row0=bs, chunk=SUB, heads=gdn_h, col0=col0,
                             ab_col0=tail_main, t_valid=t_dec, name="gdn_sample")

        wg_pad = jnp.zeros((LANES, gla_kw), f32).at[2 * gdn_h:2 * gdn_h + GLA_GATE_RANK].set(gla_w_gate[l])
        ol_p, lp = _gla_call(tail, wg_pad, gla_b_gate[l], jnp.zeros((b, gla_h, GLA_DK, HEAD_DIM), f32), gla_norm[l],
                             nseq=b, rows_per_seq=s_len, row0=0, chunk=CHUNK, heads=gla_h, ab_col0=tail_main,
                             t_valid=s_len, name="gla_prompt")
        ol_s, ls = _gla_call(tail, wg_pad, gla_b_gate[l], state_gla[l], gla_norm[l],
                             nseq=db, rows_per_seq=SUB, row0=bs, chunk=SUB, heads=gla_h, ab_col0=tail_main,
                             t_valid=t_dec, name="gla_sample")

        o_p = jnp.concatenate([om_p, od_p, ol_p], axis=1)
        o_s = jnp.concatenate([om_s, od_s.reshape(db, SUB, gdn_w)[:, :t_dec], ol_s.reshape(db, SUB, gla_w)[:, :t_dec]], axis=2)
        o_mix = pack_rows(o_p, o_s)
        x = _mm_res_call([o_mix], w_out, l, x, mod8, 5, 1.0, "w_out")

        h = _premod_call(x, norm_ffn_b[l], mod8, 6, 7)
        a = _mm_up_call(h, ffn_b_wg, ffn_b_wu, l)
        x = _mm_res_call(a, ffn_b_wd, l, x, mod8, 8, 0.5, "ffn_b_down")

        conv_cols = slice(col0, col0 + 3 * gdn_w)
        proj_s = proj[bs:].reshape(db, SUB, main_w)
        outs["kp"].append(k_rot[:bs].reshape(b, s_len, kvh, HEAD_DIM))
        outs["vp"].append(proj[:bs, v_col0:v_col0 + kvw].reshape(b, s_len, kvh, HEAD_DIM))
        outs["ks"].append(sample_rows(k_rot).reshape(db, t_dec, kvh, HEAD_DIM))
        outs["vs"].append(proj_s[:, :t_dec, v_col0:v_col0 + kvw].reshape(db, t_dec, kvh, HEAD_DIM))
        outs["gp"].append(gp)
        outs["gs"].append(gs)
        outs["cp"].append(jnp.stack([proj[(i + 1) * s_len - (GDN_CONV - 1):(i + 1) * s_len, conv_cols] for i in range(b)]))
        outs["cs"].append(proj_s[:, t_dec - (GDN_CONV - 1):t_dec, conv_cols])
        outs["lp"].append(lp)
        outs["ls"].append(ls)

    y_p = _rms_call(x, final_norm, 0, bs)
    y_s = _rms_call(x, final_norm, bs, db * SUB)
    st = {k: jnp.stack(v) for k, v in outs.items()}
    return (y_p.reshape(b, s_len, d), y_s.reshape(db, SUB, d)[:, :t_dec],
            st["kp"], st["vp"], st["ks"], st["vs"], st["gp"], st["gs"], st["cp"], st["cs"], st["lp"], st["ls"])
```

```python
import functools
import math

import jax
import jax.numpy as jnp
from jax import lax
from jax.experimental import pallas as pl
from jax.experimental.pallas import tpu as pltpu

f32 = jnp.float32
bf16 = jnp.bfloat16

HEAD_DIM = 128
MOBA_BLOCK = 256
MOBA_TOPK = 3
ROPE_THETA = 500000.0
ROPE_DIM = HEAD_DIM // 4
GDN_CONV = 4
GLA_DK = HEAD_DIM // 2
GLA_GATE_RANK = 16
GLA_TAU = 16.0
GLA_SUB = 16
INV_BLK = 16
CHUNK = 64
PAGE_SIZE = 128
N_MOD = 9
EPS = 1e-6

LANES = 128
SUBLANES = 8
TM = 256
SUB = 32
V7X_VMEM_BYTES = 64 * 1024 * 1024
VMEM_BUDGET = 58 * 1024 * 1024


def _cparams(sem, need_bytes):
    limit = int(min(max(need_bytes * 1.25 + (4 << 20), 16 << 20), VMEM_BUDGET))
    return pltpu.CompilerParams(dimension_semantics=sem, vmem_limit_bytes=limit)


def _pick(n, prefs):
    for p in prefs:
        if n % p == 0:
            return p
    raise ValueError(f"no tile in {prefs} divides {n}")


def _bdot(a, b):
    return jnp.dot(a.astype(bf16), b.astype(bf16), preferred_element_type=f32)


def _bdot_nt(a, b):
    return lax.dot_general(a.astype(bf16), b.astype(bf16), (((1,), (1,)), ((), ())),
                           preferred_element_type=f32)


def _bdot_tn(a, b):
    return lax.dot_general(a.astype(bf16), b.astype(bf16), (((0,), (0,)), ((), ())),
                           preferred_element_type=f32)


def _split2(a):
    hi = a.astype(bf16)
    lo = (a - hi.astype(f32)).astype(bf16)
    return hi, lo


def _split3(a):
    hi = a.astype(bf16)
    r = a - hi.astype(f32)
    mid = r.astype(bf16)
    lo = (r - mid.astype(f32)).astype(bf16)
    return hi, mid, lo


def _dot3(a, b):
    ah, al = _split2(a)
    bh, bl = _split2(b)
    d = functools.partial(jnp.dot, preferred_element_type=f32)
    return d(ah, bh) + (d(ah, bl) + d(al, bh))


def _dot3_nt(a, b):
    ah, al = _split2(a)
    bh, bl = _split2(b)
    d = functools.partial(lax.dot_general, dimension_numbers=(((1,), (1,)), ((), ())),
                          preferred_element_type=f32)
    return d(ah, bh) + (d(ah, bl) + d(al, bh))


def _sel_dot(sel, b):
    bh, bm, bl = _split3(b)
    d = functools.partial(jnp.dot, preferred_element_type=f32)
    return d(sel, bh) + (d(sel, bm) + d(sel, bl))


def _sel_dot_nt(sel, b):
    bh, bm, bl = _split3(b)
    d = functools.partial(lax.dot_general, dimension_numbers=(((1,), (1,)), ((), ())),
                          preferred_element_type=f32)
    return d(sel, bh) + (d(sel, bm) + d(sel, bl))


def _silu(x):
    return x * jax.nn.sigmoid(x)


def _softplus(x):
    return jnp.maximum(x, 0.0) + jnp.log(1.0 + jnp.exp(-jnp.abs(x)))


def _iota(shape, dim):
    return lax.broadcasted_iota(jnp.int32, shape, dim)


def _rms_heads(o, g):
    return o * lax.rsqrt(jnp.mean(o * o, axis=-1, keepdims=True) + EPS) * g


def _mod_kernel(c_ref, w_ref, b_ref, o_ref):
    c = c_ref[...]
    o_ref[...] = _bdot(_silu(c), w_ref[...]) + b_ref[...]


def _mod_call(c16, ada_w, ada_b):
    depth, d, n = ada_w.shape
    tn = _pick(n, (512, 256, 128))
    need = 2 * d * tn * 4 + d * tn * 2 + 4 * 16 * tn * 4 + 2 * 16 * d * 4
    return pl.pallas_call(
        _mod_kernel,
        grid=(depth, n // tn),
        in_specs=[pl.BlockSpec((16, d), lambda l, j: (0, 0)),
                  pl.BlockSpec((None, d, tn), lambda l, j: (l, 0, j)),
                  pl.BlockSpec((None, 1, tn), lambda l, j: (l, 0, j))],
        out_specs=pl.BlockSpec((None, 16, tn), lambda l, j: (l, 0, j)),
        out_shape=jax.ShapeDtypeStruct((depth, 16, n), f32),
        compiler_params=_cparams(("arbitrary", "arbitrary"), need),
        name="adaln_mod",
    )(c16, ada_w, ada_b.reshape(depth, 1, n))


def _premod_kernel(x_ref, g_ref, sh_ref, sc_ref, o_ref):
    g = g_ref[...]
    for s in range(TM // SUB):
        rows = slice(s * SUB, (s + 1) * SUB)
        xs = x_ref[rows, :]
        y = xs * lax.rsqrt(jnp.mean(xs * xs, axis=-1, keepdims=True) + EPS) * g
        o_ref[rows, :] = (y * (1.0 + sc_ref[s:s + 1, :]) + sh_ref[s:s + 1, :]).astype(o_ref.dtype)


def _premod_call(x, g, mod8, v_shift, v_scale):
    m, d = x.shape
    nsub = TM // SUB
    need = 2 * TM * d * 4 + 2 * TM * d * 2 + 6 * nsub * d * 4
    return pl.pallas_call(
        _premod_kernel,
        grid=(m // TM,),
        in_specs=[pl.BlockSpec((TM, d), lambda i: (i, 0)),
                  pl.BlockSpec((1, d), lambda i: (0, 0)),
                  pl.BlockSpec((nsub, d), lambda i: (i, v_shift)),
                  pl.BlockSpec((nsub, d), lambda i: (i, v_scale))],
        out_specs=pl.BlockSpec((TM, d), lambda i: (i, 0)),
        out_shape=jax.ShapeDtypeStruct((m, d), bf16),
        compiler_params=_cparams(("arbitrary",), need),
        name="modulate",
    )(x, g.reshape(1, d), mod8, mod8)


def _pack_kernel(p_ref, s_ref, o_ref, *, n_p):
    i = pl.program_id(0)

    @pl.when(i < n_p)
    def _():
        o_ref[...] = p_ref[...]

    @pl.when(i >= n_p)
    def _():
        o_ref[...] = s_ref[...]


def _pack_call(p_rows, s_rows):
    bs, d = p_rows.shape
    n_p, n_s = bs // TM, s_rows.shape[0] // TM
    return pl.pallas_call(
        functools.partial(_pack_kernel, n_p=n_p),
        grid=(n_p + n_s,),
        in_specs=[pl.BlockSpec((TM, d), lambda i: (jnp.minimum(i, n_p - 1), 0)),
                  pl.BlockSpec((TM, d), lambda i: (jnp.maximum(i - n_p, 0), 0))],
        out_specs=pl.BlockSpec((TM, d), lambda i: (i, 0)),
        out_shape=jax.ShapeDtypeStruct((bs + s_rows.shape[0], d), p_rows.dtype),
        compiler_params=_cparams(("arbitrary",), 6 * TM * d * 4),
        name="pack_tokens",
    )(p_rows, s_rows)


def _rms_kernel(x_ref, g_ref, o_ref):
    x = x_ref[...]
    o_ref[...] = x * lax.rsqrt(jnp.mean(x * x, axis=-1, keepdims=True) + EPS) * g_ref[...]


def _rms_call(x, g, row0, nrows):
    d = x.shape[1]
    blk0 = row0 // TM
    return pl.pallas_call(
        _rms_kernel,
        grid=(nrows // TM,),
        in_specs=[pl.BlockSpec((TM, d), lambda i: (blk0 + i, 0)), pl.BlockSpec((1, d), lambda i: (0, 0))],
        out_specs=pl.BlockSpec((TM, d), lambda i: (i, 0)),
        out_shape=jax.ShapeDtypeStruct((nrows, d), f32),
        compiler_params=_cparams(("arbitrary",), 4 * TM * d * 4),
        name="final_norm",
    )(x, g.reshape(1, d))


MM_VMEM_TARGET = 48 << 20


V7X_BF16_FLOPS = 1.15e15
V7X_HBM_BYTES_PER_S = 3.0e12
STEP_OVERHEAD_S = 0.35e-6
MXU_EFF_BY_TM = {1408: 0.90, 768: 0.85, 512: 0.80, 256: 0.70}


def _ws_plan(m, k, n, n_weights, tile_bytes_per_out_elem, io_bytes_per_out_elem, tns=(1024, 512, 256, 128),
             tms=(768, 512, 256)):
    best = None
    for tn in tns:
        if n % tn:
            continue
        for tm in tms:
            if m % tm:
                continue
            for bufs in (2, 1):
                need = n_weights * k * tn * (4 * bufs + 2) + 2 * tm * k * 2 + tm * tn * tile_bytes_per_out_elem
                if need > MM_VMEM_TARGET:
                    continue
                w_bytes = n_weights * k * n * 4
                hbm = (n // tn) * m * k * 2 + w_bytes + m * n * io_bytes_per_out_elem
                t = max(2.0 * n_weights * m * k * n / V7X_BF16_FLOPS / MXU_EFF_BY_TM[tm], hbm / V7X_HBM_BYTES_PER_S)
                t += (n // tn) * (m // tm) * STEP_OVERHEAD_S
                if bufs == 1:
                    t += w_bytes / V7X_HBM_BYTES_PER_S
                if best is None or t < best[0]:
                    best = (t, tm, tn, bufs, need)
    if best is None:
        raise ValueError(f"no weight-stationary tiling for {(m, k, n)}")
    return best[1:]


def _wspec(shape, index_map, bufs):
    if bufs == 1:
        return pl.BlockSpec(shape, index_map, pipeline_mode=pl.Buffered(1))
    return pl.BlockSpec(shape, index_map)


def _mm_plain_t_kernel(a_ref, w_ref, o_ref, wb_ref):
    @pl.when(pl.program_id(1) == 0)
    def _():
        wb_ref[...] = w_ref[...].astype(bf16)

    o_ref[...] = lax.dot_general(a_ref[...], wb_ref[...], (((1,), (1,)), ((), ())),
                                 preferred_element_type=f32).astype(o_ref.dtype)


def _mm_plain_call(a, wt3, layer, n_cols, name):
    m, k = a.shape
    tm, tn, bufs, need = _ws_plan(m, k, n_cols, 1, 2 * 4 + 4, 4)
    return pl.pallas_call(
        _mm_plain_t_kernel,
        grid=(n_cols // tn, m // tm),
        in_specs=[pl.BlockSpec((tm, k), lambda j, i: (i, 0)),
                  _wspec((None, tn, k), lambda j, i: (layer, j, 0), bufs)],
        out_specs=pl.BlockSpec((tm, tn), lambda j, i: (i, j)),
        out_shape=jax.ShapeDtypeStruct((m, n_cols), f32),
        scratch_shapes=[pltpu.VMEM((tn, k), bf16)],
        compiler_params=_cparams(("arbitrary", "arbitrary"), need),
        name=name,
    )(a, wt3)


def _mm_up_kernel(a_ref, wg_ref, wu_ref, o_ref, wgb_ref, wub_ref):
    @pl.when(pl.program_id(1) == 0)
    def _():
        wgb_ref[...] = wg_ref[...].astype(bf16)
        wub_ref[...] = wu_ref[...].astype(bf16)

    a = a_ref[...]
    g = jnp.dot(a, wgb_ref[...], preferred_element_type=f32)
    u = jnp.dot(a, wub_ref[...], preferred_element_type=f32)
    o_ref[...] = (_silu(g) * u).astype(o_ref.dtype)


def _mm_up_call(a, wg3, wu3, layer):
    m, k = a.shape
    f = wg3.shape[2]
    tm, tn, bufs, need = _ws_plan(m, k, LANES * 8, 2, 2 * 2 + 3 * 4, 2, tns=(512, 256, 128),
                                  tms=(1408, 768, 512, 256))
    tn = min(tn, f)
    n_main = f // tn
    rem = f - n_main * tn
    assert rem % LANES == 0 and (rem == 0 or (n_main * tn) % rem == 0)

    def call(width, col_blk0, ncols, name):
        wspec = _wspec((None, k, width), lambda j, i: (layer, 0, col_blk0 + j), bufs)
        return pl.pallas_call(
            _mm_up_kernel,
            grid=(ncols, m // tm),
            in_specs=[pl.BlockSpec((tm, k), lambda j, i: (i, 0)), wspec, wspec],
            out_specs=pl.BlockSpec((tm, width), lambda j, i: (i, j)),
            out_shape=jax.ShapeDtypeStruct((m, ncols * width), bf16),
            scratch_shapes=[pltpu.VMEM((k, width), bf16), pltpu.VMEM((k, width), bf16)],
            compiler_params=_cparams(("arbitrary", "arbitrary"), need),
            name=name,
        )(a, wg3, wu3)

    outs = [call(tn, 0, n_main, "ffn_up")]
    if rem:
        outs.append(call(rem, (n_main * tn) // rem, 1, "ffn_up_tail"))
    return outs


def _mm_res_kernel(*refs, nseg, scale):
    a_refs, w_refs = refs[:nseg], refs[nseg:2 * nseg]
    r_ref, gate_ref, o_ref = refs[2 * nseg:2 * nseg + 3]
    wb_refs = refs[2 * nseg + 3:]

    @pl.when(pl.program_id(1) == 0)
    def _():
        for w_ref, wb_ref in zip(w_refs, wb_refs):
            wb_ref[...] = w_ref[...].astype(bf16)

    acc = jnp.dot(a_refs[0][...], wb_refs[0][...], preferred_element_type=f32)
    for a_ref, wb_ref in zip(a_refs[1:], wb_refs[1:]):
        acc = acc + jnp.dot(a_ref[...], wb_ref[...], preferred_element_type=f32)
    for s in range(acc.shape[0] // SUB):
        rows = slice(s * SUB, (s + 1) * SUB)
        o_ref[rows, :] = r_ref[rows, :] + (scale * gate_ref[s:s + 1, :]) * acc[rows, :]


def _mm_res_call(acts, w3, layer, res, mod8, v_gate, scale, name):
    m, k0 = acts[0].shape
    n = w3.shape[2]
    nk = 1
    while (k0 // nk) * 1024 * 6 > (36 << 20) and (k0 // nk) % (2 * LANES) == 0:
        nk *= 2
    kc = k0 // nk
    k_rest = sum(a.shape[1] for a in acts[1:])
    tm, tn, bufs, need = _ws_plan(m, kc + k_rest, n, 1, 4 * 4 + 2 * 4 + 4, 8 * nk)
    nsub = tm // SUB
    out = res
    for kb in range(nk):
        segs = [(acts[0], kc, kb, kb)]
        row0 = k0
        if kb == nk - 1:
            for a in acts[1:]:
                assert row0 % a.shape[1] == 0
                segs.append((a, a.shape[1], 0, row0 // a.shape[1]))
                row0 += a.shape[1]
        a_specs = [pl.BlockSpec((tm, w), lambda j, i, cb=cb: (i, cb)) for _, w, cb, _ in segs]
        w_specs = [_wspec((None, w, tn), lambda j, i, rb=rb: (layer, rb, j), bufs) for _, w, _, rb in segs]
        out = pl.pallas_call(
            functools.partial(_mm_res_kernel, nseg=len(segs), scale=scale),
            grid=(n // tn, m // tm),
            in_specs=a_specs + w_specs + [pl.BlockSpec((tm, tn), lambda j, i: (i, j)),
                                          pl.BlockSpec((nsub, tn), lambda j, i: (i, v_gate * (n // tn) + j))],
            out_specs=pl.BlockSpec((tm, tn), lambda j, i: (i, j)),
            out_shape=jax.ShapeDtypeStruct((m, n), f32),
            scratch_shapes=[pltpu.VMEM((w, tn), bf16) for _, w, _, _ in segs],
            compiler_params=_cparams(("arbitrary", "arbitrary"), need),
            name=f"{name}_k{kb}",
        )(*[s[0] for s in segs], *([w3] * len(segs)), out, mod8)
    return out


def _rope_kernel(x_ref, cos_ref, sin_ref, q_ref, k_ref):
    x = x_ref[...]
    w = x.shape[1]
    nh = w // HEAD_DIM
    cosf = jnp.concatenate([cos_ref[...]] * nh, axis=1)
    sinf = jnp.concatenate([sin_ref[...]] * nh, axis=1)
    lane = _iota(x.shape, 1) % HEAD_DIM
    half = ROPE_DIM // 2
    partner = jnp.where(lane < half, pltpu.roll(x, w - half, 1), pltpu.roll(x, half, 1))
    y = x * cosf + partner * sinf
    qw = q_ref.shape[1]
    q_ref[...] = y[:, :qw]
    k_ref[...] = y[:, qw:]


def _rope_call(proj, cos_t, sin_t, qw, kw):
    m = proj.shape[0]
    w = qw + kw
    return pl.pallas_call(
        _rope_kernel,
        grid=(m // TM,),
        in_specs=[pl.BlockSpec((TM, w), lambda i: (i, 0)),
                  pl.BlockSpec((TM, HEAD_DIM), lambda i: (i, 0)),
                  pl.BlockSpec((TM, HEAD_DIM), lambda i: (i, 0))],
        out_specs=[pl.BlockSpec((TM, qw), lambda i: (i, 0)), pl.BlockSpec((TM, kw), lambda i: (i, 0))],
        out_shape=[jax.ShapeDtypeStruct((m, qw), f32), jax.ShapeDtypeStruct((m, kw), f32)],
        compiler_params=_cparams(("arbitrary",), 10 * TM * w * 4),
        name="rope",
    )(proj, cos_t, sin_t)


def _top_blocks(gate, n_valid):
    lane = _iota(gate.shape, 1)
    gate = jnp.where(lane < n_valid, gate, -jnp.inf)
    picks = []
    for kk in range(MOBA_TOPK):
        mx = jnp.max(gate, axis=-1, keepdims=True)
        idx = jnp.min(jnp.where(gate == mx, lane, LANES), axis=-1, keepdims=True)
        picks.append(jnp.where(kk < n_valid, idx, -1))
        gate = jnp.where(lane == idx, -jnp.inf, gate)
    return picks


def _top_blocks_t(gate, n_valid):
    blk = _iota(gate.shape, 0)
    gate = jnp.where(blk < n_valid, gate, -jnp.inf)
    picks = []
    for kk in range(MOBA_TOPK):
        mx = jnp.max(gate, axis=0, keepdims=True)
        idx = jnp.min(jnp.where(gate == mx, blk, gate.shape[0]), axis=0, keepdims=True)
        picks.append(jnp.where(kk < n_valid, idx, -1))
        gate = jnp.where(blk == idx, -jnp.inf, gate)
    return picks


def _moba_prompt_kernel(q_ref, k_ref, v_ref, g_ref, o_ref, kmean_sc):
    i = pl.program_id(2)
    s_len = k_ref.shape[0]
    nb = s_len // MOBA_BLOCK
    scale = HEAD_DIM ** -0.5
    q2 = jnp.concatenate([q_ref[:, :HEAD_DIM], q_ref[:, HEAD_DIM:]], axis=0)
    rows = q2.shape[0]
    @pl.when(i == 0)
    def _():
        blk_row = _iota(kmean_sc.shape, 0)
        kmean = jnp.zeros(kmean_sc.shape, f32)
        for n in range(nb):
            mean_n = jnp.mean(k_ref[n * MOBA_BLOCK:(n + 1) * MOBA_BLOCK, :], axis=0, keepdims=True)
            kmean = jnp.where(blk_row == n, mean_n, kmean)
        kmean_sc[...] = kmean

    gate_t = _dot3_nt(kmean_sc[...], q2)
    picks_t = _top_blocks_t(gate_t, i)
    prow = _iota((SUBLANES, rows), 0)
    pk = jnp.zeros((SUBLANES, rows), f32)
    for kk in range(MOBA_TOPK):
        pk = jnp.where(prow == kk, picks_t[kk].astype(f32), pk)
    picks = [lax.dot_general(pk, (_iota((SUBLANES, LANES), 0) == kk).astype(f32), (((0,), (0,)), ((), ())),
                             preferred_element_type=f32) for kk in range(MOBA_TOPK)]
    qs = (q2 * scale).astype(bf16)

    own = pl.multiple_of(i * MOBA_BLOCK, MOBA_BLOCK)
    s = _bdot_nt(qs, k_ref[pl.ds(own, MOBA_BLOCK), :])
    rq = _iota(s.shape, 0) % MOBA_BLOCK
    ck = _iota(s.shape, 1)
    s = jnp.where(ck <= rq, s, -jnp.inf)
    m0 = jnp.max(s, axis=-1, keepdims=True)
    p = jnp.exp(s - m0)
    l0 = jnp.sum(p, axis=-1, keepdims=True)
    acc0 = _bdot(p, v_ref[pl.ds(own, MOBA_BLOCK), :])

    def masked_scores(n):
        start = pl.multiple_of(jnp.minimum(n, nb - 1) * MOBA_BLOCK, MOBA_BLOCK)
        sn = _bdot_nt(qs, k_ref[pl.ds(start, MOBA_BLOCK), :])
        nf = n.astype(f32)
        sel = (picks[0] == nf) | (picks[1] == nf) | (picks[2] == nf)
        sn = jnp.concatenate([jnp.where(sel, sn[:, t * LANES:(t + 1) * LANES], -jnp.inf)
                              for t in range(MOBA_BLOCK // LANES)], axis=1)
        return sn, start

    def body(t, carry):
        m, l, acc = carry
        sa, start_a = masked_scores(2 * t)
        sb, start_b = masked_scores(2 * t + 1)
        m_new = jnp.maximum(m, jnp.maximum(jnp.max(sa, axis=-1, keepdims=True), jnp.max(sb, axis=-1, keepdims=True)))
        alpha = jnp.exp(m - m_new)
        pa = jnp.exp(sa - m_new)
        pb = jnp.exp(sb - m_new)
        l_new = alpha * l + (jnp.sum(pa, axis=-1, keepdims=True) + jnp.sum(pb, axis=-1, keepdims=True))
        acc_new = alpha * acc + (_bdot(pa, v_ref[pl.ds(start_a, MOBA_BLOCK), :])
                                 + _bdot(pb, v_ref[pl.ds(start_b, MOBA_BLOCK), :]))
        return m_new, l_new, acc_new

    m, l, acc = lax.fori_loop(0, (i + 1) // 2, body, (m0, l0, acc0))
    o = _rms_heads(acc / l, g_ref[...])
    half = rows // 2
    o_ref[:, :HEAD_DIM] = o[:half].astype(o_ref.dtype)
    o_ref[:, HEAD_DIM:] = o[half:].astype(o_ref.dtype)


def _moba_prompt_call(q_rot, k_rot, proj, gnorm, b, s_len, kvh, v_col0):
    qt = s_len // MOBA_BLOCK
    vb = v_col0 // HEAD_DIM
    need = 4 * s_len * HEAD_DIM * 4 + 4 * MOBA_BLOCK * 2 * HEAD_DIM * 4 + 16 * 2 * MOBA_BLOCK * MOBA_BLOCK * 4
    return pl.pallas_call(
        _moba_prompt_kernel,
        grid=(b, kvh, qt),
        in_specs=[pl.BlockSpec((MOBA_BLOCK, 2 * HEAD_DIM), lambda bi, h, i: (bi * qt + i, h)),
                  pl.BlockSpec((s_len, HEAD_DIM), lambda bi, h, i: (bi, h)),
                  pl.BlockSpec((s_len, HEAD_DIM), lambda bi, h, i: (bi, vb + h)),
                  pl.BlockSpec((1, HEAD_DIM), lambda bi, h, i: (0, 0))],
        out_specs=pl.BlockSpec((MOBA_BLOCK, 2 * HEAD_DIM), lambda bi, h, i: (bi * qt + i, h)),
        out_shape=jax.ShapeDtypeStruct((b * s_len, 2 * kvh * HEAD_DIM), bf16),
        scratch_shapes=[pltpu.VMEM((-(-qt // SUBLANES) * SUBLANES, HEAD_DIM), f32)],
        compiler_params=_cparams(("arbitrary", "arbitrary", "arbitrary"), need),
        name="moba_prompt",
    )(q_rot, k_rot, proj, gnorm.reshape(1, HEAD_DIM))


QROWS = 8


def _moba_sample_kernel(pt_ref, q_ref, kn_ref, vn_ref, *refs, nb, bps, kvh, t_len):
    del pt_ref
    npg = bps * (MOBA_BLOCK // PAGE_SIZE)
    k_pages, v_pages = refs[:npg], refs[npg:2 * npg]
    g_ref, o_ref, qs_sc, s_sc, gate_sc, idx_sc, m_sc, l_sc, acc_sc = refs[2 * npg:]
    j = pl.program_id(1)
    nsk = nb // bps
    scale = HEAD_DIM ** -0.5
    r2 = 2 * QROWS

    def block_of(pages, bb, h):
        ppb = MOBA_BLOCK // PAGE_SIZE
        return jnp.concatenate([pages[bb * ppb + t][pl.ds(h, PAGE_SIZE, stride=kvh), :] for t in range(ppb)], axis=0)

    @pl.when(j == 0)
    def _():
        for h in range(kvh):
            q2 = jnp.concatenate([q_ref[0:QROWS, (2 * h) * HEAD_DIM:(2 * h + 1) * HEAD_DIM],
                                  q_ref[0:QROWS, (2 * h + 1) * HEAD_DIM:(2 * h + 2) * HEAD_DIM]], axis=0)
            qs_sc[h] = q2
        gate_sc[...] = jnp.zeros_like(gate_sc)

    @pl.when(j < nsk)
    def _():
        lane = _iota((r2, LANES), 1)
        pairs = [(bb, h) for bb in range(bps) for h in range(kvh)]
        q2s = [qs_sc[h] for h in range(kvh)]
        khs = [block_of(k_pages, bb, h) for bb, h in pairs]
        scores = [_bdot_nt(q2s[h] * scale, kh) for (bb, h), kh in zip(pairs, khs)]
        kmeans = [jnp.sum(kh, axis=0, keepdims=True) * (1.0 / MOBA_BLOCK) for kh in khs]
        cols = [jnp.sum(q2s[h] * km, axis=-1, keepdims=True) for (bb, h), km in zip(pairs, kmeans)]
        for (bb, h), sc in zip(pairs, scores):
            s_sc[j * bps + bb, h] = sc
        for h in range(kvh):
            gate = gate_sc[h]
            for (bb, hh), col in zip(pairs, cols):
                if hh == h:
                    gate = jnp.where(lane == j * bps + bb, col, gate)
            gate_sc[h] = gate

    @pl.when(j == nsk - 1)
    def _():
        for h in range(kvh):
            picks = _top_blocks(gate_sc[h], nb)
            for kk in range(MOBA_TOPK):
                idx_sc[h * MOBA_TOPK + kk] = jnp.broadcast_to(picks[kk], (r2, LANES))
            kn = jnp.concatenate([kn_ref[0:QROWS, h * HEAD_DIM:(h + 1) * HEAD_DIM],
                                  jnp.zeros((LANES - QROWS, HEAD_DIM), f32)], axis=0)
            vn = jnp.concatenate([vn_ref[0:QROWS, h * HEAD_DIM:(h + 1) * HEAD_DIM],
                                  jnp.zeros((LANES - QROWS, HEAD_DIM), f32)], axis=0)
            s = _bdot_nt(qs_sc[h] * scale, kn)
            tq = _iota(s.shape, 0) % QROWS
            ck = _iota(s.shape, 1)
            s = jnp.where((ck <= tq) & (ck < t_len), s, -jnp.inf)
            m0 = jnp.max(s, axis=-1, keepdims=True)
            p = jnp.exp(s - m0)
            m_sc[h] = jnp.broadcast_to(m0, (r2, LANES))
            l_sc[h] = jnp.broadcast_to(jnp.sum(p, axis=-1, keepdims=True), (r2, LANES))
            acc_sc[h] = _bdot(p, vn)

    @pl.when(j >= nsk)
    def _():
        heads_ = range(kvh)
        picks = [[idx_sc[h * MOBA_TOPK + kk][:, 0:1] for kk in range(MOBA_TOPK)] for h in heads_]
        sns = []
        for h in heads_:
            row = []
            for bb in range(bps):
                n = (j - nsk) * bps + bb
                sel = (picks[h][0] == n) | (picks[h][1] == n) | (picks[h][2] == n)
                row.append(jnp.where(sel, s_sc[n, h], -jnp.inf))
            sns.append(row)
        ms = [m_sc[h][:, 0:1] for h in heads_]
        m_news = []
        for h in heads_:
            mx = ms[h]
            for sn in sns[h]:
                mx = jnp.maximum(mx, jnp.max(sn, axis=-1, keepdims=True))
            m_news.append(mx)
        alphas = [jnp.exp(ms[h] - m_news[h]) for h in heads_]
        pns = [[jnp.exp(sn - m_news[h]) for sn in sns[h]] for h in heads_]
        pvs = [[_bdot(pns[h][bb], block_of(v_pages, bb, h)) for bb in range(bps)] for h in heads_]
        for h in heads_:
            l = alphas[h] * l_sc[h][:, 0:1]
            acc = alphas[h] * acc_sc[h]
            for bb in range(bps):
                l = l + jnp.sum(pns[h][bb], axis=-1, keepdims=True)
                acc = acc + pvs[h][bb]
            m_sc[h] = jnp.broadcast_to(m_news[h], (r2, LANES))
            l_sc[h] = jnp.broadcast_to(l, (r2, LANES))
            acc_sc[h] = acc

    @pl.when(j == 2 * nsk - 1)
    def _():
        for h in range(kvh):
            o = _rms_heads(acc_sc[h] / l_sc[h][:, 0:1], g_ref[...])
            o_ref[:, (2 * h) * HEAD_DIM:(2 * h + 1) * HEAD_DIM] = o[:QROWS]
            o_ref[:, (2 * h + 1) * HEAD_DIM:(2 * h + 2) * HEAD_DIM] = o[QROWS:]


def _moba_sample_call(page_table, q_rot, k_rot, proj, cache_k, cache_v, gnorm, layer,
                      db, row0, kvh, v_col0, t_len):
    n_pages = page_table.shape[1]
    nb = n_pages * PAGE_SIZE // MOBA_BLOCK
    ppb = MOBA_BLOCK // PAGE_SIZE
    bps = _pick(nb, (8, 4, 2, 1))
    nsk = nb // bps
    npg = bps * ppb
    qw = 2 * kvh * HEAD_DIM
    kw = kvh * HEAD_DIM
    blk0 = row0 // SUB

    def kmap(t):
        return lambda b, j, pt: (layer, pt[b, npg * jnp.minimum(j, nsk - 1) + t], 0, 0)

    def vmap_(t):
        return lambda b, j, pt: (layer, pt[b, npg * jnp.maximum(j - nsk, 0) + t], 0, 0)

    depth, n_pool = cache_k.shape[:2]
    cache_k = cache_k.reshape(depth, n_pool, PAGE_SIZE * kvh, HEAD_DIM)
    cache_v = cache_v.reshape(depth, n_pool, PAGE_SIZE * kvh, HEAD_DIM)
    page_spec = lambda fn: pl.BlockSpec((None, None, PAGE_SIZE * kvh, HEAD_DIM), fn)
    r2 = 2 * QROWS
    need = 4 * npg * PAGE_SIZE * kvh * HEAD_DIM * 4 + nb * kvh * r2 * MOBA_BLOCK * 4 + (8 << 20)
    grid_spec = pltpu.PrefetchScalarGridSpec(
        num_scalar_prefetch=1,
        grid=(db, 2 * nsk),
        in_specs=[pl.BlockSpec((SUB, qw), lambda b, j, pt: (blk0 + b, 0)),
                  pl.BlockSpec((SUB, kw), lambda b, j, pt: (blk0 + b, 0)),
                  pl.BlockSpec((SUB, kw), lambda b, j, pt: (blk0 + b, v_col0 // kw))]
                 + [page_spec(kmap(t)) for t in range(npg)] + [page_spec(vmap_(t)) for t in range(npg)]
                 + [pl.BlockSpec((1, HEAD_DIM), lambda b, j, pt: (0, 0))],
        out_specs=pl.BlockSpec((QROWS, qw), lambda b, j, pt: (b, 0)),
        scratch_shapes=[pltpu.VMEM((kvh, r2, HEAD_DIM), f32),
                        pltpu.VMEM((nb, kvh, r2, MOBA_BLOCK), f32),
                        pltpu.VMEM((kvh, r2, LANES), f32),
                        pltpu.VMEM((kvh * MOBA_TOPK, r2, LANES), jnp.int32),
                        pltpu.VMEM((kvh, r2, LANES), f32),
                        pltpu.VMEM((kvh, r2, LANES), f32),
                        pltpu.VMEM((kvh, r2, HEAD_DIM), f32)])
    return pl.pallas_call(
        functools.partial(_moba_sample_kernel, nb=nb, bps=bps, kvh=kvh, t_len=t_len),
        grid_spec=grid_spec,
        out_shape=jax.ShapeDtypeStruct((db * QROWS, qw), f32),
        compiler_params=_cparams(("arbitrary", "arbitrary"), need),
        name="moba_sample",
    )(page_table, q_rot, k_rot, proj, *([cache_k] * npg), *([cache_v] * npg), gnorm.reshape(1, HEAD_DIM))


def _inv_unit_lower(lmats, nblk):
    c = lmats[0].shape[0]
    ii = _iota((c, c), 0)
    jj = _iota((c, c), 1)
    eye = (ii == jj).astype(f32)
    same = (ii // INV_BLK) == (jj // INV_BLK)
    dmats = [jnp.where(same, l, 0.0) for l in lmats]
    ps = [-d for d in dmats]
    xs = [eye + p for p in ps]
    k = 2
    while k < INV_BLK:
        ps = [_bdot(p, p) for p in ps]
        xs = [x + _bdot(x, p) for x, p in zip(xs, ps)]
        k *= 2
    if nblk > 1:
        mms = [_bdot(x, l - d) for x, l, d in zip(xs, lmats, dmats)]
        ys = [eye - mm for mm in mms]
        pms = mms
        k = 2
        while k < nblk:
            pms = [_bdot(pm, pm) for pm in pms]
            ys = [y + _bdot(y, pm) for y, pm in zip(ys, pms)]
            k *= 2
        xs = [_bdot(y, x) for y, x in zip(ys, xs)]
    for _ in range(2):
        rs = [eye - _dot3(eye + l, x) for l, x in zip(lmats, xs)]
        xs = [x + _bdot(x, r) for x, r in zip(xs, rs)]
    return xs


def _gdn_kernel(main_ref, ab_ref, cw_ref, prev_ref, alog_ref, dtb_ref, s0_ref, g_ref, o_ref, s_ref, carry_sc,
                *, heads, group, col0, t_valid):
    c_idx = pl.program_id(1)
    c = main_ref.shape[0]
    gw = heads * HEAD_DIM
    ch = 3 * gw

    @pl.when(c_idx == 0)
    def _():
        s_ref[...] = s0_ref[...]
        carry_sc[0:SUBLANES, :] = prev_ref[...]

    x = main_ref[:, col0:col0 + ch]
    carry_sc[SUBLANES:, :] = x
    y = cw_ref[GDN_CONV - 1:GDN_CONV, :] * x
    for tap in range(1, GDN_CONV):
        y = y + cw_ref[GDN_CONV - 1 - tap:GDN_CONV - tap, :] * carry_sc[SUBLANES - tap:SUBLANES - tap + c, :]
    carry_sc[0:SUBLANES, :] = x[c - SUBLANES:, :]
    y = _silu(y)

    ab = ab_ref[...]
    lane = _iota(ab.shape, 1)
    row_ok = (c_idx * c + _iota((c, 1), 0)) < t_valid
    g_all = jnp.where(row_ok & (lane < heads), -jnp.exp(alog_ref[...]) * _softplus(ab + dtb_ref[...]), 0.0)
    beta_all = jnp.where(row_ok, jax.nn.sigmoid(ab), 0.0)
    tri = (_iota((c, c), 0) >= _iota((c, c), 1)).astype(bf16)
    gam_all = _sel_dot(tri, g_all)
    eye_l = (_iota((LANES, LANES), 0) == _iota((LANES, LANES), 1)).astype(bf16)
    gam_t = _sel_dot_nt(eye_l, gam_all)
    gnorm = g_ref[...]

    r = group * c
    ii = _iota((r, r), 0)
    jj = _iota((r, r), 1)
    same = (ii // c) == (jj // c)
    low = same & (ii >= jj)
    strict = same & (ii > jj)
    groups = [range(g * group, (g + 1) * group) for g in range(heads // group)]
    stack = lambda hs, f: jnp.concatenate([f(h) for h in hs], axis=0)
    qs, ks, vs, betas, gcols, decays, egams, kds = [], [], [], [], [], [], [], []
    for hs in groups:
        xq = stack(hs, lambda h: y[:, h * HEAD_DIM:(h + 1) * HEAD_DIM])
        xk = stack(hs, lambda h: y[:, gw + h * HEAD_DIM:gw + (h + 1) * HEAD_DIM])
        vs.append(stack(hs, lambda h: y[:, 2 * gw + h * HEAD_DIM:2 * gw + (h + 1) * HEAD_DIM]))
        qs.append(xq * lax.rsqrt(jnp.sum(xq * xq, axis=-1, keepdims=True) + EPS) * (HEAD_DIM ** -0.5))
        ks.append(xk * lax.rsqrt(jnp.sum(xk * xk, axis=-1, keepdims=True) + EPS))
        betas.append(stack(hs, lambda h: beta_all[:, heads + h:heads + h + 1]))
        gcol = stack(hs, lambda h: gam_all[:, h:h + 1])
        glast = stack(hs, lambda h: jnp.broadcast_to(gam_all[c - 1:c, h:h + 1], (c, 1)))
        grow = jnp.concatenate([gam_t[h:h + 1, :] for h in hs], axis=1)
        gcols.append(gcol)
        egams.append(jnp.exp(gcol))
        decays.append(jnp.exp(jnp.where(low, gcol - grow, -jnp.inf)))
        kds.append(ks[-1] * jnp.exp(glast - gcol))
    kks = [_bdot_nt(k, k) for k in ks]
    lmats = [jnp.where(strict, beta * kk * decay, 0.0) for beta, kk, decay in zip(betas, kks, decays)]
    tmats = _inv_unit_lower(lmats, c // INV_BLK)
    us = [_bdot(t, beta * v) for t, beta, v in zip(tmats, betas, vs)]
    ws = [_bdot(t, (beta * egam) * k) for t, beta, egam, k in zip(tmats, betas, egams, ks)]
    qks = [_bdot_nt(q, k) * decay for q, k, decay in zip(qs, ks, decays)]
    qes = [q * egam for q, egam in zip(qs, egams)]
    rows_of = lambda a: slice(a * c, (a + 1) * c)
    states = [[s_ref[h] for h in hs] for hs in groups]
    v_news = [jnp.concatenate([u[rows_of(a), :] - _bdot(w[rows_of(a), :], st[a]) for a in range(group)], axis=0)
              for u, w, st in zip(us, ws, states)]
    o_inters = [jnp.concatenate([_bdot(qe[rows_of(a), :], st[a]) for a in range(group)], axis=0)
                for qe, st in zip(qes, states)]
    os_ = [oi + _bdot(qk, vn) for oi, qk, vn in zip(o_inters, qks, v_news)]
    for hs, st, kd, vn, o in zip(groups, states, kds, v_news, os_):
        for a, h in enumerate(hs):
            s_ref[h] = jnp.exp(gam_all[c - 1:c, h:h + 1]) * st[a] + _bdot_tn(kd[rows_of(a), :], vn[rows_of(a), :])
            z = main_ref[:, col0 + ch + h * HEAD_DIM:col0 + ch + (h + 1) * HEAD_DIM]
            o_ref[:, h * HEAD_DIM:(h + 1) * HEAD_DIM] = (_rms_heads(o[rows_of(a), :], gnorm) * _silu(z)).astype(o_ref.dtype)


def _gdn_call(proj, tail, cw, prev8, alog, dtb, s0, gnorm, layer, *, nseq, rows_per_seq, row0, chunk, heads,
              col0, ab_col0, t_valid, name):
    mainw = proj.shape[1]
    gw = heads * HEAD_DIM
    ch = 3 * gw
    nch = rows_per_seq // chunk
    blk0 = row0 // chunk
    need = 2 * chunk * mainw * 4 + 4 * heads * HEAD_DIM * HEAD_DIM * 4 + 40 * chunk * ch * 4 + (4 << 20)
    return pl.pallas_call(
        functools.partial(_gdn_kernel, heads=heads, group=1, col0=col0, t_valid=t_valid),
        grid=(nseq, nch),
        in_specs=[pl.BlockSpec((chunk, mainw), lambda b, c: (blk0 + b * nch + c, 0)),
                  pl.BlockSpec((chunk, LANES), lambda b, c: (blk0 + b * nch + c, ab_col0 // LANES)),
                  pl.BlockSpec((None, GDN_CONV, ch), lambda b, c: (layer, 0, 0)),
                  pl.BlockSpec((None, SUBLANES, ch), lambda b, c: (b, 0, 0)),
                  pl.BlockSpec((1, LANES), lambda b, c: (0, 0)),
                  pl.BlockSpec((1, LANES), lambda b, c: (0, 0)),
                  pl.BlockSpec((None, heads, HEAD_DIM, HEAD_DIM), lambda b, c: (b, 0, 0, 0)),
                  pl.BlockSpec((1, HEAD_DIM), lambda b, c: (0, 0))],
        out_specs=[pl.BlockSpec((chunk, gw), lambda b, c: (b * nch + c, 0)),
                   pl.BlockSpec((None, heads, HEAD_DIM, HEAD_DIM), lambda b, c: (b, 0, 0, 0))],
        out_shape=[jax.ShapeDtypeStruct((nseq * rows_per_seq, gw), bf16),
                   jax.ShapeDtypeStruct((nseq, heads, HEAD_DIM, HEAD_DIM), f32)],
        scratch_shapes=[pltpu.VMEM((SUBLANES + chunk, ch), f32)],
        compiler_params=_cparams(("arbitrary", "arbitrary"), need),
        name=name,
    )(proj, tail, cw, prev8, alog, dtb, s0, gnorm.reshape(1, HEAD_DIM))


def _gla_kernel(tail_ref, wg_ref, bg_ref, s0_ref, g_ref, o_ref, s_ref, *, heads, ab_col0, t_valid):
    c_idx = pl.program_id(1)
    c = tail_ref.shape[0]
    kw = heads * GLA_DK
    vw = heads * HEAD_DIM

    @pl.when(c_idx == 0)
    def _():
        s_ref[...] = s0_ref[...]

    row_ok = (c_idx * c + _iota((c, 1), 0)) < t_valid
    fblk = tail_ref[:, ab_col0:ab_col0 + LANES]
    pre = _dot3(fblk, wg_ref[...]) + bg_ref[...]
    log_a = jnp.where(row_ok, -_softplus(-pre) * (1.0 / GLA_TAU), 0.0)
    tri = (_iota((c, c), 0) >= _iota((c, c), 1)).astype(bf16)
    bc_all = _sel_dot(tri, log_a)
    gnorm = g_ref[...]
    lane = _iota((c, LANES), 1)
    first = lane < GLA_DK
    ii = _iota((c, LANES), 0)
    jj = lane % GLA_DK
    rowid = _iota((c, 1), 0)
    eye_l = _iota((LANES, LANES), 0) == _iota((LANES, LANES), 1)
    seg = (((_iota((2 * LANES, LANES), 0) % LANES) < GLA_DK) == (_iota((2 * LANES, LANES), 1) < GLA_DK)).astype(bf16)
    zpad = jnp.zeros((GLA_DK - c, LANES), f32) if c < GLA_DK else None

    def stack_pair(a):
        a0 = jnp.where(first, a, 0.0)
        a1 = jnp.where(first, 0.0, a)
        parts = [a0, a1] if zpad is None else [a0, zpad, a1, zpad]
        return jnp.concatenate(parts, axis=0)

    pairs = range(heads // 2)
    qs = [tail_ref[:, p * LANES:(p + 1) * LANES] * (GLA_DK ** -0.5) for p in pairs]
    ks = [jnp.where(row_ok, tail_ref[:, kw + p * LANES:kw + (p + 1) * LANES], 0.0) for p in pairs]
    bcs = [bc_all[:, p * LANES:(p + 1) * LANES] for p in pairs]

    def v_stack(p):
        v0 = tail_ref[:, 2 * kw + (2 * p) * HEAD_DIM:2 * kw + (2 * p + 1) * HEAD_DIM]
        v1 = tail_ref[:, 2 * kw + (2 * p + 1) * HEAD_DIM:2 * kw + (2 * p + 2) * HEAD_DIM]
        vparts = [v0, v1] if zpad is None else [v0, zpad, v1, zpad]
        return jnp.concatenate(vparts, axis=0).astype(bf16)

    v2s = [v_stack(p) for p in pairs]
    states = [s_ref[p] for p in pairs]
    qes = [q * jnp.exp(bc) for q, bc in zip(qs, bcs)]
    o0s = [_bdot(jnp.where(first, qe, 0.0), s) for qe, s in zip(qes, states)]
    o1s = [_bdot(jnp.where(first, 0.0, qe), s) for qe, s in zip(qes, states)]

    pieces = [[jnp.zeros((GLA_SUB, LANES), f32)] for _ in pairs]
    for sb in range(1, c // GLA_SUB):
        r0 = sb * GLA_SUB
        rs = slice(r0, r0 + GLA_SUB)
        q_is = [q[rs, :] * jnp.exp(bc[rs, :] - bc[r0:r0 + 1, :]) for q, bc in zip(qs, bcs)]
        k_js = [k * jnp.exp(jnp.where(rowid < r0, bc[r0:r0 + 1, :] - bc, -jnp.inf)) for k, bc in zip(ks, bcs)]
        for p in pairs:
            pieces[p].append(_bdot_nt(q_is[p], stack_pair(k_js[p])))
    att_offs = [jnp.concatenate(pc, axis=0) for pc in pieces]

    atts = [jnp.zeros((c, LANES), f32) for _ in pairs]
    for dlt in range(GLA_SUB):
        ok = (rowid % GLA_SUB) >= dlt
        prs = []
        for p in pairs:
            k_r = pltpu.roll(ks[p], dlt, 0) if dlt else ks[p]
            bc_r = pltpu.roll(bcs[p], dlt, 0) if dlt else bcs[p]
            prs.append(qs[p] * k_r * jnp.exp(jnp.where(ok, bcs[p] - bc_r, -jnp.inf)))
        splits = [_split2(pr) for pr in prs]
        sums = [jnp.dot(jnp.concatenate([hi, lo], axis=1), seg, preferred_element_type=f32) for hi, lo in splits]
        atts = [jnp.where(jj == ii - dlt, sm, att) for sm, att in zip(sums, atts)]
    atts = [att + off for att, off in zip(atts, att_offs)]

    o0s = [o0 + _bdot(jnp.where(first, att, 0.0), v2) for o0, att, v2 in zip(o0s, atts, v2s)]
    o1s = [o1 + _bdot(jnp.where(first, 0.0, att), v2) for o1, att, v2 in zip(o1s, atts, v2s)]
    bls = [bc[c - 1:c, :] for bc in bcs]
    ebl_cols = [jnp.sum(jnp.where(eye_l, jnp.exp(bl), 0.0), axis=-1, keepdims=True) for bl in bls]
    kds = [stack_pair(k * jnp.exp(bl - bc)) for k, bl, bc in zip(ks, bls, bcs)]
    for p in pairs:
        s_ref[p] = ebl_cols[p] * states[p] + _bdot_tn(kds[p], v2s[p])
        for a, o in ((0, o0s[p]), (1, o1s[p])):
            h = 2 * p + a
            r_ = tail_ref[:, 2 * kw + vw + h * HEAD_DIM:2 * kw + vw + (h + 1) * HEAD_DIM]
            o_ref[:, h * HEAD_DIM:(h + 1) * HEAD_DIM] = (_rms_heads(o, gnorm) * _silu(r_)).astype(o_ref.dtype)


def _gla_call(tail, wg_pad, bg, s0, gnorm, *, nseq, rows_per_seq, row0, chunk, heads, ab_col0, t_valid, name):
    tailw = tail.shape[1]
    kw = heads * GLA_DK
    vw = heads * HEAD_DIM
    nch = rows_per_seq // chunk
    blk0 = row0 // chunk
    assert heads % 2 == 0 and chunk <= GLA_DK and chunk % GLA_SUB == 0
    s0 = s0.reshape(nseq, heads // 2, 2 * GLA_DK, HEAD_DIM)
    need = 2 * chunk * tailw * 4 + 4 * heads * GLA_DK * HEAD_DIM * 4 + 2 * LANES * kw * 4 + 40 * chunk * tailw * 4
    o, s_fin = pl.pallas_call(
        functools.partial(_gla_kernel, heads=heads, ab_col0=ab_col0, t_valid=t_valid),
        grid=(nseq, nch),
        in_specs=[pl.BlockSpec((chunk, tailw), lambda b, c: (blk0 + b * nch + c, 0)),
                  pl.BlockSpec((LANES, kw), lambda b, c: (0, 0)),
                  pl.BlockSpec((1, kw), lambda b, c: (0, 0)),
                  pl.BlockSpec((None, heads // 2, 2 * GLA_DK, HEAD_DIM), lambda b, c: (b, 0, 0, 0)),
                  pl.BlockSpec((1, HEAD_DIM), lambda b, c: (0, 0))],
        out_specs=[pl.BlockSpec((chunk, vw), lambda b, c: (b * nch + c, 0)),
                   pl.BlockSpec((None, heads // 2, 2 * GLA_DK, HEAD_DIM), lambda b, c: (b, 0, 0, 0))],
        out_shape=[jax.ShapeDtypeStruct((nseq * rows_per_seq, vw), bf16),
                   jax.ShapeDtypeStruct((nseq, heads // 2, 2 * GLA_DK, HEAD_DIM), f32)],
        compiler_params=_cparams(("arbitrary", "arbitrary"), need),
        name=name,
    )(tail, wg_pad, bg.reshape(1, kw), s0, gnorm.reshape(1, HEAD_DIM))
    return o, s_fin.reshape(nseq, heads, GLA_DK, HEAD_DIM)


def kernel(x_prompt, x_sample, c_prompt, c_sample, cache_k, cache_v, page_table, state_gdn, state_gdn_conv, state_gla, ada_w, ada_b, norm_ffn_a, ffn_a_wg, ffn_a_wu, ffn_a_wd, norm_mix, w_in, moba_norm, gdn_conv_w, gdn_a_log, gdn_dt_bias, gdn_norm, gla_w_gate, gla_b_gate, gla_norm, w_out, norm_ffn_b, ffn_b_wg, ffn_b_wu, ffn_b_wd, final_norm):
    b, s_len, d = x_prompt.shape
    db, t_dec, _ = x_sample.shape
    depth = ada_w.shape[0]
    n_pages = page_table.shape[1]
    past_len = n_pages * PAGE_SIZE

    n_heads = d // HEAD_DIM
    moba_h = n_heads // 4
    kvh = moba_h // 2
    gdn_h = (3 * n_heads) // 8
    gla_h = n_heads - moba_h - gdn_h
    qw, kvw = moba_h * HEAD_DIM, kvh * HEAD_DIM
    gdn_w, gla_kw, gla_w = gdn_h * HEAD_DIM, gla_h * GLA_DK, gla_h * HEAD_DIM
    main_w = qw + 2 * kvw + 4 * gdn_w
    small0 = main_w
    gla0 = main_w + 2 * gdn_h
    f0 = gla0 + 2 * gla_kw + 2 * gla_w
    tail_main = 2 * gla_kw + 2 * gla_w
    assert w_in.shape[2] == f0 + GLA_GATE_RANK
    assert s_len % TM == 0 and s_len % CHUNK == 0 and s_len >= MOBA_TOPK * MOBA_BLOCK
    assert (db * SUB) % TM == 0 and b + db <= 16
    assert GDN_CONV - 1 <= t_dec <= QROWS and past_len % MOBA_BLOCK == 0
    assert 2 * gdn_h + GLA_GATE_RANK <= LANES and tail_main % LANES == 0

    bs = b * s_len
    m_tot = bs + db * SUB
    n_tiles_p = bs // TM

    def pack_rows(p_rows, s_rows):
        s_pad = jnp.pad(s_rows, ((0, 0), (0, SUB - s_rows.shape[1]), (0, 0)))
        return jnp.concatenate([p_rows, s_pad.reshape(db * SUB, s_rows.shape[2])], axis=0)

    def sample_rows(a):
        return a[bs:].reshape(db, SUB, a.shape[1])[:, :t_dec]

    x = _pack_call(x_prompt.reshape(bs, d),
                   jnp.pad(x_sample, ((0, 0), (0, SUB - t_dec), (0, 0))).reshape(db * SUB, d))

    c16 = jnp.concatenate([c_prompt, c_sample, jnp.zeros((16 - b - db, d), f32)], axis=0)
    mod = _mod_call(c16, ada_w, ada_b)

    half = ROPE_DIM // 2
    inv_freq = ROPE_THETA ** (-jnp.arange(half, dtype=f32) / half)
    pos_s = past_len + jnp.minimum(jnp.arange(SUB), t_dec - 1)
    pos = jnp.concatenate([jnp.tile(jnp.arange(s_len), b), jnp.tile(pos_s, db)]).astype(f32)
    ang = pos[:, None] * inv_freq[None, :]
    ones = jnp.ones((m_tot, HEAD_DIM - ROPE_DIM), f32)
    cos_t = jnp.concatenate([jnp.cos(ang), jnp.cos(ang), ones], axis=1)
    sin_t = jnp.concatenate([-jnp.sin(ang), jnp.sin(ang), 0.0 * ones], axis=1)

    tail_w = -(-(tail_main + LANES) // 512) * 512
    w_in_t = jnp.swapaxes(w_in, 1, 2)
    w_tail_t = jnp.concatenate([w_in_t[:, gla0:f0], w_in_t[:, small0:gla0], w_in_t[:, f0:],
                                jnp.zeros((depth, tail_w - tail_main - 2 * gdn_h - GLA_GATE_RANK, d), f32)], axis=1)
    zeros_prev = jnp.zeros((b, SUBLANES, 3 * gdn_w), f32)

    outs = {k: [] for k in ("kp", "vp", "ks", "vs", "gp", "gs", "cp", "cs", "lp", "ls")}
    for l in range(depth):
        mod8 = jnp.concatenate([jnp.repeat(mod[l, :b], (s_len // TM) * (TM // SUB), axis=0), mod[l, b:b + db]], axis=0)

        h = _premod_call(x, norm_ffn_a[l], mod8, 0, 1)
        a = _mm_up_call(h, ffn_a_wg, ffn_a_wu, l)
        x = _mm_res_call(a, ffn_a_wd, l, x, mod8, 2, 0.5, "ffn_a_down")

        h = _premod_call(x, norm_mix[l], mod8, 3, 4)
        proj = _mm_plain_call(h, w_in_t, l, main_w, "w_in_main")
        tail = _mm_plain_call(h, w_tail_t, l, tail_w, "w_in_tail")

        q_rot, k_rot = _rope_call(proj, cos_t, sin_t, qw, kvw)
        v_col0 = qw + kvw
        om_p = _moba_prompt_call(q_rot, k_rot, proj, moba_norm[l], b, s_len, kvh, v_col0)
        om_s = _moba_sample_call(page_table, q_rot, k_rot, proj, cache_k, cache_v, moba_norm[l], l,
                                 db, bs, kvh, v_col0, t_dec)
        om_s = om_s.reshape(db, QROWS, qw)[:, :t_dec].astype(bf16)

        col0 = qw + 2 * kvw
        alog = jnp.zeros((1, LANES), f32).at[0, :gdn_h].set(gdn_a_log[l])
        dtb = jnp.zeros((1, LANES), f32).at[0, :gdn_h].set(gdn_dt_bias[l])
        od_p, gp = _gdn_call(proj, tail, gdn_conv_w, zeros_prev, alog, dtb,
                             jnp.zeros((b, gdn_h, HEAD_DIM, HEAD_DIM), f32), gdn_norm[l], l,
                             nseq=b, rows_per_seq=s_len, row0=0, chunk=CHUNK, heads=gdn_h, col0=col0,
                             ab_col0=tail_main, t_valid=s_len, name="gdn_prompt")
        prev_s = jnp.pad(state_gdn_conv[l], ((0, 0), (SUBLANES - (GDN_CONV - 1), 0), (0, 0)))
        od_s, gs = _gdn_call(proj, tail, gdn_conv_w, prev_s, alog, dtb, state_gdn[l], gdn_norm[l], l,
                             nseq=db, rows_per_seq=SUB, row0=bs, chunk=SUB, heads=gdn_h, col0=col0,
                             ab_col0=tail_main, t_valid=t_dec, name="gdn_sample")

        wg_pad = jnp.zeros((LANES, gla_kw), f32).at[2 * gdn_h:2 * gdn_h + GLA_GATE_RANK].set(gla_w_gate[l])
        ol_p, lp = _gla_call(tail, wg_pad, gla_b_gate[l], jnp.zeros((b, gla_h, GLA_DK, HEAD_DIM), f32), gla_norm[l],
                             nseq=b, rows_per_seq=s_len, row0=0, chunk=CHUNK, heads=gla_h, ab_col0=tail_main,
                             t_valid=s_len, name="gla_prompt")
        ol_s, ls = _gla_call(tail, wg_pad, gla_b_gate[l], state_gla[l], gla_norm[l],
                             nseq=db, rows_per_seq=SUB, row0=bs, chunk=SUB, heads=gla_h, ab_col0=tail_main,
                             t_valid=t_dec, name="gla_sample")

        o_p = jnp.concatenate([om_p, od_p, ol_p], axis=1)
        o_s = jnp.concatenate([om_s, od_s.reshape(db, SUB, gdn_w)[:, :t_dec], ol_s.reshape(db, SUB, gla_w)[:, :t_dec]], axis=2)
        o_mix = pack_rows(o_p, o_s)
        x = _mm_res_call([o_mix], w_out, l, x, mod8, 5, 1.0, "w_out")

        h = _premod_call(x, norm_ffn_b[l], mod8, 6, 7)
        a = _mm_up_call(h, ffn_b_wg, ffn_b_wu, l)
        x = _mm_res_call(a, ffn_b_wd, l, x, mod8, 8, 0.5, "ffn_b_down")

        conv_cols = slice(col0, col0 + 3 * gdn_w)
        proj_s = proj[bs:].reshape(db, SUB, main_w)
        outs["kp"].append(k_rot[:bs].reshape(b, s_len, kvh, HEAD_DIM))
        outs["vp"].append(proj[:bs, v_col0:v_col0 + kvw].reshape(b, s_len, kvh, HEAD_DIM))
        outs["ks"].append(sample_rows(k_rot).reshape(db, t_dec, kvh, HEAD_DIM))
        outs["vs"].append(proj_s[:, :t_dec, v_col0:v_col0 + kvw].reshape(db, t_dec, kvh, HEAD_DIM))
        outs["gp"].append(gp)
        outs["gs"].append(gs)
        outs["cp"].append(jnp.stack([proj[(i + 1) * s_len - (GDN_CONV - 1):(i + 1) * s_len, conv_cols] for i in range(b)]))
        outs["cs"].append(proj_s[:, t_dec - (GDN_CONV - 1):t_dec, conv_cols])
        outs["lp"].append(lp)
        outs["ls"].append(ls)

    y_p = _rms_call(x, final_norm, 0, bs)
    y_s = _rms_call(x, final_norm, bs, db * SUB)
    st = {k: jnp.stack(v) for k, v in outs.items()}
    return (y_p.reshape(b, s_len, d), y_s.reshape(db, SUB, d)[:, :t_dec],
            st["kp"], st["vp"], st["ks"], st["vs"], st["gp"], st["gs"], st["cp"], st["cs"], st["lp"], st["ls"])
```

```python
import functools
import math

import jax
import jax.numpy as jnp
from jax import lax
from jax.experimental import pallas as pl
from jax.experimental.pallas import tpu as pltpu

f32 = jnp.float32
bf16 = jnp.bfloat16

HEAD_DIM = 128
MOBA_BLOCK = 256
MOBA_TOPK = 3
ROPE_THETA = 500000.0
ROPE_DIM = HEAD_DIM // 4
GDN_CONV = 4
GLA_DK = HEAD_DIM // 2
GLA_GATE_RANK = 16
GLA_TAU = 16.0
GLA_SUB = 16
INV_BLK = 16
CHUNK = 64
PAGE_SIZE = 128
N_MOD = 9
EPS = 1e-6

LANES = 128
SUBLANES = 8
TM = 256
SUB = 32
V7X_VMEM_BYTES = 64 * 1024 * 1024
VMEM_BUDGET = 58 * 1024 * 1024


def _cparams(sem, need_bytes):
    limit = int(min(max(need_bytes * 1.25 + (4 << 20), 16 << 20), VMEM_BUDGET))
    return pltpu.CompilerParams(dimension_semantics=sem, vmem_limit_bytes=limit)


def _pick(n, prefs):
    for p in prefs:
        if n % p == 0:
            return p
    raise ValueError(f"no tile in {prefs} divides {n}")


def _bdot(a, b):
    return jnp.dot(a.astype(bf16), b.astype(bf16), preferred_element_type=f32)


def _bdot_nt(a, b):
    return lax.dot_general(a.astype(bf16), b.astype(bf16), (((1,), (1,)), ((), ())),
                           preferred_element_type=f32)


def _bdot_tn(a, b):
    return lax.dot_general(a.astype(bf16), b.astype(bf16), (((0,), (0,)), ((), ())),
                           preferred_element_type=f32)


def _split2(a):
    hi = a.astype(bf16)
    lo = (a - hi.astype(f32)).astype(bf16)
    return hi, lo


def _split3(a):
    hi = a.astype(bf16)
    r = a - hi.astype(f32)
    mid = r.astype(bf16)
    lo = (r - mid.astype(f32)).astype(bf16)
    return hi, mid, lo


def _dot3(a, b):
    ah, al = _split2(a)
    bh, bl = _split2(b)
    d = functools.partial(jnp.dot, preferred_element_type=f32)
    return d(ah, bh) + (d(ah, bl) + d(al, bh))


def _dot3_nt(a, b):
    ah, al = _split2(a)
    bh, bl = _split2(b)
    d = functools.partial(lax.dot_general, dimension_numbers=(((1,), (1,)), ((), ())),
                          preferred_element_type=f32)
    return d(ah, bh) + (d(ah, bl) + d(al, bh))


def _sel_dot(sel, b):
    bh, bm, bl = _split3(b)
    d = functools.partial(jnp.dot, preferred_element_type=f32)
    return d(sel, bh) + (d(sel, bm) + d(sel, bl))


def _sel_dot_nt(sel, b):
    bh, bm, bl = _split3(b)
    d = functools.partial(lax.dot_general, dimension_numbers=(((1,), (1,)), ((), ())),
                          preferred_element_type=f32)
    return d(sel, bh) + (d(sel, bm) + d(sel, bl))


def _silu(x):
    return x * jax.nn.sigmoid(x)


def _softplus(x):
    return jnp.maximum(x, 0.0) + jnp.log(1.0 + jnp.exp(-jnp.abs(x)))


def _iota(shape, dim):
    return lax.broadcasted_iota(jnp.int32, shape, dim)


def _rms_heads(o, g):
    return o * lax.rsqrt(jnp.mean(o * o, axis=-1, keepdims=True) + EPS) * g


def _mod_kernel(c_ref, w_ref, b_ref, o_ref):
    c = c_ref[...]
    o_ref[...] = _bdot(_silu(c), w_ref[...]) + b_ref[...]


def _mod_call(c16, ada_w, ada_b):
    depth, d, n = ada_w.shape
    tn = _pick(n, (512, 256, 128))
    need = 2 * d * tn * 4 + d * tn * 2 + 4 * 16 * tn * 4 + 2 * 16 * d * 4
    return pl.pallas_call(
        _mod_kernel,
        grid=(depth, n // tn),
        in_specs=[pl.BlockSpec((16, d), lambda l, j: (0, 0)),
                  pl.BlockSpec((None, d, tn), lambda l, j: (l, 0, j)),
                  pl.BlockSpec((None, 1, tn), lambda l, j: (l, 0, j))],
        out_specs=pl.BlockSpec((None, 16, tn), lambda l, j: (l, 0, j)),
        out_shape=jax.ShapeDtypeStruct((depth, 16, n), f32),
        compiler_params=_cparams(("arbitrary", "arbitrary"), need),
        name="adaln_mod",
    )(c16, ada_w, ada_b.reshape(depth, 1, n))


def _premod_kernel(x_ref, g_ref, sh_ref, sc_ref, o_ref):
    g = g_ref[...]
    for s in range(TM // SUB):
        rows = slice(s * SUB, (s + 1) * SUB)
        xs = x_ref[rows, :]
        y = xs * lax.rsqrt(jnp.mean(xs * xs, axis=-1, keepdims=True) + EPS) * g
        o_ref[rows, :] = (y * (1.0 + sc_ref[s:s + 1, :]) + sh_ref[s:s + 1, :]).astype(o_ref.dtype)


def _premod_call(x, g, mod8, v_shift, v_scale):
    m, d = x.shape
    nsub = TM // SUB
    need = 2 * TM * d * 4 + 2 * TM * d * 2 + 6 * nsub * d * 4
    return pl.pallas_call(
        _premod_kernel,
        grid=(m // TM,),
        in_specs=[pl.BlockSpec((TM, d), lambda i: (i, 0)),
                  pl.BlockSpec((1, d), lambda i: (0, 0)),
                  pl.BlockSpec((nsub, d), lambda i: (i, v_shift)),
                  pl.BlockSpec((nsub, d), lambda i: (i, v_scale))],
        out_specs=pl.BlockSpec((TM, d), lambda i: (i, 0)),
        out_shape=jax.ShapeDtypeStruct((m, d), bf16),
        compiler_params=_cparams(("arbitrary",), need),
        name="modulate",
    )(x, g.reshape(1, d), mod8, mod8)


def _pack_kernel(p_ref, s_ref, o_ref, *, n_p):
    i = pl.program_id(0)

    @pl.when(i < n_p)
    def _():
        o_ref[...] = p_ref[...]

    @pl.when(i >= n_p)
    def _():
        o_ref[...] = s_ref[...]


def _pack_call(p_rows, s_rows):
    bs, d = p_rows.shape
    n_p, n_s = bs // TM, s_rows.shape[0] // TM
    return pl.pallas_call(
        functools.partial(_pack_kernel, n_p=n_p),
        grid=(n_p + n_s,),
        in_specs=[pl.BlockSpec((TM, d), lambda i: (jnp.minimum(i, n_p - 1), 0)),
                  pl.BlockSpec((TM, d), lambda i: (jnp.maximum(i - n_p, 0), 0))],
        out_specs=pl.BlockSpec((TM, d), lambda i: (i, 0)),
        out_shape=jax.ShapeDtypeStruct((bs + s_rows.shape[0], d), p_rows.dtype),
        compiler_params=_cparams(("arbitrary",), 6 * TM * d * 4),
        name="pack_tokens",
    )(p_rows, s_rows)


def _rms_kernel(x_ref, g_ref, o_ref):
    x = x_ref[...]
    o_ref[...] = x * lax.rsqrt(jnp.mean(x * x, axis=-1, keepdims=True) + EPS) * g_ref[...]


def _rms_call(x, g, row0, nrows):
    d = x.shape[1]
    blk0 = row0 // TM
    return pl.pallas_call(
        _rms_kernel,
        grid=(nrows // TM,),
        in_specs=[pl.BlockSpec((TM, d), lambda i: (blk0 + i, 0)), pl.BlockSpec((1, d), lambda i: (0, 0))],
        out_specs=pl.BlockSpec((TM, d), lambda i: (i, 0)),
        out_shape=jax.ShapeDtypeStruct((nrows, d), f32),
        compiler_params=_cparams(("arbitrary",), 4 * TM * d * 4),
        name="final_norm",
    )(x, g.reshape(1, d))


MM_VMEM_TARGET = 48 << 20


V7X_BF16_FLOPS = 1.15e15
V7X_HBM_BYTES_PER_S = 3.0e12
STEP_OVERHEAD_S = 0.35e-6
MXU_EFF_BY_TM = {1408: 0.90, 768: 0.85, 512: 0.80, 256: 0.70}


def _ws_plan(m, k, n, n_weights, tile_bytes_per_out_elem, io_bytes_per_out_elem, tns=(1024, 512, 256, 128),
             tms=(768, 512, 256)):
    best = None
    for tn in tns:
        if n % tn:
            continue
        for tm in tms:
            if m % tm:
                continue
            for bufs in (2, 1):
                need = n_weights * k * tn * (4 * bufs + 2) + 2 * tm * k * 2 + tm * tn * tile_bytes_per_out_elem
                if need > MM_VMEM_TARGET:
                    continue
                w_bytes = n_weights * k * n * 4
                hbm = (n // tn) * m * k * 2 + w_bytes + m * n * io_bytes_per_out_elem
                t = max(2.0 * n_weights * m * k * n / V7X_BF16_FLOPS / MXU_EFF_BY_TM[tm], hbm / V7X_HBM_BYTES_PER_S)
                t += (n // tn) * (m // tm) * STEP_OVERHEAD_S
                if bufs == 1:
                    t += w_bytes / V7X_HBM_BYTES_PER_S
                if best is None or t < best[0]:
                    best = (t, tm, tn, bufs, need)
    if best is None:
        raise ValueError(f"no weight-stationary tiling for {(m, k, n)}")
    return best[1:]


def _wspec(shape, index_map, bufs):
    if bufs == 1:
        return pl.BlockSpec(shape, index_map, pipeline_mode=pl.Buffered(1))
    return pl.BlockSpec(shape, index_map)


def _mm_plain_t_kernel(a_ref, w_ref, o_ref, wb_ref):
    @pl.when(pl.program_id(1) == 0)
    def _():
        wb_ref[...] = w_ref[0].astype(bf16)

    o_ref[...] = lax.dot_general(a_ref[...], wb_ref[...], (((1,), (1,)), ((), ())),
                                 preferred_element_type=f32).astype(o_ref.dtype)


def _mm_plain_call(a, wt3, layer, row0, n_cols, name):
    m, k = a.shape
    assert row0 % SUBLANES == 0
    tm, tn, bufs, need = _ws_plan(m, k, n_cols, 1, 2 * 4 + 4, 4)
    wshape = (pl.Element(1), pl.Element(tn), pl.Element(k))
    return pl.pallas_call(
        _mm_plain_t_kernel,
        grid=(n_cols // tn, m // tm),
        in_specs=[pl.BlockSpec((tm, k), lambda j, i: (i, 0)),
                  _wspec(wshape, lambda j, i: (layer, pl.multiple_of(row0 + j * tn, SUBLANES), 0), bufs)],
        out_specs=pl.BlockSpec((tm, tn), lambda j, i: (i, j)),
        out_shape=jax.ShapeDtypeStruct((m, n_cols), f32),
        scratch_shapes=[pltpu.VMEM((tn, k), bf16)],
        compiler_params=_cparams(("arbitrary", "arbitrary"), need),
        name=name,
    )(a, wt3)


def _mm_up_kernel(a_ref, wg_ref, wu_ref, o_ref, wgb_ref, wub_ref):
    @pl.when(pl.program_id(1) == 0)
    def _():
        wgb_ref[...] = wg_ref[...].astype(bf16)
        wub_ref[...] = wu_ref[...].astype(bf16)

    a = a_ref[...]
    g = jnp.dot(a, wgb_ref[...], preferred_element_type=f32)
    u = jnp.dot(a, wub_ref[...], preferred_element_type=f32)
    o_ref[...] = (_silu(g) * u).astype(o_ref.dtype)


def _mm_up_call(a, wg3, wu3, layer):
    m, k = a.shape
    f = wg3.shape[2]
    tm, tn, bufs, need = _ws_plan(m, k, LANES * 8, 2, 2 * 2 + 3 * 4, 2, tns=(512, 256, 128),
                                  tms=(1408, 768, 512, 256))
    tn = min(tn, f)
    n_main = f // tn
    rem = f - n_main * tn
    assert rem % LANES == 0 and (rem == 0 or (n_main * tn) % rem == 0)

    def call(width, col_blk0, ncols, name):
        wspec = _wspec((None, k, width), lambda j, i: (layer, 0, col_blk0 + j), bufs)
        return pl.pallas_call(
            _mm_up_kernel,
            grid=(ncols, m // tm),
            in_specs=[pl.BlockSpec((tm, k), lambda j, i: (i, 0)), wspec, wspec],
            out_specs=pl.BlockSpec((tm, width), lambda j, i: (i, j)),
            out_shape=jax.ShapeDtypeStruct((m, ncols * width), bf16),
            scratch_shapes=[pltpu.VMEM((k, width), bf16), pltpu.VMEM((k, width), bf16)],
            compiler_params=_cparams(("arbitrary", "arbitrary"), need),
            name=name,
        )(a, wg3, wu3)

    outs = [call(tn, 0, n_main, "ffn_up")]
    if rem:
        outs.append(call(rem, (n_main * tn) // rem, 1, "ffn_up_tail"))
    return outs


def _mm_res_kernel(*refs, nseg, scale):
    a_refs, w_refs = refs[:nseg], refs[nseg:2 * nseg]
    r_ref, gate_ref, o_ref = refs[2 * nseg:2 * nseg + 3]
    wb_refs = refs[2 * nseg + 3:]

    @pl.when(pl.program_id(1) == 0)
    def _():
        for w_ref, wb_ref in zip(w_refs, wb_refs):
            wb_ref[...] = w_ref[...].astype(bf16)

    acc = jnp.dot(a_refs[0][...], wb_refs[0][...], preferred_element_type=f32)
    for a_ref, wb_ref in zip(a_refs[1:], wb_refs[1:]):
        acc = acc + jnp.dot(a_ref[...], wb_ref[...], preferred_element_type=f32)
    for s in range(acc.shape[0] // SUB):
        rows = slice(s * SUB, (s + 1) * SUB)
        o_ref[rows, :] = r_ref[rows, :] + (scale * gate_ref[s:s + 1, :]) * acc[rows, :]


def _mm_res_call(acts, w3, layer, res, mod8, v_gate, scale, name):
    m, k0 = acts[0].shape
    n = w3.shape[2]
    nk = 1
    while (k0 // nk) * 1024 * 6 > (36 << 20) and (k0 // nk) % (2 * LANES) == 0:
        nk *= 2
    kc = k0 // nk
    k_rest = sum(a.shape[1] for a in acts[1:])
    tm, tn, bufs, need = _ws_plan(m, kc + k_rest, n, 1, 4 * 4 + 2 * 4 + 4, 8 * nk)
    nsub = tm // SUB
    out = res
    for kb in range(nk):
        segs = [(acts[0], kc, kb, kb)]
        row0 = k0
        if kb == nk - 1:
            for a in acts[1:]:
                assert row0 % a.shape[1] == 0
                segs.append((a, a.shape[1], 0, row0 // a.shape[1]))
                row0 += a.shape[1]
        a_specs = [pl.BlockSpec((tm, w), lambda j, i, cb=cb: (i, cb)) for _, w, cb, _ in segs]
        w_specs = [_wspec((None, w, tn), lambda j, i, rb=rb: (layer, rb, j), bufs) for _, w, _, rb in segs]
        out = pl.pallas_call(
            functools.partial(_mm_res_kernel, nseg=len(segs), scale=scale),
            grid=(n // tn, m // tm),
            in_specs=a_specs + w_specs + [pl.BlockSpec((tm, tn), lambda j, i: (i, j)),
                                          pl.BlockSpec((nsub, tn), lambda j, i: (i, v_gate * (n // tn) + j))],
            out_specs=pl.BlockSpec((tm, tn), lambda j, i: (i, j)),
            out_shape=jax.ShapeDtypeStruct((m, n), f32),
            scratch_shapes=[pltpu.VMEM((w, tn), bf16) for _, w, _, _ in segs],
            compiler_params=_cparams(("arbitrary", "arbitrary"), need),
            name=f"{name}_k{kb}",
        )(*[s[0] for s in segs], *([w3] * len(segs)), out, mod8)
    return out


def _rope_kernel(x_ref, cos_ref, sin_ref, q_ref, k_ref):
    x = x_ref[...]
    w = x.shape[1]
    nh = w // HEAD_DIM
    cosf = jnp.concatenate([cos_ref[...]] * nh, axis=1)
    sinf = jnp.concatenate([sin_ref[...]] * nh, axis=1)
    lane = _iota(x.shape, 1) % HEAD_DIM
    half = ROPE_DIM // 2
    partner = jnp.where(lane < half, pltpu.roll(x, w - half, 1), pltpu.roll(x, half, 1))
    y = x * cosf + partner * sinf
    qw = q_ref.shape[1]
    q_ref[...] = y[:, :qw]
    k_ref[...] = y[:, qw:]


def _rope_call(proj, cos_t, sin_t, qw, kw):
    m = proj.shape[0]
    w = qw + kw
    return pl.pallas_call(
        _rope_kernel,
        grid=(m // TM,),
        in_specs=[pl.BlockSpec((TM, w), lambda i: (i, 0)),
                  pl.BlockSpec((TM, HEAD_DIM), lambda i: (i, 0)),
                  pl.BlockSpec((TM, HEAD_DIM), lambda i: (i, 0))],
        out_specs=[pl.BlockSpec((TM, qw), lambda i: (i, 0)), pl.BlockSpec((TM, kw), lambda i: (i, 0))],
        out_shape=[jax.ShapeDtypeStruct((m, qw), f32), jax.ShapeDtypeStruct((m, kw), f32)],
        compiler_params=_cparams(("arbitrary",), 10 * TM * w * 4),
        name="rope",
    )(proj, cos_t, sin_t)


def _top_blocks(gate, n_valid):
    lane = _iota(gate.shape, 1)
    gate = jnp.where(lane < n_valid, gate, -jnp.inf)
    picks = []
    for kk in range(MOBA_TOPK):
        mx = jnp.max(gate, axis=-1, keepdims=True)
        idx = jnp.min(jnp.where(gate == mx, lane, LANES), axis=-1, keepdims=True)
        picks.append(jnp.where(kk < n_valid, idx, -1))
        gate = jnp.where(lane == idx, -jnp.inf, gate)
    return picks


def _top_blocks_t(gate, n_valid):
    blk = _iota(gate.shape, 0)
    gate = jnp.where(blk < n_valid, gate, -jnp.inf)
    picks = []
    for kk in range(MOBA_TOPK):
        mx = jnp.max(gate, axis=0, keepdims=True)
        idx = jnp.min(jnp.where(gate == mx, blk, gate.shape[0]), axis=0, keepdims=True)
        picks.append(jnp.where(kk < n_valid, idx, -1))
        gate = jnp.where(blk == idx, -jnp.inf, gate)
    return picks


def _moba_prompt_kernel(q_ref, k_ref, v_ref, g_ref, o_ref, kmean_sc):
    i = pl.program_id(2)
    s_len = k_ref.shape[0]
    nb = s_len // MOBA_BLOCK
    scale = HEAD_DIM ** -0.5
    q2 = jnp.concatenate([q_ref[:, :HEAD_DIM], q_ref[:, HEAD_DIM:]], axis=0)
    rows = q2.shape[0]
    @pl.when(i == 0)
    def _():
        blk_row = _iota(kmean_sc.shape, 0)
        kmean = jnp.zeros(kmean_sc.shape, f32)
        for n in range(nb):
            mean_n = jnp.mean(k_ref[n * MOBA_BLOCK:(n + 1) * MOBA_BLOCK, :], axis=0, keepdims=True)
            kmean = jnp.where(blk_row == n, mean_n, kmean)
        kmean_sc[...] = kmean

    gate_t = _dot3_nt(kmean_sc[...], q2)
    picks_t = _top_blocks_t(gate_t, i)
    prow = _iota((SUBLANES, rows), 0)
    pk = jnp.zeros((SUBLANES, rows), f32)
    for kk in range(MOBA_TOPK):
        pk = jnp.where(prow == kk, picks_t[kk].astype(f32), pk)
    picks = [lax.dot_general(pk, (_iota((SUBLANES, LANES), 0) == kk).astype(f32), (((0,), (0,)), ((), ())),
                             preferred_element_type=f32) for kk in range(MOBA_TOPK)]
    qs = (q2 * scale).astype(bf16)

    own = pl.multiple_of(i * MOBA_BLOCK, MOBA_BLOCK)
    s = _bdot_nt(qs, k_ref[pl.ds(own, MOBA_BLOCK), :])
    rq = _iota(s.shape, 0) % MOBA_BLOCK
    ck = _iota(s.shape, 1)
    s = jnp.where(ck <= rq, s, -jnp.inf)
    m0 = jnp.max(s, axis=-1, keepdims=True)
    p = jnp.exp(s - m0)
    l0 = jnp.sum(p, axis=-1, keepdims=True)
    acc0 = _bdot(p, v_ref[pl.ds(own, MOBA_BLOCK), :])

    def masked_scores(n):
        start = pl.multiple_of(jnp.minimum(n, nb - 1) * MOBA_BLOCK, MOBA_BLOCK)
        sn = _bdot_nt(qs, k_ref[pl.ds(start, MOBA_BLOCK), :])
        nf = n.astype(f32)
        sel = (picks[0] == nf) | (picks[1] == nf) | (picks[2] == nf)
        sn = jnp.concatenate([jnp.where(sel, sn[:, t * LANES:(t + 1) * LANES], -jnp.inf)
                              for t in range(MOBA_BLOCK // LANES)], axis=1)
        return sn, start

    def body(t, carry):
        m, l, acc = carry
        sa, start_a = masked_scores(2 * t)
        sb, start_b = masked_scores(2 * t + 1)
        m_new = jnp.maximum(m, jnp.maximum(jnp.max(sa, axis=-1, keepdims=True), jnp.max(sb, axis=-1, keepdims=True)))
        alpha = jnp.exp(m - m_new)
        pa = jnp.exp(sa - m_new)
        pb = jnp.exp(sb - m_new)
        l_new = alpha * l + (jnp.sum(pa, axis=-1, keepdims=True) + jnp.sum(pb, axis=-1, keepdims=True))
        acc_new = alpha * acc + (_bdot(pa, v_ref[pl.ds(start_a, MOBA_BLOCK), :])
                                 + _bdot(pb, v_ref[pl.ds(start_b, MOBA_BLOCK), :]))
        return m_new, l_new, acc_new

    m, l, acc = lax.fori_loop(0, (i + 1) // 2, body, (m0, l0, acc0))
    o = _rms_heads(acc / l, g_ref[...])
    half = rows // 2
    o_ref[:, :HEAD_DIM] = o[:half].astype(o_ref.dtype)
    o_ref[:, HEAD_DIM:] = o[half:].astype(o_ref.dtype)


def _moba_prompt_call(q_rot, k_rot, proj, gnorm, b, s_len, kvh, v_col0):
    qt = s_len // MOBA_BLOCK
    vb = v_col0 // HEAD_DIM
    need = 4 * s_len * HEAD_DIM * 4 + 4 * MOBA_BLOCK * 2 * HEAD_DIM * 4 + 16 * 2 * MOBA_BLOCK * MOBA_BLOCK * 4
    return pl.pallas_call(
        _moba_prompt_kernel,
        grid=(b, kvh, qt),
        in_specs=[pl.BlockSpec((MOBA_BLOCK, 2 * HEAD_DIM), lambda bi, h, i: (bi * qt + i, h)),
                  pl.BlockSpec((s_len, HEAD_DIM), lambda bi, h, i: (bi, h)),
                  pl.BlockSpec((s_len, HEAD_DIM), lambda bi, h, i: (bi, vb + h)),
                  pl.BlockSpec((1, HEAD_DIM), lambda bi, h, i: (0, 0))],
        out_specs=pl.BlockSpec((MOBA_BLOCK, 2 * HEAD_DIM), lambda bi, h, i: (bi * qt + i, h)),
        out_shape=jax.ShapeDtypeStruct((b * s_len, 2 * kvh * HEAD_DIM), bf16),
        scratch_shapes=[pltpu.VMEM((-(-qt // SUBLANES) * SUBLANES, HEAD_DIM), f32)],
        compiler_params=_cparams(("arbitrary", "arbitrary", "arbitrary"), need),
        name="moba_prompt",
    )(q_rot, k_rot, proj, gnorm.reshape(1, HEAD_DIM))


QROWS = 8


def _moba_sample_kernel(pt_ref, q_ref, kn_ref, vn_ref, *refs, nb, bps, kvh, t_len):
    del pt_ref
    npg = bps * (MOBA_BLOCK // PAGE_SIZE)
    k_pages, v_pages = refs[:npg], refs[npg:2 * npg]
    g_ref, o_ref, qs_sc, s_sc, gate_sc, idx_sc, m_sc, l_sc, acc_sc = refs[2 * npg:]
    j = pl.program_id(1)
    nsk = nb // bps
    scale = HEAD_DIM ** -0.5
    r2 = 2 * QROWS

    def block_of(pages, bb, h):
        ppb = MOBA_BLOCK // PAGE_SIZE
        return jnp.concatenate([pages[bb * ppb + t][pl.ds(h, PAGE_SIZE, stride=kvh), :] for t in range(ppb)], axis=0)

    @pl.when(j == 0)
    def _():
        for h in range(kvh):
            q2 = jnp.concatenate([q_ref[0:QROWS, (2 * h) * HEAD_DIM:(2 * h + 1) * HEAD_DIM],
                                  q_ref[0:QROWS, (2 * h + 1) * HEAD_DIM:(2 * h + 2) * HEAD_DIM]], axis=0)
            qs_sc[h] = q2
        gate_sc[...] = jnp.zeros_like(gate_sc)

    @pl.when(j < nsk)
    def _():
        lane = _iota((r2, LANES), 1)
        pairs = [(bb, h) for bb in range(bps) for h in range(kvh)]
        q2s = [qs_sc[h] for h in range(kvh)]
        khs = [block_of(k_pages, bb, h) for bb, h in pairs]
        scores = [_bdot_nt(q2s[h] * scale, kh) for (bb, h), kh in zip(pairs, khs)]
        kmeans = [jnp.sum(kh, axis=0, keepdims=True) * (1.0 / MOBA_BLOCK) for kh in khs]
        cols = [jnp.sum(q2s[h] * km, axis=-1, keepdims=True) for (bb, h), km in zip(pairs, kmeans)]
        for (bb, h), sc in zip(pairs, scores):
            s_sc[j * bps + bb, h] = sc
        for h in range(kvh):
            gate = gate_sc[h]
            for (bb, hh), col in zip(pairs, cols):
                if hh == h:
                    gate = jnp.where(lane == j * bps + bb, col, gate)
            gate_sc[h] = gate

    @pl.when(j == nsk - 1)
    def _():
        for h in range(kvh):
            picks = _top_blocks(gate_sc[h], nb)
            for kk in range(MOBA_TOPK):
                idx_sc[h * MOBA_TOPK + kk] = jnp.broadcast_to(picks[kk], (r2, LANES))
            kn = jnp.concatenate([kn_ref[0:QROWS, h * HEAD_DIM:(h + 1) * HEAD_DIM],
                                  jnp.zeros((LANES - QROWS, HEAD_DIM), f32)], axis=0)
            vn = jnp.concatenate([vn_ref[0:QROWS, h * HEAD_DIM:(h + 1) * HEAD_DIM],
                                  jnp.zeros((LANES - QROWS, HEAD_DIM), f32)], axis=0)
            s = _bdot_nt(qs_sc[h] * scale, kn)
            tq = _iota(s.shape, 0) % QROWS
            ck = _iota(s.shape, 1)
            s = jnp.where((ck <= tq) & (ck < t_len), s, -jnp.inf)
            m0 = jnp.max(s, axis=-1, keepdims=True)
            p = jnp.exp(s - m0)
            m_sc[h] = jnp.broadcast_to(m0, (r2, LANES))
            l_sc[h] = jnp.broadcast_to(jnp.sum(p, axis=-1, keepdims=True), (r2, LANES))
            acc_sc[h] = _bdot(p, vn)

    @pl.when(j >= nsk)
    def _():
        heads_ = range(kvh)
        picks = [[idx_sc[h * MOBA_TOPK + kk][:, 0:1] for kk in range(MOBA_TOPK)] for h in heads_]
        sns = []
        for h in heads_:
            row = []
            for bb in range(bps):
                n = (j - nsk) * bps + bb
                sel = (picks[h][0] == n) | (picks[h][1] == n) | (picks[h][2] == n)
                row.append(jnp.where(sel, s_sc[n, h], -jnp.inf))
            sns.append(row)
        ms = [m_sc[h][:, 0:1] for h in heads_]
        m_news = []
        for h in heads_:
            mx = ms[h]
            for sn in sns[h]:
                mx = jnp.maximum(mx, jnp.max(sn, axis=-1, keepdims=True))
            m_news.append(mx)
        alphas = [jnp.exp(ms[h] - m_news[h]) for h in heads_]
        pns = [[jnp.exp(sn - m_news[h]) for sn in sns[h]] for h in heads_]
        pvs = [[_bdot(pns[h][bb], block_of(v_pages, bb, h)) for bb in range(bps)] for h in heads_]
        for h in heads_:
            l = alphas[h] * l_sc[h][:, 0:1]
            acc = alphas[h] * acc_sc[h]
            for bb in range(bps):
                l = l + jnp.sum(pns[h][bb], axis=-1, keepdims=True)
                acc = acc + pvs[h][bb]
            m_sc[h] = jnp.broadcast_to(m_news[h], (r2, LANES))
            l_sc[h] = jnp.broadcast_to(l, (r2, LANES))
            acc_sc[h] = acc

    @pl.when(j == 2 * nsk - 1)
    def _():
        for h in range(kvh):
            o = _rms_heads(acc_sc[h] / l_sc[h][:, 0:1], g_ref[...])
            o_ref[:, (2 * h) * HEAD_DIM:(2 * h + 1) * HEAD_DIM] = o[:QROWS]
            o_ref[:, (2 * h + 1) * HEAD_DIM:(2 * h + 2) * HEAD_DIM] = o[QROWS:]


def _moba_sample_call(page_table, q_rot, k_rot, proj, cache_k, cache_v, gnorm, layer,
                      db, row0, kvh, v_col0, t_len):
    n_pages = page_table.shape[1]
    nb = n_pages * PAGE_SIZE // MOBA_BLOCK
    ppb = MOBA_BLOCK // PAGE_SIZE
    bps = _pick(nb, (8, 4, 2, 1))
    nsk = nb // bps
    npg = bps * ppb
    qw = 2 * kvh * HEAD_DIM
    kw = kvh * HEAD_DIM
    blk0 = row0 // SUB

    def kmap(t):
        return lambda b, j, pt: (layer, pt[b, npg * jnp.minimum(j, nsk - 1) + t], 0, 0)

    def vmap_(t):
        return lambda b, j, pt: (layer, pt[b, npg * jnp.maximum(j - nsk, 0) + t], 0, 0)

    depth, n_pool = cache_k.shape[:2]
    cache_k = cache_k.reshape(depth, n_pool, PAGE_SIZE * kvh, HEAD_DIM)
    cache_v = cache_v.reshape(depth, n_pool, PAGE_SIZE * kvh, HEAD_DIM)
    page_spec = lambda fn: pl.BlockSpec((None, None, PAGE_SIZE * kvh, HEAD_DIM), fn)
    r2 = 2 * QROWS
    need = 4 * npg * PAGE_SIZE * kvh * HEAD_DIM * 4 + nb * kvh * r2 * MOBA_BLOCK * 4 + (8 << 20)
    grid_spec = pltpu.PrefetchScalarGridSpec(
        num_scalar_prefetch=1,
        grid=(db, 2 * nsk),
        in_specs=[pl.BlockSpec((SUB, qw), lambda b, j, pt: (blk0 + b, 0)),
                  pl.BlockSpec((SUB, kw), lambda b, j, pt: (blk0 + b, 0)),
                  pl.BlockSpec((SUB, kw), lambda b, j, pt: (blk0 + b, v_col0 // kw))]
                 + [page_spec(kmap(t)) for t in range(npg)] + [page_spec(vmap_(t)) for t in range(npg)]
                 + [pl.BlockSpec((1, HEAD_DIM), lambda b, j, pt: (0, 0))],
        out_specs=pl.BlockSpec((QROWS, qw), lambda b, j, pt: (b, 0)),
        scratch_shapes=[pltpu.VMEM((kvh, r2, HEAD_DIM), f32),
                        pltpu.VMEM((nb, kvh, r2, MOBA_BLOCK), f32),
                        pltpu.VMEM((kvh, r2, LANES), f32),
                        pltpu.VMEM((kvh * MOBA_TOPK, r2, LANES), jnp.int32),
                        pltpu.VMEM((kvh, r2, LANES), f32),
                        pltpu.VMEM((kvh, r2, LANES), f32),
                        pltpu.VMEM((kvh, r2, HEAD_DIM), f32)])
    return pl.pallas_call(
        functools.partial(_moba_sample_kernel, nb=nb, bps=bps, kvh=kvh, t_len=t_len),
        grid_spec=grid_spec,
        out_shape=jax.ShapeDtypeStruct((db * QROWS, qw), f32),
        compiler_params=_cparams(("arbitrary", "arbitrary"), need),
        name="moba_sample",
    )(page_table, q_rot, k_rot, proj, *([cache_k] * npg), *([cache_v] * npg), gnorm.reshape(1, HEAD_DIM))


def _inv_unit_lower(lmats, nblk):
    c = lmats[0].shape[0]
    ii = _iota((c, c), 0)
    jj = _iota((c, c), 1)
    eye = (ii == jj).astype(f32)
    same = (ii // INV_BLK) == (jj // INV_BLK)
    dmats = [jnp.where(same, l, 0.0) for l in lmats]
    ps = [-d for d in dmats]
    xs = [eye + p for p in ps]
    k = 2
    while k < INV_BLK:
        ps = [_bdot(p, p) for p in ps]
        xs = [x + _bdot(x, p) for x, p in zip(xs, ps)]
        k *= 2
    if nblk > 1:
        mms = [_bdot(x, l - d) for x, l, d in zip(xs, lmats, dmats)]
        ys = [eye - mm for mm in mms]
        pms = mms
        k = 2
        while k < nblk:
            pms = [_bdot(pm, pm) for pm in pms]
            ys = [y + _bdot(y, pm) for y, pm in zip(ys, pms)]
            k *= 2
        xs = [_bdot(y, x) for y, x in zip(ys, xs)]
    for _ in range(2):
        rs = [eye - _dot3(eye + l, x) for l, x in zip(lmats, xs)]
        xs = [x + _bdot(x, r) for x, r in zip(xs, rs)]
    return xs


def _gdn_kernel(main_ref, ab_ref, cw_ref, prev_ref, alog_ref, dtb_ref, s0_ref, g_ref, o_ref, s_ref, carry_sc,
                *, heads, group, col0, t_valid):
    c_idx = pl.program_id(1)
    c = main_ref.shape[0]
    gw = heads * HEAD_DIM
    ch = 3 * gw

    @pl.when(c_idx == 0)
    def _():
        s_ref[...] = s0_ref[...]
        carry_sc[0:SUBLANES, :] = prev_ref[...]

    x = main_ref[:, col0:col0 + ch]
    carry_sc[SUBLANES:, :] = x
    y = cw_ref[GDN_CONV - 1:GDN_CONV, :] * x
    for tap in range(1, GDN_CONV):
        y = y + cw_ref[GDN_CONV - 1 - tap:GDN_CONV - tap, :] * carry_sc[SUBLANES - tap:SUBLANES - tap + c, :]
    carry_sc[0:SUBLANES, :] = x[c - SUBLANES:, :]
    y = _silu(y)

    ab = ab_ref[...]
    lane = _iota(ab.shape, 1)
    row_ok = (c_idx * c + _iota((c, 1), 0)) < t_valid
    g_all = jnp.where(row_ok & (lane < heads), -jnp.exp(alog_ref[...]) * _softplus(ab + dtb_ref[...]), 0.0)
    beta_all = jnp.where(row_ok, jax.nn.sigmoid(ab), 0.0)
    tri = (_iota((c, c), 0) >= _iota((c, c), 1)).astype(bf16)
    gam_all = _sel_dot(tri, g_all)
    eye_l = (_iota((LANES, LANES), 0) == _iota((LANES, LANES), 1)).astype(bf16)
    gam_t = _sel_dot_nt(eye_l, gam_all)
    gnorm = g_ref[...]

    r = group * c
    ii = _iota((r, r), 0)
    jj = _iota((r, r), 1)
    same = (ii // c) == (jj // c)
    low = same & (ii >= jj)
    strict = same & (ii > jj)
    groups = [range(g * group, (g + 1) * group) for g in range(heads // group)]
    stack = lambda hs, f: jnp.concatenate([f(h) for h in hs], axis=0)
    qs, ks, vs, betas, gcols, decays, egams, kds = [], [], [], [], [], [], [], []
    for hs in groups:
        xq = stack(hs, lambda h: y[:, h * HEAD_DIM:(h + 1) * HEAD_DIM])
        xk = stack(hs, lambda h: y[:, gw + h * HEAD_DIM:gw + (h + 1) * HEAD_DIM])
        vs.append(stack(hs, lambda h: y[:, 2 * gw + h * HEAD_DIM:2 * gw + (h + 1) * HEAD_DIM]))
        qs.append(xq * lax.rsqrt(jnp.sum(xq * xq, axis=-1, keepdims=True) + EPS) * (HEAD_DIM ** -0.5))
        ks.append(xk * lax.rsqrt(jnp.sum(xk * xk, axis=-1, keepdims=True) + EPS))
        betas.append(stack(hs, lambda h: beta_all[:, heads + h:heads + h + 1]))
        gcol = stack(hs, lambda h: gam_all[:, h:h + 1])
        glast = stack(hs, lambda h: jnp.broadcast_to(gam_all[c - 1:c, h:h + 1], (c, 1)))
        grow = jnp.concatenate([gam_t[h:h + 1, :] for h in hs], axis=1)
        gcols.append(gcol)
        egams.append(jnp.exp(gcol))
        decays.append(jnp.exp(jnp.where(low, gcol - grow, -jnp.inf)))
        kds.append(ks[-1] * jnp.exp(glast - gcol))
    kks = [_bdot_nt(k, k) for k in ks]
    lmats = [jnp.where(strict, beta * kk * decay, 0.0) for beta, kk, decay in zip(betas, kks, decays)]
    tmats = _inv_unit_lower(lmats, c // INV_BLK)
    us = [_bdot(t, beta * v) for t, beta, v in zip(tmats, betas, vs)]
    ws = [_bdot(t, (beta * egam) * k) for t, beta, egam, k in zip(tmats, betas, egams, ks)]
    qks = [_bdot_nt(q, k) * decay for q, k, decay in zip(qs, ks, decays)]
    qes = [q * egam for q, egam in zip(qs, egams)]
    rows_of = lambda a: slice(a * c, (a + 1) * c)
    states = [[s_ref[h] for h in hs] for hs in groups]
    v_news = [jnp.concatenate([u[rows_of(a), :] - _bdot(w[rows_of(a), :], st[a]) for a in range(group)], axis=0)
              for u, w, st in zip(us, ws, states)]
    o_inters = [jnp.concatenate([_bdot(qe[rows_of(a), :], st[a]) for a in range(group)], axis=0)
                for qe, st in zip(qes, states)]
    os_ = [oi + _bdot(qk, vn) for oi, qk, vn in zip(o_inters, qks, v_news)]
    for hs, st, kd, vn, o in zip(groups, states, kds, v_news, os_):
        for a, h in enumerate(hs):
            s_ref[h] = jnp.exp(gam_all[c - 1:c, h:h + 1]) * st[a] + _bdot_tn(kd[rows_of(a), :], vn[rows_of(a), :])
            z = main_ref[:, col0 + ch + h * HEAD_DIM:col0 + ch + (h + 1) * HEAD_DIM]
            o_ref[:, h * HEAD_DIM:(h + 1) * HEAD_DIM] = (_rms_heads(o[rows_of(a), :], gnorm) * _silu(z)).astype(o_ref.dtype)


def _gdn_call(proj, small, cw, prev8, alog, dtb, s0, gnorm, layer, *, nseq, rows_per_seq, row0, chunk, heads,
              col0, t_valid, name):
    mainw = proj.shape[1]
    gw = heads * HEAD_DIM
    ch = 3 * gw
    nch = rows_per_seq // chunk
    blk0 = row0 // chunk
    need = 2 * chunk * mainw * 4 + 4 * heads * HEAD_DIM * HEAD_DIM * 4 + 40 * chunk * ch * 4 + (4 << 20)
    return pl.pallas_call(
        functools.partial(_gdn_kernel, heads=heads, group=1, col0=col0, t_valid=t_valid),
        grid=(nseq, nch),
        in_specs=[pl.BlockSpec((chunk, mainw), lambda b, c: (blk0 + b * nch + c, 0)),
                  pl.BlockSpec((chunk, LANES), lambda b, c: (blk0 + b * nch + c, 0)),
                  pl.BlockSpec((None, GDN_CONV, ch), lambda b, c: (layer, 0, 0)),
                  pl.BlockSpec((None, SUBLANES, ch), lambda b, c: (b, 0, 0)),
                  pl.BlockSpec((1, LANES), lambda b, c: (0, 0)),
                  pl.BlockSpec((1, LANES), lambda b, c: (0, 0)),
                  pl.BlockSpec((None, heads, HEAD_DIM, HEAD_DIM), lambda b, c: (b, 0, 0, 0)),
                  pl.BlockSpec((1, HEAD_DIM), lambda b, c: (0, 0))],
        out_specs=[pl.BlockSpec((chunk, gw), lambda b, c: (b * nch + c, 0)),
                   pl.BlockSpec((None, heads, HEAD_DIM, HEAD_DIM), lambda b, c: (b, 0, 0, 0))],
        out_shape=[jax.ShapeDtypeStruct((nseq * rows_per_seq, gw), bf16),
                   jax.ShapeDtypeStruct((nseq, heads, HEAD_DIM, HEAD_DIM), f32)],
        scratch_shapes=[pltpu.VMEM((SUBLANES + chunk, ch), f32)],
        compiler_params=_cparams(("arbitrary", "arbitrary"), need),
        name=name,
    )(proj, small, cw, prev8, alog, dtb, s0, gnorm.reshape(1, HEAD_DIM))


def _gla_kernel(tail_ref, small_ref, wg_ref, bg_ref, s0_ref, g_ref, o_ref, s_ref, *, heads, t_valid):
    c_idx = pl.program_id(1)
    c = tail_ref.shape[0]
    kw = heads * GLA_DK
    vw = heads * HEAD_DIM

    @pl.when(c_idx == 0)
    def _():
        s_ref[...] = s0_ref[...]

    row_ok = (c_idx * c + _iota((c, 1), 0)) < t_valid
    fblk = small_ref[...]
    pre = _dot3(fblk, wg_ref[...]) + bg_ref[...]
    log_a = jnp.where(row_ok, -_softplus(-pre) * (1.0 / GLA_TAU), 0.0)
    tri = (_iota((c, c), 0) >= _iota((c, c), 1)).astype(bf16)
    bc_all = _sel_dot(tri, log_a)
    gnorm = g_ref[...]
    lane = _iota((c, LANES), 1)
    first = lane < GLA_DK
    ii = _iota((c, LANES), 0)
    jj = lane % GLA_DK
    rowid = _iota((c, 1), 0)
    eye_l = _iota((LANES, LANES), 0) == _iota((LANES, LANES), 1)
    seg = (((_iota((2 * LANES, LANES), 0) % LANES) < GLA_DK) == (_iota((2 * LANES, LANES), 1) < GLA_DK)).astype(bf16)
    zpad = jnp.zeros((GLA_DK - c, LANES), f32) if c < GLA_DK else None

    def stack_pair(a):
        a0 = jnp.where(first, a, 0.0)
        a1 = jnp.where(first, 0.0, a)
        parts = [a0, a1] if zpad is None else [a0, zpad, a1, zpad]
        return jnp.concatenate(parts, axis=0)

    pairs = range(heads // 2)
    qs = [tail_ref[:, p * LANES:(p + 1) * LANES] * (GLA_DK ** -0.5) for p in pairs]
    ks = [jnp.where(row_ok, tail_ref[:, kw + p * LANES:kw + (p + 1) * LANES], 0.0) for p in pairs]
    bcs = [bc_all[:, p * LANES:(p + 1) * LANES] for p in pairs]

    def v_stack(p):
        v0 = tail_ref[:, 2 * kw + (2 * p) * HEAD_DIM:2 * kw + (2 * p + 1) * HEAD_DIM]
        v1 = tail_ref[:, 2 * kw + (2 * p + 1) * HEAD_DIM:2 * kw + (2 * p + 2) * HEAD_DIM]
        vparts = [v0, v1] if zpad is None else [v0, zpad, v1, zpad]
        return jnp.concatenate(vparts, axis=0).astype(bf16)

    v2s = [v_stack(p) for p in pairs]
    states = [s_ref[p] for p in pairs]
    qes = [q * jnp.exp(bc) for q, bc in zip(qs, bcs)]
    o0s = [_bdot(jnp.where(first, qe, 0.0), s) for qe, s in zip(qes, states)]
    o1s = [_bdot(jnp.where(first, 0.0, qe), s) for qe, s in zip(qes, states)]

    pieces = [[jnp.zeros((GLA_SUB, LANES), f32)] for _ in pairs]
    for sb in range(1, c // GLA_SUB):
        r0 = sb * GLA_SUB
        rs = slice(r0, r0 + GLA_SUB)
        q_is = [q[rs, :] * jnp.exp(bc[rs, :] - bc[r0:r0 + 1, :]) for q, bc in zip(qs, bcs)]
        k_js = [k * jnp.exp(jnp.where(rowid < r0, bc[r0:r0 + 1, :] - bc, -jnp.inf)) for k, bc in zip(ks, bcs)]
        for p in pairs:
            pieces[p].append(_bdot_nt(q_is[p], stack_pair(k_js[p])))
    att_offs = [jnp.concatenate(pc, axis=0) for pc in pieces]

    atts = [jnp.zeros((c, LANES), f32) for _ in pairs]
    for dlt in range(GLA_SUB):
        ok = (rowid % GLA_SUB) >= dlt
        prs = []
        for p in pairs:
            k_r = pltpu.roll(ks[p], dlt, 0) if dlt else ks[p]
            bc_r = pltpu.roll(bcs[p], dlt, 0) if dlt else bcs[p]
            prs.append(qs[p] * k_r * jnp.exp(jnp.where(ok, bcs[p] - bc_r, -jnp.inf)))
        splits = [_split2(pr) for pr in prs]
        sums = [jnp.dot(jnp.concatenate([hi, lo], axis=1), seg, preferred_element_type=f32) for hi, lo in splits]
        atts = [jnp.where(jj == ii - dlt, sm, att) for sm, att in zip(sums, atts)]
    atts = [att + off for att, off in zip(atts, att_offs)]

    o0s = [o0 + _bdot(jnp.where(first, att, 0.0), v2) for o0, att, v2 in zip(o0s, atts, v2s)]
    o1s = [o1 + _bdot(jnp.where(first, 0.0, att), v2) for o1, att, v2 in zip(o1s, atts, v2s)]
    bls = [bc[c - 1:c, :] for bc in bcs]
    ebl_cols = [jnp.sum(jnp.where(eye_l, jnp.exp(bl), 0.0), axis=-1, keepdims=True) for bl in bls]
    kds = [stack_pair(k * jnp.exp(bl - bc)) for k, bl, bc in zip(ks, bls, bcs)]
    for p in pairs:
        s_ref[p] = ebl_cols[p] * states[p] + _bdot_tn(kds[p], v2s[p])
        for a, o in ((0, o0s[p]), (1, o1s[p])):
            h = 2 * p + a
            r_ = tail_ref[:, 2 * kw + vw + h * HEAD_DIM:2 * kw + vw + (h + 1) * HEAD_DIM]
            o_ref[:, h * HEAD_DIM:(h + 1) * HEAD_DIM] = (_rms_heads(o, gnorm) * _silu(r_)).astype(o_ref.dtype)


def _gla_call(tail, small, wg_pad, bg, s0, gnorm, *, nseq, rows_per_seq, row0, chunk, heads, t_valid, name):
    tailw = tail.shape[1]
    kw = heads * GLA_DK
    vw = heads * HEAD_DIM
    nch = rows_per_seq // chunk
    blk0 = row0 // chunk
    assert heads % 2 == 0 and chunk <= GLA_DK and chunk % GLA_SUB == 0
    s0 = s0.reshape(nseq, heads // 2, 2 * GLA_DK, HEAD_DIM)
    need = 2 * chunk * tailw * 4 + 4 * heads * GLA_DK * HEAD_DIM * 4 + 2 * LANES * kw * 4 + 40 * chunk * tailw * 4
    o, s_fin = pl.pallas_call(
        functools.partial(_gla_kernel, heads=heads, t_valid=t_valid),
        grid=(nseq, nch),
        in_specs=[pl.BlockSpec((chunk, tailw), lambda b, c: (blk0 + b * nch + c, 0)),
                  pl.BlockSpec((chunk, LANES), lambda b, c: (blk0 + b * nch + c, 0)),
                  pl.BlockSpec((LANES, kw), lambda b, c: (0, 0)),
                  pl.BlockSpec((1, kw), lambda b, c: (0, 0)),
                  pl.BlockSpec((None, heads // 2, 2 * GLA_DK, HEAD_DIM), lambda b, c: (b, 0, 0, 0)),
                  pl.BlockSpec((1, HEAD_DIM), lambda b, c: (0, 0))],
        out_specs=[pl.BlockSpec((chunk, vw), lambda b, c: (b * nch + c, 0)),
                   pl.BlockSpec((None, heads // 2, 2 * GLA_DK, HEAD_DIM), lambda b, c: (b, 0, 0, 0))],
        out_shape=[jax.ShapeDtypeStruct((nseq * rows_per_seq, vw), bf16),
                   jax.ShapeDtypeStruct((nseq, heads // 2, 2 * GLA_DK, HEAD_DIM), f32)],
        compiler_params=_cparams(("arbitrary", "arbitrary"), need),
        name=name,
    )(tail, small, wg_pad, bg.reshape(1, kw), s0, gnorm.reshape(1, HEAD_DIM))
    return o, s_fin.reshape(nseq, heads, GLA_DK, HEAD_DIM)


def kernel(x_prompt, x_sample, c_prompt, c_sample, cache_k, cache_v, page_table, state_gdn, state_gdn_conv, state_gla, ada_w, ada_b, norm_ffn_a, ffn_a_wg, ffn_a_wu, ffn_a_wd, norm_mix, w_in, moba_norm, gdn_conv_w, gdn_a_log, gdn_dt_bias, gdn_norm, gla_w_gate, gla_b_gate, gla_norm, w_out, norm_ffn_b, ffn_b_wg, ffn_b_wu, ffn_b_wd, final_norm):
    b, s_len, d = x_prompt.shape
    db, t_dec, _ = x_sample.shape
    depth = ada_w.shape[0]
    n_pages = page_table.shape[1]
    past_len = n_pages * PAGE_SIZE

    n_heads = d // HEAD_DIM
    moba_h = n_heads // 4
    kvh = moba_h // 2
    gdn_h = (3 * n_heads) // 8
    gla_h = n_heads - moba_h - gdn_h
    qw, kvw = moba_h * HEAD_DIM, kvh * HEAD_DIM
    gdn_w, gla_kw, gla_w = gdn_h * HEAD_DIM, gla_h * GLA_DK, gla_h * HEAD_DIM
    main_w = qw + 2 * kvw + 4 * gdn_w
    small0 = main_w
    gla0 = main_w + 2 * gdn_h
    f0 = gla0 + 2 * gla_kw + 2 * gla_w
    tail_main = 2 * gla_kw + 2 * gla_w
    assert w_in.shape[2] == f0 + GLA_GATE_RANK
    assert s_len % TM == 0 and s_len % CHUNK == 0 and s_len >= MOBA_TOPK * MOBA_BLOCK
    assert (db * SUB) % TM == 0 and b + db <= 16
    assert GDN_CONV - 1 <= t_dec <= QROWS and past_len % MOBA_BLOCK == 0
    assert 2 * gdn_h + GLA_GATE_RANK <= LANES and tail_main % LANES == 0

    bs = b * s_len
    m_tot = bs + db * SUB
    n_tiles_p = bs // TM

    def pack_rows(p_rows, s_rows):
        s_pad = jnp.pad(s_rows, ((0, 0), (0, SUB - s_rows.shape[1]), (0, 0)))
        return jnp.concatenate([p_rows, s_pad.reshape(db * SUB, s_rows.shape[2])], axis=0)

    def sample_rows(a):
        return a[bs:].reshape(db, SUB, a.shape[1])[:, :t_dec]

    x = _pack_call(x_prompt.reshape(bs, d),
                   jnp.pad(x_sample, ((0, 0), (0, SUB - t_dec), (0, 0))).reshape(db * SUB, d))

    c16 = jnp.concatenate([c_prompt, c_sample, jnp.zeros((16 - b - db, d), f32)], axis=0)
    mod = _mod_call(c16, ada_w, ada_b)

    half = ROPE_DIM // 2
    inv_freq = ROPE_THETA ** (-jnp.arange(half, dtype=f32) / half)
    pos_s = past_len + jnp.minimum(jnp.arange(SUB), t_dec - 1)
    pos = jnp.concatenate([jnp.tile(jnp.arange(s_len), b), jnp.tile(pos_s, db)]).astype(f32)
    ang = pos[:, None] * inv_freq[None, :]
    ones = jnp.ones((m_tot, HEAD_DIM - ROPE_DIM), f32)
    cos_t = jnp.concatenate([jnp.cos(ang), jnp.cos(ang), ones], axis=1)
    sin_t = jnp.concatenate([-jnp.sin(ang), jnp.sin(ang), 0.0 * ones], axis=1)

    w_in_t = jnp.swapaxes(w_in, 1, 2)
    w_small_t = jnp.concatenate([w_in_t[:, small0:gla0], w_in_t[:, f0:],
                                 jnp.zeros((depth, LANES - 2 * gdn_h - GLA_GATE_RANK, d), f32)], axis=1)
    zeros_prev = jnp.zeros((b, SUBLANES, 3 * gdn_w), f32)

    outs = {k: [] for k in ("kp", "vp", "ks", "vs", "gp", "gs", "cp", "cs", "lp", "ls")}
    for l in range(depth):
        mod8 = jnp.concatenate([jnp.repeat(mod[l, :b], (s_len // TM) * (TM // SUB), axis=0), mod[l, b:b + db]], axis=0)

        h = _premod_call(x, norm_ffn_a[l], mod8, 0, 1)
        a = _mm_up_call(h, ffn_a_wg, ffn_a_wu, l)
        x = _mm_res_call(a, ffn_a_wd, l, x, mod8, 2, 0.5, "ffn_a_down")

        h = _premod_call(x, norm_mix[l], mod8, 3, 4)
        proj = _mm_plain_call(h, w_in_t, l, 0, main_w, "w_in_main")
        tail = _mm_plain_call(h, w_in_t, l, gla0, tail_main, "w_in_tail")
        small = _mm_plain_call(h, w_small_t, l, 0, LANES, "w_in_small")

        q_rot, k_rot = _rope_call(proj, cos_t, sin_t, qw, kvw)
        v_col0 = qw + kvw
        om_p = _moba_prompt_call(q_rot, k_rot, proj, moba_norm[l], b, s_len, kvh, v_col0)
        om_s = _moba_sample_call(page_table, q_rot, k_rot, proj, cache_k, cache_v, moba_norm[l], l,
                                 db, bs, kvh, v_col0, t_dec)
        om_s = om_s.reshape(db, QROWS, qw)[:, :t_dec].astype(bf16)

        col0 = qw + 2 * kvw
        alog = jnp.zeros((1, LANES), f32).at[0, :gdn_h].set(gdn_a_log[l])
        dtb = jnp.zeros((1, LANES), f32).at[0, :gdn_h].set(gdn_dt_bias[l])
        od_p, gp = _gdn_call(proj, small, gdn_conv_w, zeros_prev, alog, dtb,
                             jnp.zeros((b, gdn_h, HEAD_DIM, HEAD_DIM), f32), gdn_norm[l], l,
                             nseq=b, rows_per_seq=s_len, row0=0, chunk=CHUNK, heads=gdn_h, col0=col0,
                             t_valid=s_len, name="gdn_prompt")
        prev_s = jnp.pad(state_gdn_conv[l], ((0, 0), (SUBLANES - (GDN_CONV - 1), 0), (0, 0)))
        od_s, gs = _gdn_call(proj, small, gdn_conv_w, prev_s, alog, dtb, state_gdn[l], gdn_norm[l], l,
                             nseq=db, rows_per_seq=SUB, row0=bs, chunk=SUB, heads=gdn_h, col0=col0,
                             t_valid=t_dec, name="gdn_sample")

        wg_pad = jnp.zeros((LANES, gla_kw), f32).at[2 * gdn_h:2 * gdn_h + GLA_GATE_RANK].set(gla_w_gate[l])
        ol_p, lp = _gla_call(tail, small, wg_pad, gla_b_gate[l], jnp.zeros((b, gla_h, GLA_DK, HEAD_DIM), f32),
                             gla_norm[l], nseq=b, rows_per_seq=s_len, row0=0, chunk=CHUNK, heads=gla_h,
                             t_valid=s_len, name="gla_prompt")
        ol_s, ls = _gla_call(tail, small, wg_pad, gla_b_gate[l], state_gla[l], gla_norm[l],
                             nseq=db, rows_per_seq=SUB, row0=bs, chunk=SUB, heads=gla_h,
                             t_valid=t_dec, name="gla_sample")

        o_p = jnp.concatenate([om_p, od_p, ol_p], axis=1)
        o_s = jnp.concatenate([om_s, od_s.reshape(db, SUB, gdn_w)[:, :t_dec], ol_s.reshape(db, SUB, gla_w)[:, :t_dec]], axis=2)
        o_mix = pack_rows(o_p, o_s)
        x = _mm_res_call([o_mix], w_out, l, x, mod8, 5, 1.0, "w_out")

        h = _premod_call(x, norm_ffn_b[l], mod8, 6, 7)
        a = _mm_up_call(h, ffn_b_wg, ffn_b_wu, l)
        x = _mm_res_call(a, ffn_b_wd, l, x, mod8, 8, 0.5, "ffn_b_down")

        conv_cols = slice(col0, col0 + 3 * gdn_w)
        proj_s = proj[bs:].reshape(db, SUB, main_w)
        outs["kp"].append(k_rot[:bs].reshape(b, s_len, kvh, HEAD_DIM))
        outs["vp"].append(proj[:bs, v_col0:v_col0 + kvw].reshape(b, s_len, kvh, HEAD_DIM))
        outs["ks"].append(sample_rows(k_rot).reshape(db, t_dec, kvh, HEAD_DIM))
        outs["vs"].append(proj_s[:, :t_dec, v_col0:v_col0 + kvw].reshape(db, t_dec, kvh, HEAD_DIM))
        outs["gp"].append(gp)
        outs["gs"].append(gs)
        outs["cp"].append(jnp.stack([proj[(i + 1) * s_len - (GDN_CONV - 1):(i + 1) * s_len, conv_cols] for i in range(b)]))
        outs["cs"].append(proj_s[:, t_dec - (GDN_CONV - 1):t_dec, conv_cols])
        outs["lp"].append(lp)
        outs["ls"].append(ls)

    y_p = _rms_call(x, final_norm, 0, bs)
    y_s = _rms_call(x, final_norm, bs, db * SUB)
    st = {k: jnp.stack(v) for k, v in outs.items()}
    return (y_p.reshape(b, s_len, d), y_s.reshape(db, SUB, d)[:, :t_dec],
            st["kp"], st["vp"], st["ks"], st["vs"], st["gp"], st["gs"], st["cp"], st["cs"], st["lp"], st["ls"])
```

```python
import functools
import math

import jax
import jax.numpy as jnp
from jax import lax
from jax.experimental import pallas as pl
from jax.experimental.pallas import tpu as pltpu

f32 = jnp.float32
bf16 = jnp.bfloat16

HEAD_DIM = 128
MOBA_BLOCK = 256
MOBA_TOPK = 3
ROPE_THETA = 500000.0
ROPE_DIM = HEAD_DIM // 4
GDN_CONV = 4
GLA_DK = HEAD_DIM // 2
GLA_GATE_RANK = 16
GLA_TAU = 16.0
GLA_SUB = 16
INV_BLK = 16
CHUNK = 64
PAGE_SIZE = 128
N_MOD = 9
EPS = 1e-6

LANES = 128
SUBLANES = 8
TM = 256
SUB = 32
V7X_VMEM_BYTES = 64 * 1024 * 1024
VMEM_BUDGET = 58 * 1024 * 1024


def _cparams(sem, need_bytes):
    limit = int(min(max(need_bytes * 1.25 + (4 << 20), 16 << 20), VMEM_BUDGET))
    return pltpu.CompilerParams(dimension_semantics=sem, vmem_limit_bytes=limit)


def _pick(n, prefs):
    for p in prefs:
        if n % p == 0:
            return p
    raise ValueError(f"no tile in {prefs} divides {n}")


def _bdot(a, b):
    return jnp.dot(a.astype(bf16), b.astype(bf16), preferred_element_type=f32)


def _bdot_nt(a, b):
    return lax.dot_general(a.astype(bf16), b.astype(bf16), (((1,), (1,)), ((), ())),
                           preferred_element_type=f32)


def _bdot_tn(a, b):
    return lax.dot_general(a.astype(bf16), b.astype(bf16), (((0,), (0,)), ((), ())),
                           preferred_element_type=f32)


def _split2(a):
    hi = a.astype(bf16)
    lo = (a - hi.astype(f32)).astype(bf16)
    return hi, lo


def _split3(a):
    hi = a.astype(bf16)
    r = a - hi.astype(f32)
    mid = r.astype(bf16)
    lo = (r - mid.astype(f32)).astype(bf16)
    return hi, mid, lo


def _dot3(a, b):
    ah, al = _split2(a)
    bh, bl = _split2(b)
    d = functools.partial(jnp.dot, preferred_element_type=f32)
    return d(ah, bh) + (d(ah, bl) + d(al, bh))


def _dot3_nt(a, b):
    ah, al = _split2(a)
    bh, bl = _split2(b)
    d = functools.partial(lax.dot_general, dimension_numbers=(((1,), (1,)), ((), ())),
                          preferred_element_type=f32)
    return d(ah, bh) + (d(ah, bl) + d(al, bh))


def _sel_dot(sel, b):
    bh, bm, bl = _split3(b)
    d = functools.partial(jnp.dot, preferred_element_type=f32)
    return d(sel, bh) + (d(sel, bm) + d(sel, bl))


def _sel_dot_nt(sel, b):
    bh, bm, bl = _split3(b)
    d = functools.partial(lax.dot_general, dimension_numbers=(((1,), (1,)), ((), ())),
                          preferred_element_type=f32)
    return d(sel, bh) + (d(sel, bm) + d(sel, bl))


def _silu(x):
    return x * jax.nn.sigmoid(x)


def _softplus(x):
    return jnp.maximum(x, 0.0) + jnp.log(1.0 + jnp.exp(-jnp.abs(x)))


def _iota(shape, dim):
    return lax.broadcasted_iota(jnp.int32, shape, dim)


def _rms_heads(o, g):
    return o * lax.rsqrt(jnp.mean(o * o, axis=-1, keepdims=True) + EPS) * g


def _mod_kernel(c_ref, w_ref, b_ref, o_ref):
    c = c_ref[...]
    o_ref[...] = _bdot(_silu(c), w_ref[...]) + b_ref[...]


def _mod_call(c16, ada_w, ada_b):
    depth, d, n = ada_w.shape
    tn = _pick(n, (512, 256, 128))
    need = 2 * d * tn * 4 + d * tn * 2 + 4 * 16 * tn * 4 + 2 * 16 * d * 4
    return pl.pallas_call(
        _mod_kernel,
        grid=(depth, n // tn),
        in_specs=[pl.BlockSpec((16, d), lambda l, j: (0, 0)),
                  pl.BlockSpec((None, d, tn), lambda l, j: (l, 0, j)),
                  pl.BlockSpec((None, 1, tn), lambda l, j: (l, 0, j))],
        out_specs=pl.BlockSpec((None, 16, tn), lambda l, j: (l, 0, j)),
        out_shape=jax.ShapeDtypeStruct((depth, 16, n), f32),
        compiler_params=_cparams(("arbitrary", "arbitrary"), need),
        name="adaln_mod",
    )(c16, ada_w, ada_b.reshape(depth, 1, n))


def _premod_kernel(x_ref, g_ref, sh_ref, sc_ref, o_ref):
    g = g_ref[...]
    for s in range(x_ref.shape[0] // SUB):
        rows = slice(s * SUB, (s + 1) * SUB)
        xs = x_ref[rows, :]
        y = xs * lax.rsqrt(jnp.mean(xs * xs, axis=-1, keepdims=True) + EPS) * g
        o_ref[rows, :] = (y * (1.0 + sc_ref[s:s + 1, :]) + sh_ref[s:s + 1, :]).astype(o_ref.dtype)


def _premod_call(x, g, mod8, v_shift, v_scale):
    m, d = x.shape
    tm = _pick(m, (3 * TM, TM))
    nsub = tm // SUB
    need = 2 * tm * d * 4 + 2 * tm * d * 2 + 6 * nsub * d * 4
    return pl.pallas_call(
        _premod_kernel,
        grid=(m // tm,),
        in_specs=[pl.BlockSpec((tm, d), lambda i: (i, 0)),
                  pl.BlockSpec((1, d), lambda i: (0, 0)),
                  pl.BlockSpec((nsub, d), lambda i: (i, v_shift)),
                  pl.BlockSpec((nsub, d), lambda i: (i, v_scale))],
        out_specs=pl.BlockSpec((tm, d), lambda i: (i, 0)),
        out_shape=jax.ShapeDtypeStruct((m, d), bf16),
        compiler_params=_cparams(("arbitrary",), need),
        name="modulate",
    )(x, g.reshape(1, d), mod8, mod8)


def _pack_kernel(p_ref, s_ref, o_ref, *, n_p):
    i = pl.program_id(0)

    @pl.when(i < n_p)
    def _():
        o_ref[...] = p_ref[...]

    @pl.when(i >= n_p)
    def _():
        o_ref[...] = s_ref[...]


def _pack_call(p_rows, s_rows):
    bs, d = p_rows.shape
    n_p, n_s = bs // TM, s_rows.shape[0] // TM
    return pl.pallas_call(
        functools.partial(_pack_kernel, n_p=n_p),
        grid=(n_p + n_s,),
        in_specs=[pl.BlockSpec((TM, d), lambda i: (jnp.minimum(i, n_p - 1), 0)),
                  pl.BlockSpec((TM, d), lambda i: (jnp.maximum(i - n_p, 0), 0))],
        out_specs=pl.BlockSpec((TM, d), lambda i: (i, 0)),
        out_shape=jax.ShapeDtypeStruct((bs + s_rows.shape[0], d), p_rows.dtype),
        compiler_params=_cparams(("arbitrary",), 6 * TM * d * 4),
        name="pack_tokens",
    )(p_rows, s_rows)


def _rms_kernel(x_ref, g_ref, o_ref):
    x = x_ref[...]
    o_ref[...] = x * lax.rsqrt(jnp.mean(x * x, axis=-1, keepdims=True) + EPS) * g_ref[...]


def _rms_call(x, g, row0, nrows):
    d = x.shape[1]
    blk0 = row0 // TM
    return pl.pallas_call(
        _rms_kernel,
        grid=(nrows // TM,),
        in_specs=[pl.BlockSpec((TM, d), lambda i: (blk0 + i, 0)), pl.BlockSpec((1, d), lambda i: (0, 0))],
        out_specs=pl.BlockSpec((TM, d), lambda i: (i, 0)),
        out_shape=jax.ShapeDtypeStruct((nrows, d), f32),
        compiler_params=_cparams(("arbitrary",), 4 * TM * d * 4),
        name="final_norm",
    )(x, g.reshape(1, d))


MM_VMEM_TARGET = 48 << 20


V7X_BF16_FLOPS = 1.15e15
V7X_HBM_BYTES_PER_S = 3.0e12
STEP_OVERHEAD_S = 0.35e-6
MXU_EFF_BY_TM = {1408: 0.90, 768: 0.85, 512: 0.80, 256: 0.70}


def _ws_plan(m, k, n, n_weights, tile_bytes_per_out_elem, io_bytes_per_out_elem, tns=(1024, 512, 256, 128),
             tms=(768, 512, 256)):
    best = None
    for tn in tns:
        if n % tn:
            continue
        for tm in tms:
            if m % tm:
                continue
            for bufs in (2, 1):
                need = n_weights * k * tn * (4 * bufs + 2) + 2 * tm * k * 2 + tm * tn * tile_bytes_per_out_elem
                if need > MM_VMEM_TARGET:
                    continue
                w_bytes = n_weights * k * n * 4
                hbm = (n // tn) * m * k * 2 + w_bytes + m * n * io_bytes_per_out_elem
                t = max(2.0 * n_weights * m * k * n / V7X_BF16_FLOPS / MXU_EFF_BY_TM[tm], hbm / V7X_HBM_BYTES_PER_S)
                t += (n // tn) * (m // tm) * STEP_OVERHEAD_S
                if bufs == 1:
                    t += w_bytes / V7X_HBM_BYTES_PER_S
                if best is None or t < best[0]:
                    best = (t, tm, tn, bufs, need)
    if best is None:
        raise ValueError(f"no weight-stationary tiling for {(m, k, n)}")
    return best[1:]


def _wspec(shape, index_map, bufs):
    if bufs == 1:
        return pl.BlockSpec(shape, index_map, pipeline_mode=pl.Buffered(1))
    return pl.BlockSpec(shape, index_map)


def _mm_plain_t_kernel(a_ref, w_ref, o_ref, wb_ref):
    @pl.when(pl.program_id(1) == 0)
    def _():
        wb_ref[...] = w_ref[0].astype(bf16)

    o_ref[...] = lax.dot_general(a_ref[...], wb_ref[...], (((1,), (1,)), ((), ())),
                                 preferred_element_type=f32).astype(o_ref.dtype)


def _mm_plain_call(a, wt3, layer, row0, n_cols, name):
    m, k = a.shape
    assert row0 % SUBLANES == 0
    tm, tn, bufs, need = _ws_plan(m, k, n_cols, 1, 2 * 4 + 4, 4)
    wshape = (pl.Element(1), pl.Element(tn), pl.Element(k))
    return pl.pallas_call(
        _mm_plain_t_kernel,
        grid=(n_cols // tn, m // tm),
        in_specs=[pl.BlockSpec((tm, k), lambda j, i: (i, 0)),
                  _wspec(wshape, lambda j, i: (layer, pl.multiple_of(row0 + j * tn, SUBLANES), 0), bufs)],
        out_specs=pl.BlockSpec((tm, tn), lambda j, i: (i, j)),
        out_shape=jax.ShapeDtypeStruct((m, n_cols), f32),
        scratch_shapes=[pltpu.VMEM((tn, k), bf16)],
        compiler_params=_cparams(("arbitrary", "arbitrary"), need),
        name=name,
    )(a, wt3)


def _mm_up_kernel(a_ref, wg_ref, wu_ref, o_ref, wgb_ref, wub_ref):
    @pl.when(pl.program_id(1) == 0)
    def _():
        wgb_ref[...] = wg_ref[...].astype(bf16)
        wub_ref[...] = wu_ref[...].astype(bf16)

    a = a_ref[...]
    g = jnp.dot(a, wgb_ref[...], preferred_element_type=f32)
    u = jnp.dot(a, wub_ref[...], preferred_element_type=f32)
    o_ref[...] = (_silu(g) * u).astype(o_ref.dtype)


def _mm_up_call(a, wg3, wu3, layer):
    m, k = a.shape
    f = wg3.shape[2]
    tm, tn, bufs, need = _ws_plan(m, k, LANES * 8, 2, 2 * 2 + 3 * 4, 2, tns=(512, 256, 128),
                                  tms=(1408, 768, 512, 256))
    tn = min(tn, f)
    n_main = f // tn
    rem = f - n_main * tn
    assert rem % LANES == 0 and (rem == 0 or (n_main * tn) % rem == 0)

    def call(width, col_blk0, ncols, name):
        wspec = _wspec((None, k, width), lambda j, i: (layer, 0, col_blk0 + j), bufs)
        return pl.pallas_call(
            _mm_up_kernel,
            grid=(ncols, m // tm),
            in_specs=[pl.BlockSpec((tm, k), lambda j, i: (i, 0)), wspec, wspec],
            out_specs=pl.BlockSpec((tm, width), lambda j, i: (i, j)),
            out_shape=jax.ShapeDtypeStruct((m, ncols * width), bf16),
            scratch_shapes=[pltpu.VMEM((k, width), bf16), pltpu.VMEM((k, width), bf16)],
            compiler_params=_cparams(("arbitrary", "arbitrary"), need),
            name=name,
        )(a, wg3, wu3)

    outs = [call(tn, 0, n_main, "ffn_up")]
    if rem:
        outs.append(call(rem, (n_main * tn) // rem, 1, "ffn_up_tail"))
    return outs


def _mm_res_kernel(*refs, nseg, scale):
    a_refs, w_refs = refs[:nseg], refs[nseg:2 * nseg]
    r_ref, gate_ref, o_ref = refs[2 * nseg:2 * nseg + 3]
    wb_refs = refs[2 * nseg + 3:]

    @pl.when(pl.program_id(1) == 0)
    def _():
        for w_ref, wb_ref in zip(w_refs, wb_refs):
            wb_ref[...] = w_ref[...].astype(bf16)

    acc = jnp.dot(a_refs[0][...], wb_refs[0][...], preferred_element_type=f32)
    for a_ref, wb_ref in zip(a_refs[1:], wb_refs[1:]):
        acc = acc + jnp.dot(a_ref[...], wb_ref[...], preferred_element_type=f32)
    for s in range(acc.shape[0] // SUB):
        rows = slice(s * SUB, (s + 1) * SUB)
        o_ref[rows, :] = r_ref[rows, :] + (scale * gate_ref[s:s + 1, :]) * acc[rows, :]


def _mm_res_call(acts, w3, layer, res, mod8, v_gate, scale, name):
    m, k0 = acts[0].shape
    n = w3.shape[2]
    nk = 1
    while (k0 // nk) * 1024 * 6 > (36 << 20) and (k0 // nk) % (2 * LANES) == 0:
        nk *= 2
    kc = k0 // nk
    k_rest = sum(a.shape[1] for a in acts[1:])
    tm, tn, bufs, need = _ws_plan(m, kc + k_rest, n, 1, 4 * 4 + 2 * 4 + 4, 8 * nk)
    nsub = tm // SUB
    out = res
    for kb in range(nk):
        segs = [(acts[0], kc, kb, kb)]
        row0 = k0
        if kb == nk - 1:
            for a in acts[1:]:
                assert row0 % a.shape[1] == 0
                segs.append((a, a.shape[1], 0, row0 // a.shape[1]))
                row0 += a.shape[1]
        a_specs = [pl.BlockSpec((tm, w), lambda j, i, cb=cb: (i, cb)) for _, w, cb, _ in segs]
        w_specs = [_wspec((None, w, tn), lambda j, i, rb=rb: (layer, rb, j), bufs) for _, w, _, rb in segs]
        out = pl.pallas_call(
            functools.partial(_mm_res_kernel, nseg=len(segs), scale=scale),
            grid=(n // tn, m // tm),
            in_specs=a_specs + w_specs + [pl.BlockSpec((tm, tn), lambda j, i: (i, j)),
                                          pl.BlockSpec((nsub, tn), lambda j, i: (i, v_gate * (n // tn) + j))],
            out_specs=pl.BlockSpec((tm, tn), lambda j, i: (i, j)),
            out_shape=jax.ShapeDtypeStruct((m, n), f32),
            scratch_shapes=[pltpu.VMEM((w, tn), bf16) for _, w, _, _ in segs],
            compiler_params=_cparams(("arbitrary", "arbitrary"), need),
            name=f"{name}_k{kb}",
        )(*[s[0] for s in segs], *([w3] * len(segs)), out, mod8)
    return out


def _rope_kernel(x_ref, cos_ref, sin_ref, q_ref, k_ref):
    x = x_ref[...]
    w = x.shape[1]
    nh = w // HEAD_DIM
    cosf = jnp.concatenate([cos_ref[...]] * nh, axis=1)
    sinf = jnp.concatenate([sin_ref[...]] * nh, axis=1)
    lane = _iota(x.shape, 1) % HEAD_DIM
    half = ROPE_DIM // 2
    partner = jnp.where(lane < half, pltpu.roll(x, w - half, 1), pltpu.roll(x, half, 1))
    y = x * cosf + partner * sinf
    qw = q_ref.shape[1]
    q_ref[...] = y[:, :qw]
    k_ref[...] = y[:, qw:]


def _rope_call(proj, cos_t, sin_t, qw, kw):
    m = proj.shape[0]
    w = qw + kw
    return pl.pallas_call(
        _rope_kernel,
        grid=(m // TM,),
        in_specs=[pl.BlockSpec((TM, w), lambda i: (i, 0)),
                  pl.BlockSpec((TM, HEAD_DIM), lambda i: (i, 0)),
                  pl.BlockSpec((TM, HEAD_DIM), lambda i: (i, 0))],
        out_specs=[pl.BlockSpec((TM, qw), lambda i: (i, 0)), pl.BlockSpec((TM, kw), lambda i: (i, 0))],
        out_shape=[jax.ShapeDtypeStruct((m, qw), f32), jax.ShapeDtypeStruct((m, kw), f32)],
        compiler_params=_cparams(("arbitrary",), 10 * TM * w * 4),
        name="rope",
    )(proj, cos_t, sin_t)


def _top_blocks(gate, n_valid):
    lane = _iota(gate.shape, 1)
    gate = jnp.where(lane < n_valid, gate, -jnp.inf)
    picks = []
    for kk in range(MOBA_TOPK):
        mx = jnp.max(gate, axis=-1, keepdims=True)
        idx = jnp.min(jnp.where(gate == mx, lane, LANES), axis=-1, keepdims=True)
        picks.append(jnp.where(kk < n_valid, idx, -1))
        gate = jnp.where(lane == idx, -jnp.inf, gate)
    return picks


def _top_blocks_t(gate, n_valid):
    blk = _iota(gate.shape, 0)
    gate = jnp.where(blk < n_valid, gate, -jnp.inf)
    picks = []
    for kk in range(MOBA_TOPK):
        mx = jnp.max(gate, axis=0, keepdims=True)
        idx = jnp.min(jnp.where(gate == mx, blk, gate.shape[0]), axis=0, keepdims=True)
        picks.append(jnp.where(kk < n_valid, idx, -1))
        gate = jnp.where(blk == idx, -jnp.inf, gate)
    return picks


def _moba_prompt_kernel(q_ref, k_ref, v_ref, g_ref, o_ref, kmean_sc):
    i = pl.program_id(2)
    s_len = k_ref.shape[0]
    nb = s_len // MOBA_BLOCK
    scale = HEAD_DIM ** -0.5
    q2 = jnp.concatenate([q_ref[:, :HEAD_DIM], q_ref[:, HEAD_DIM:]], axis=0)
    rows = q2.shape[0]
    @pl.when(i == 0)
    def _():
        blk_row = _iota(kmean_sc.shape, 0)
        kmean = jnp.zeros(kmean_sc.shape, f32)
        for n in range(nb):
            mean_n = jnp.mean(k_ref[n * MOBA_BLOCK:(n + 1) * MOBA_BLOCK, :], axis=0, keepdims=True)
            kmean = jnp.where(blk_row == n, mean_n, kmean)
        kmean_sc[...] = kmean

    gate_t = _dot3_nt(kmean_sc[...], q2)
    picks_t = _top_blocks_t(gate_t, i)
    prow = _iota((SUBLANES, rows), 0)
    pk = jnp.zeros((SUBLANES, rows), f32)
    for kk in range(MOBA_TOPK):
        pk = jnp.where(prow == kk, picks_t[kk].astype(f32), pk)
    picks = [lax.dot_general(pk, (_iota((SUBLANES, LANES), 0) == kk).astype(f32), (((0,), (0,)), ((), ())),
                             preferred_element_type=f32) for kk in range(MOBA_TOPK)]
    qs = (q2 * scale).astype(bf16)

    own = pl.multiple_of(i * MOBA_BLOCK, MOBA_BLOCK)
    s = _bdot_nt(qs, k_ref[pl.ds(own, MOBA_BLOCK), :])
    rq = _iota(s.shape, 0) % MOBA_BLOCK
    ck = _iota(s.shape, 1)
    s = jnp.where(ck <= rq, s, -jnp.inf)
    m0 = jnp.max(s, axis=-1, keepdims=True)
    p = jnp.exp(s - m0)
    l0 = jnp.sum(p, axis=-1, keepdims=True)
    acc0 = _bdot(p, v_ref[pl.ds(own, MOBA_BLOCK), :])

    def masked_scores(n):
        start = pl.multiple_of(jnp.minimum(n, nb - 1) * MOBA_BLOCK, MOBA_BLOCK)
        sn = _bdot_nt(qs, k_ref[pl.ds(start, MOBA_BLOCK), :])
        nf = n.astype(f32)
        sel = (picks[0] == nf) | (picks[1] == nf) | (picks[2] == nf)
        sn = jnp.concatenate([jnp.where(sel, sn[:, t * LANES:(t + 1) * LANES], -jnp.inf)
                              for t in range(MOBA_BLOCK // LANES)], axis=1)
        return sn, start

    def body(t, carry):
        m, l, acc = carry
        sa, start_a = masked_scores(2 * t)
        sb, start_b = masked_scores(2 * t + 1)
        m_new = jnp.maximum(m, jnp.maximum(jnp.max(sa, axis=-1, keepdims=True), jnp.max(sb, axis=-1, keepdims=True)))
        alpha = jnp.exp(m - m_new)
        pa = jnp.exp(sa - m_new)
        pb = jnp.exp(sb - m_new)
        l_new = alpha * l + (jnp.sum(pa, axis=-1, keepdims=True) + jnp.sum(pb, axis=-1, keepdims=True))
        acc_new = alpha * acc + (_bdot(pa, v_ref[pl.ds(start_a, MOBA_BLOCK), :])
                                 + _bdot(pb, v_ref[pl.ds(start_b, MOBA_BLOCK), :]))
        return m_new, l_new, acc_new

    m, l, acc = lax.fori_loop(0, (i + 1) // 2, body, (m0, l0, acc0))
    o = _rms_heads(acc / l, g_ref[...])
    half = rows // 2
    o_ref[:, :HEAD_DIM] = o[:half].astype(o_ref.dtype)
    o_ref[:, HEAD_DIM:] = o[half:].astype(o_ref.dtype)


def _moba_prompt_call(q_rot, k_rot, proj, gnorm, b, s_len, kvh, v_col0):
    qt = s_len // MOBA_BLOCK
    vb = v_col0 // HEAD_DIM
    need = 4 * s_len * HEAD_DIM * 4 + 4 * MOBA_BLOCK * 2 * HEAD_DIM * 4 + 16 * 2 * MOBA_BLOCK * MOBA_BLOCK * 4
    return pl.pallas_call(
        _moba_prompt_kernel,
        grid=(b, kvh, qt),
        in_specs=[pl.BlockSpec((MOBA_BLOCK, 2 * HEAD_DIM), lambda bi, h, i: (bi * qt + i, h)),
                  pl.BlockSpec((s_len, HEAD_DIM), lambda bi, h, i: (bi, h)),
                  pl.BlockSpec((s_len, HEAD_DIM), lambda bi, h, i: (bi, vb + h)),
                  pl.BlockSpec((1, HEAD_DIM), lambda bi, h, i: (0, 0))],
        out_specs=pl.BlockSpec((MOBA_BLOCK, 2 * HEAD_DIM), lambda bi, h, i: (bi * qt + i, h)),
        out_shape=jax.ShapeDtypeStruct((b * s_len, 2 * kvh * HEAD_DIM), bf16),
        scratch_shapes=[pltpu.VMEM((-(-qt // SUBLANES) * SUBLANES, HEAD_DIM), f32)],
        compiler_params=_cparams(("arbitrary", "arbitrary", "arbitrary"), need),
        name="moba_prompt",
    )(q_rot, k_rot, proj, gnorm.reshape(1, HEAD_DIM))


QROWS = 8


def _moba_sample_kernel(pt_ref, q_ref, kn_ref, vn_ref, *refs, nb, bps, kvh, t_len):
    del pt_ref
    npg = bps * (MOBA_BLOCK // PAGE_SIZE)
    k_pages, v_pages = refs[:npg], refs[npg:2 * npg]
    g_ref, o_ref, qs_sc, s_sc, gate_sc, idx_sc, m_sc, l_sc, acc_sc = refs[2 * npg:]
    j = pl.program_id(1)
    nsk = nb // bps
    scale = HEAD_DIM ** -0.5
    r2 = 2 * QROWS

    def block_of(pages, bb, h):
        ppb = MOBA_BLOCK // PAGE_SIZE
        return jnp.concatenate([pages[bb * ppb + t][pl.ds(h, PAGE_SIZE, stride=kvh), :] for t in range(ppb)], axis=0)

    @pl.when(j == 0)
    def _():
        for h in range(kvh):
            q2 = jnp.concatenate([q_ref[0:QROWS, (2 * h) * HEAD_DIM:(2 * h + 1) * HEAD_DIM],
                                  q_ref[0:QROWS, (2 * h + 1) * HEAD_DIM:(2 * h + 2) * HEAD_DIM]], axis=0)
            qs_sc[h] = q2
        gate_sc[...] = jnp.zeros_like(gate_sc)

    @pl.when(j < nsk)
    def _():
        lane = _iota((r2, LANES), 1)
        pairs = [(bb, h) for bb in range(bps) for h in range(kvh)]
        q2s = [qs_sc[h] for h in range(kvh)]
        khs = [block_of(k_pages, bb, h) for bb, h in pairs]
        scores = [_bdot_nt(q2s[h] * scale, kh) for (bb, h), kh in zip(pairs, khs)]
        kmeans = [jnp.sum(kh, axis=0, keepdims=True) * (1.0 / MOBA_BLOCK) for kh in khs]
        cols = [jnp.sum(q2s[h] * km, axis=-1, keepdims=True) for (bb, h), km in zip(pairs, kmeans)]
        for (bb, h), sc in zip(pairs, scores):
            s_sc[j * bps + bb, h] = sc
        for h in range(kvh):
            gate = gate_sc[h]
            for (bb, hh), col in zip(pairs, cols):
                if hh == h:
                    gate = jnp.where(lane == j * bps + bb, col, gate)
            gate_sc[h] = gate

    @pl.when(j == nsk - 1)
    def _():
        for h in range(kvh):
            picks = _top_blocks(gate_sc[h], nb)
            for kk in range(MOBA_TOPK):
                idx_sc[h * MOBA_TOPK + kk] = jnp.broadcast_to(picks[kk], (r2, LANES))
            kn = jnp.concatenate([kn_ref[0:QROWS, h * HEAD_DIM:(h + 1) * HEAD_DIM],
                                  jnp.zeros((LANES - QROWS, HEAD_DIM), f32)], axis=0)
            vn = jnp.concatenate([vn_ref[0:QROWS, h * HEAD_DIM:(h + 1) * HEAD_DIM],
                                  jnp.zeros((LANES - QROWS, HEAD_DIM), f32)], axis=0)
            s = _bdot_nt(qs_sc[h] * scale, kn)
            tq = _iota(s.shape, 0) % QROWS
            ck = _iota(s.shape, 1)
            s = jnp.where((ck <= tq) & (ck < t_len), s, -jnp.inf)
            m0 = jnp.max(s, axis=-1, keepdims=True)
            p = jnp.exp(s - m0)
            m_sc[h] = jnp.broadcast_to(m0, (r2, LANES))
            l_sc[h] = jnp.broadcast_to(jnp.sum(p, axis=-1, keepdims=True), (r2, LANES))
            acc_sc[h] = _bdot(p, vn)

    @pl.when(j >= nsk)
    def _():
        heads_ = range(kvh)
        picks = [[idx_sc[h * MOBA_TOPK + kk][:, 0:1] for kk in range(MOBA_TOPK)] for h in heads_]
        sns = []
        for h in heads_:
            row = []
            for bb in range(bps):
                n = (j - nsk) * bps + bb
                sel = (picks[h][0] == n) | (picks[h][1] == n) | (picks[h][2] == n)
                row.append(jnp.where(sel, s_sc[n, h], -jnp.inf))
            sns.append(row)
        ms = [m_sc[h][:, 0:1] for h in heads_]
        m_news = []
        for h in heads_:
            mx = ms[h]
            for sn in sns[h]:
                mx = jnp.maximum(mx, jnp.max(sn, axis=-1, keepdims=True))
            m_news.append(mx)
        alphas = [jnp.exp(ms[h] - m_news[h]) for h in heads_]
        pns = [[jnp.exp(sn - m_news[h]) for sn in sns[h]] for h in heads_]
        pvs = [[_bdot(pns[h][bb], block_of(v_pages, bb, h)) for bb in range(bps)] for h in heads_]
        for h in heads_:
            l = alphas[h] * l_sc[h][:, 0:1]
            acc = alphas[h] * acc_sc[h]
            for bb in range(bps):
                l = l + jnp.sum(pns[h][bb], axis=-1, keepdims=True)
                acc = acc + pvs[h][bb]
            m_sc[h] = jnp.broadcast_to(m_news[h], (r2, LANES))
            l_sc[h] = jnp.broadcast_to(l, (r2, LANES))
            acc_sc[h] = acc

    @pl.when(j == 2 * nsk - 1)
    def _():
        for h in range(kvh):
            o = _rms_heads(acc_sc[h] / l_sc[h][:, 0:1], g_ref[...])
            o_ref[:, (2 * h) * HEAD_DIM:(2 * h + 1) * HEAD_DIM] = o[:QROWS]
            o_ref[:, (2 * h + 1) * HEAD_DIM:(2 * h + 2) * HEAD_DIM] = o[QROWS:]


def _moba_sample_call(page_table, q_rot, k_rot, proj, cache_k, cache_v, gnorm, layer,
                      db, row0, kvh, v_col0, t_len):
    n_pages = page_table.shape[1]
    nb = n_pages * PAGE_SIZE // MOBA_BLOCK
    ppb = MOBA_BLOCK // PAGE_SIZE
    bps = _pick(nb, (16, 8, 4, 2, 1))
    nsk = nb // bps
    npg = bps * ppb
    qw = 2 * kvh * HEAD_DIM
    kw = kvh * HEAD_DIM
    blk0 = row0 // SUB

    def kmap(t):
        return lambda b, j, pt: (layer, pt[b, npg * jnp.minimum(j, nsk - 1) + t], 0, 0)

    def vmap_(t):
        return lambda b, j, pt: (layer, pt[b, npg * jnp.maximum(j - nsk, 0) + t], 0, 0)

    depth, n_pool = cache_k.shape[:2]
    cache_k = cache_k.reshape(depth, n_pool, PAGE_SIZE * kvh, HEAD_DIM)
    cache_v = cache_v.reshape(depth, n_pool, PAGE_SIZE * kvh, HEAD_DIM)
    page_spec = lambda fn: pl.BlockSpec((None, None, PAGE_SIZE * kvh, HEAD_DIM), fn)
    r2 = 2 * QROWS
    need = 4 * npg * PAGE_SIZE * kvh * HEAD_DIM * 4 + nb * kvh * r2 * MOBA_BLOCK * 4 + (8 << 20)
    grid_spec = pltpu.PrefetchScalarGridSpec(
        num_scalar_prefetch=1,
        grid=(db, 2 * nsk),
        in_specs=[pl.BlockSpec((SUB, qw), lambda b, j, pt: (blk0 + b, 0)),
                  pl.BlockSpec((SUB, kw), lambda b, j, pt: (blk0 + b, 0)),
                  pl.BlockSpec((SUB, kw), lambda b, j, pt: (blk0 + b, v_col0 // kw))]
                 + [page_spec(kmap(t)) for t in range(npg)] + [page_spec(vmap_(t)) for t in range(npg)]
                 + [pl.BlockSpec((1, HEAD_DIM), lambda b, j, pt: (0, 0))],
        out_specs=pl.BlockSpec((QROWS, qw), lambda b, j, pt: (b, 0)),
        scratch_shapes=[pltpu.VMEM((kvh, r2, HEAD_DIM), f32),
                        pltpu.VMEM((nb, kvh, r2, MOBA_BLOCK), f32),
                        pltpu.VMEM((kvh, r2, LANES), f32),
                        pltpu.VMEM((kvh * MOBA_TOPK, r2, LANES), jnp.int32),
                        pltpu.VMEM((kvh, r2, LANES), f32),
                        pltpu.VMEM((kvh, r2, LANES), f32),
                        pltpu.VMEM((kvh, r2, HEAD_DIM), f32)])
    return pl.pallas_call(
        functools.partial(_moba_sample_kernel, nb=nb, bps=bps, kvh=kvh, t_len=t_len),
        grid_spec=grid_spec,
        out_shape=jax.ShapeDtypeStruct((db * QROWS, qw), f32),
        compiler_params=_cparams(("arbitrary", "arbitrary"), need),
        name="moba_sample",
    )(page_table, q_rot, k_rot, proj, *([cache_k] * npg), *([cache_v] * npg), gnorm.reshape(1, HEAD_DIM))


def _inv_unit_lower(lmats, nblk):
    c = lmats[0].shape[0]
    ii = _iota((c, c), 0)
    jj = _iota((c, c), 1)
    eye = (ii == jj).astype(f32)
    same = (ii // INV_BLK) == (jj // INV_BLK)
    dmats = [jnp.where(same, l, 0.0) for l in lmats]
    ps = [-d for d in dmats]
    xs = [eye + p for p in ps]
    k = 2
    while k < INV_BLK:
        ps = [_bdot(p, p) for p in ps]
        xs = [x + _bdot(x, p) for x, p in zip(xs, ps)]
        k *= 2
    if nblk > 1:
        mms = [_bdot(x, l - d) for x, l, d in zip(xs, lmats, dmats)]
        ys = [eye - mm for mm in mms]
        pms = mms
        k = 2
        while k < nblk:
            pms = [_bdot(pm, pm) for pm in pms]
            ys = [y + _bdot(y, pm) for y, pm in zip(ys, pms)]
            k *= 2
        xs = [_bdot(y, x) for y, x in zip(ys, xs)]
    for _ in range(2):
        rs = [eye - _dot3(eye + l, x) for l, x in zip(lmats, xs)]
        xs = [x + _bdot(x, r) for x, r in zip(xs, rs)]
    return xs


def _gdn_kernel(main_ref, ab_ref, cw_ref, prev_ref, alog_ref, dtb_ref, s0_ref, g_ref, o_ref, s_ref, carry_sc,
                *, heads, group, col0, t_valid):
    c_idx = pl.program_id(1)
    c = main_ref.shape[0]
    gw = heads * HEAD_DIM
    ch = 3 * gw

    @pl.when(c_idx == 0)
    def _():
        s_ref[...] = s0_ref[...]
        carry_sc[0:SUBLANES, :] = prev_ref[...]

    x = main_ref[:, col0:col0 + ch]
    carry_sc[SUBLANES:, :] = x
    y = cw_ref[GDN_CONV - 1:GDN_CONV, :] * x
    for tap in range(1, GDN_CONV):
        y = y + cw_ref[GDN_CONV - 1 - tap:GDN_CONV - tap, :] * carry_sc[SUBLANES - tap:SUBLANES - tap + c, :]
    carry_sc[0:SUBLANES, :] = x[c - SUBLANES:, :]
    y = _silu(y)

    ab = ab_ref[...]
    lane = _iota(ab.shape, 1)
    row_ok = (c_idx * c + _iota((c, 1), 0)) < t_valid
    g_all = jnp.where(row_ok & (lane < heads), -jnp.exp(alog_ref[...]) * _softplus(ab + dtb_ref[...]), 0.0)
    beta_all = jnp.where(row_ok, jax.nn.sigmoid(ab), 0.0)
    tri = (_iota((c, c), 0) >= _iota((c, c), 1)).astype(bf16)
    gam_all = _sel_dot(tri, g_all)
    eye_l = (_iota((LANES, LANES), 0) == _iota((LANES, LANES), 1)).astype(bf16)
    gam_t = _sel_dot_nt(eye_l, gam_all)
    gnorm = g_ref[...]

    r = group * c
    ii = _iota((r, r), 0)
    jj = _iota((r, r), 1)
    same = (ii // c) == (jj // c)
    low = same & (ii >= jj)
    strict = same & (ii > jj)
    groups = [range(g * group, (g + 1) * group) for g in range(heads // group)]
    stack = lambda hs, f: jnp.concatenate([f(h) for h in hs], axis=0)
    qs, ks, vs, betas, gcols, decays, egams, kds = [], [], [], [], [], [], [], []
    for hs in groups:
        xq = stack(hs, lambda h: y[:, h * HEAD_DIM:(h + 1) * HEAD_DIM])
        xk = stack(hs, lambda h: y[:, gw + h * HEAD_DIM:gw + (h + 1) * HEAD_DIM])
        vs.append(stack(hs, lambda h: y[:, 2 * gw + h * HEAD_DIM:2 * gw + (h + 1) * HEAD_DIM]))
        qs.append(xq * lax.rsqrt(jnp.sum(xq * xq, axis=-1, keepdims=True) + EPS) * (HEAD_DIM ** -0.5))
        ks.append(xk * lax.rsqrt(jnp.sum(xk * xk, axis=-1, keepdims=True) + EPS))
        betas.append(stack(hs, lambda h: beta_all[:, heads + h:heads + h + 1]))
        gcol = stack(hs, lambda h: gam_all[:, h:h + 1])
        glast = stack(hs, lambda h: jnp.broadcast_to(gam_all[c - 1:c, h:h + 1], (c, 1)))
        grow = jnp.concatenate([gam_t[h:h + 1, :] for h in hs], axis=1)
        gcols.append(gcol)
        egams.append(jnp.exp(gcol))
        decays.append(jnp.exp(jnp.where(low, gcol - grow, -jnp.inf)))
        kds.append(ks[-1] * jnp.exp(glast - gcol))
    kks = [_bdot_nt(k, k) for k in ks]
    lmats = [jnp.where(strict, beta * kk * decay, 0.0) for beta, kk, decay in zip(betas, kks, decays)]
    tmats = _inv_unit_lower(lmats, c // INV_BLK)
    us = [_bdot(t, beta * v) for t, beta, v in zip(tmats, betas, vs)]
    ws = [_bdot(t, (beta * egam) * k) for t, beta, egam, k in zip(tmats, betas, egams, ks)]
    qks = [_bdot_nt(q, k) * decay for q, k, decay in zip(qs, ks, decays)]
    qes = [q * egam for q, egam in zip(qs, egams)]
    rows_of = lambda a: slice(a * c, (a + 1) * c)
    states = [[s_ref[h] for h in hs] for hs in groups]
    v_news = [jnp.concatenate([u[rows_of(a), :] - _bdot(w[rows_of(a), :], st[a]) for a in range(group)], axis=0)
              for u, w, st in zip(us, ws, states)]
    o_inters = [jnp.concatenate([_bdot(qe[rows_of(a), :], st[a]) for a in range(group)], axis=0)
                for qe, st in zip(qes, states)]
    os_ = [oi + _bdot(qk, vn) for oi, qk, vn in zip(o_inters, qks, v_news)]
    for hs, st, kd, vn, o in zip(groups, states, kds, v_news, os_):
        for a, h in enumerate(hs):
            s_ref[h] = jnp.exp(gam_all[c - 1:c, h:h + 1]) * st[a] + _bdot_tn(kd[rows_of(a), :], vn[rows_of(a), :])
            z = main_ref[:, col0 + ch + h * HEAD_DIM:col0 + ch + (h + 1) * HEAD_DIM]
            o_ref[:, h * HEAD_DIM:(h + 1) * HEAD_DIM] = (_rms_heads(o[rows_of(a), :], gnorm) * _silu(z)).astype(o_ref.dtype)


def _gdn_call(proj, small, cw, prev8, alog, dtb, s0, gnorm, layer, *, nseq, rows_per_seq, row0, chunk, heads,
              col0, t_valid, name):
    mainw = proj.shape[1]
    gw = heads * HEAD_DIM
    ch = 3 * gw
    nch = rows_per_seq // chunk
    blk0 = row0 // chunk
    need = 2 * chunk * mainw * 4 + 4 * heads * HEAD_DIM * HEAD_DIM * 4 + 40 * chunk * ch * 4 + (4 << 20)
    return pl.pallas_call(
        functools.partial(_gdn_kernel, heads=heads, group=1, col0=col0, t_valid=t_valid),
        grid=(nseq, nch),
        in_specs=[pl.BlockSpec((chunk, mainw), lambda b, c: (blk0 + b * nch + c, 0)),
                  pl.BlockSpec((chunk, LANES), lambda b, c: (blk0 + b * nch + c, 0)),
                  pl.BlockSpec((None, GDN_CONV, ch), lambda b, c: (layer, 0, 0)),
                  pl.BlockSpec((None, SUBLANES, ch), lambda b, c: (b, 0, 0)),
                  pl.BlockSpec((1, LANES), lambda b, c: (0, 0)),
                  pl.BlockSpec((1, LANES), lambda b, c: (0, 0)),
                  pl.BlockSpec((None, heads, HEAD_DIM, HEAD_DIM), lambda b, c: (b, 0, 0, 0)),
                  pl.BlockSpec((1, HEAD_DIM), lambda b, c: (0, 0))],
        out_specs=[pl.BlockSpec((chunk, gw), lambda b, c: (b * nch + c, 0)),
                   pl.BlockSpec((None, heads, HEAD_DIM, HEAD_DIM), lambda b, c: (b, 0, 0, 0))],
        out_shape=[jax.ShapeDtypeStruct((nseq * rows_per_seq, gw), bf16),
                   jax.ShapeDtypeStruct((nseq, heads, HEAD_DIM, HEAD_DIM), f32)],
        scratch_shapes=[pltpu.VMEM((SUBLANES + chunk, ch), f32)],
        compiler_params=_cparams(("arbitrary", "arbitrary"), need),
        name=name,
    )(proj, small, cw, prev8, alog, dtb, s0, gnorm.reshape(1, HEAD_DIM))


def _gla_kernel(tail_ref, small_ref, wg_ref, bg_ref, s0_ref, g_ref, o_ref, s_ref, *, heads, t_valid):
    c_idx = pl.program_id(1)
    c = tail_ref.shape[0]
    kw = heads * GLA_DK
    vw = heads * HEAD_DIM

    @pl.when(c_idx == 0)
    def _():
        s_ref[...] = s0_ref[...]

    row_ok = (c_idx * c + _iota((c, 1), 0)) < t_valid
    fblk = small_ref[...]
    pre = _dot3(fblk, wg_ref[...]) + bg_ref[...]
    log_a = jnp.where(row_ok, -_softplus(-pre) * (1.0 / GLA_TAU), 0.0)
    tri = (_iota((c, c), 0) >= _iota((c, c), 1)).astype(bf16)
    bc_all = _sel_dot(tri, log_a)
    gnorm = g_ref[...]
    lane = _iota((c, LANES), 1)
    first = lane < GLA_DK
    ii = _iota((c, LANES), 0)
    jj = lane % GLA_DK
    rowid = _iota((c, 1), 0)
    eye_l = _iota((LANES, LANES), 0) == _iota((LANES, LANES), 1)
    seg = (((_iota((2 * LANES, LANES), 0) % LANES) < GLA_DK) == (_iota((2 * LANES, LANES), 1) < GLA_DK)).astype(bf16)
    zpad = jnp.zeros((GLA_DK - c, LANES), f32) if c < GLA_DK else None

    def stack_pair(a):
        a0 = jnp.where(first, a, 0.0)
        a1 = jnp.where(first, 0.0, a)
        parts = [a0, a1] if zpad is None else [a0, zpad, a1, zpad]
        return jnp.concatenate(parts, axis=0)

    pairs = range(heads // 2)
    qs = [tail_ref[:, p * LANES:(p + 1) * LANES] * (GLA_DK ** -0.5) for p in pairs]
    ks = [jnp.where(row_ok, tail_ref[:, kw + p * LANES:kw + (p + 1) * LANES], 0.0) for p in pairs]
    bcs = [bc_all[:, p * LANES:(p + 1) * LANES] for p in pairs]

    def v_stack(p):
        v0 = tail_ref[:, 2 * kw + (2 * p) * HEAD_DIM:2 * kw + (2 * p + 1) * HEAD_DIM]
        v1 = tail_ref[:, 2 * kw + (2 * p + 1) * HEAD_DIM:2 * kw + (2 * p + 2) * HEAD_DIM]
        vparts = [v0, v1] if zpad is None else [v0, zpad, v1, zpad]
        return jnp.concatenate(vparts, axis=0).astype(bf16)

    v2s = [v_stack(p) for p in pairs]
    states = [s_ref[p] for p in pairs]
    qes = [q * jnp.exp(bc) for q, bc in zip(qs, bcs)]
    o0s = [_bdot(jnp.where(first, qe, 0.0), s) for qe, s in zip(qes, states)]
    o1s = [_bdot(jnp.where(first, 0.0, qe), s) for qe, s in zip(qes, states)]

    pieces = [[jnp.zeros((GLA_SUB, LANES), f32)] for _ in pairs]
    for sb in range(1, c // GLA_SUB):
        r0 = sb * GLA_SUB
        rs = slice(r0, r0 + GLA_SUB)
        q_is = [q[rs, :] * jnp.exp(bc[rs, :] - bc[r0:r0 + 1, :]) for q, bc in zip(qs, bcs)]
        k_js = [k * jnp.exp(jnp.where(rowid < r0, bc[r0:r0 + 1, :] - bc, -jnp.inf)) for k, bc in zip(ks, bcs)]
        for p in pairs:
            pieces[p].append(_bdot_nt(q_is[p], stack_pair(k_js[p])))
    att_offs = [jnp.concatenate(pc, axis=0) for pc in pieces]

    atts = [jnp.zeros((c, LANES), f32) for _ in pairs]
    for dlt in range(GLA_SUB):
        ok = (rowid % GLA_SUB) >= dlt
        prs = []
        for p in pairs:
            k_r = pltpu.roll(ks[p], dlt, 0) if dlt else ks[p]
            bc_r = pltpu.roll(bcs[p], dlt, 0) if dlt else bcs[p]
            prs.append(qs[p] * k_r * jnp.exp(jnp.where(ok, bcs[p] - bc_r, -jnp.inf)))
        splits = [_split2(pr) for pr in prs]
        sums = [jnp.dot(jnp.concatenate([hi, lo], axis=1), seg, preferred_element_type=f32) for hi, lo in splits]
        atts = [jnp.where(jj == ii - dlt, sm, att) for sm, att in zip(sums, atts)]
    atts = [att + off for att, off in zip(atts, att_offs)]

    o0s = [o0 + _bdot(jnp.where(first, att, 0.0), v2) for o0, att, v2 in zip(o0s, atts, v2s)]
    o1s = [o1 + _bdot(jnp.where(first, 0.0, att), v2) for o1, att, v2 in zip(o1s, atts, v2s)]
    bls = [bc[c - 1:c, :] for bc in bcs]
    ebl_cols = [jnp.sum(jnp.where(eye_l, jnp.exp(bl), 0.0), axis=-1, keepdims=True) for bl in bls]
    kds = [stack_pair(k * jnp.exp(bl - bc)) for k, bl, bc in zip(ks, bls, bcs)]
    for p in pairs:
        s_ref[p] = ebl_cols[p] * states[p] + _bdot_tn(kds[p], v2s[p])
        for a, o in ((0, o0s[p]), (1, o1s[p])):
            h = 2 * p + a
            r_ = tail_ref[:, 2 * kw + vw + h * HEAD_DIM:2 * kw + vw + (h + 1) * HEAD_DIM]
            o_ref[:, h * HEAD_DIM:(h + 1) * HEAD_DIM] = (_rms_heads(o, gnorm) * _silu(r_)).astype(o_ref.dtype)


def _gla_call(tail, small, wg_pad, bg, s0, gnorm, *, nseq, rows_per_seq, row0, chunk, heads, t_valid, name):
    tailw = tail.shape[1]
    kw = heads * GLA_DK
    vw = heads * HEAD_DIM
    nch = rows_per_seq // chunk
    blk0 = row0 // chunk
    assert heads % 2 == 0 and chunk <= GLA_DK and chunk % GLA_SUB == 0
    s0 = s0.reshape(nseq, heads // 2, 2 * GLA_DK, HEAD_DIM)
    need = 2 * chunk * tailw * 4 + 4 * heads * GLA_DK * HEAD_DIM * 4 + 2 * LANES * kw * 4 + 40 * chunk * tailw * 4
    o, s_fin = pl.pallas_call(
        functools.partial(_gla_kernel, heads=heads, t_valid=t_valid),
        grid=(nseq, nch),
        in_specs=[pl.BlockSpec((chunk, tailw), lambda b, c: (blk0 + b * nch + c, 0)),
                  pl.BlockSpec((chunk, LANES), lambda b, c: (blk0 + b * nch + c, 0)),
                  pl.BlockSpec((LANES, kw), lambda b, c: (0, 0)),
                  pl.BlockSpec((1, kw), lambda b, c: (0, 0)),
                  pl.BlockSpec((None, heads // 2, 2 * GLA_DK, HEAD_DIM), lambda b, c: (b, 0, 0, 0)),
                  pl.BlockSpec((1, HEAD_DIM), lambda b, c: (0, 0))],
        out_specs=[pl.BlockSpec((chunk, vw), lambda b, c: (b * nch + c, 0)),
                   pl.BlockSpec((None, heads // 2, 2 * GLA_DK, HEAD_DIM), lambda b, c: (b, 0, 0, 0))],
        out_shape=[jax.ShapeDtypeStruct((nseq * rows_per_seq, vw), bf16),
                   jax.ShapeDtypeStruct((nseq, heads // 2, 2 * GLA_DK, HEAD_DIM), f32)],
        compiler_params=_cparams(("arbitrary", "arbitrary"), need),
        name=name,
    )(tail, small, wg_pad, bg.reshape(1, kw), s0, gnorm.reshape(1, HEAD_DIM))
    return o, s_fin.reshape(nseq, heads, GLA_DK, HEAD_DIM)


def kernel(x_prompt, x_sample, c_prompt, c_sample, cache_k, cache_v, page_table, state_gdn, state_gdn_conv, state_gla, ada_w, ada_b, norm_ffn_a, ffn_a_wg, ffn_a_wu, ffn_a_wd, norm_mix, w_in, moba_norm, gdn_conv_w, gdn_a_log, gdn_dt_bias, gdn_norm, gla_w_gate, gla_b_gate, gla_norm, w_out, norm_ffn_b, ffn_b_wg, ffn_b_wu, ffn_b_wd, final_norm):
    b, s_len, d = x_prompt.shape
    db, t_dec, _ = x_sample.shape
    depth = ada_w.shape[0]
    n_pages = page_table.shape[1]
    past_len = n_pages * PAGE_SIZE

    n_heads = d // HEAD_DIM
    moba_h = n_heads // 4
    kvh = moba_h // 2
    gdn_h = (3 * n_heads) // 8
    gla_h = n_heads - moba_h - gdn_h
    qw, kvw = moba_h * HEAD_DIM, kvh * HEAD_DIM
    gdn_w, gla_kw, gla_w = gdn_h * HEAD_DIM, gla_h * GLA_DK, gla_h * HEAD_DIM
    main_w = qw + 2 * kvw + 4 * gdn_w
    small0 = main_w
    gla0 = main_w + 2 * gdn_h
    f0 = gla0 + 2 * gla_kw + 2 * gla_w
    tail_main = 2 * gla_kw + 2 * gla_w
    assert w_in.shape[2] == f0 + GLA_GATE_RANK
    assert s_len % TM == 0 and s_len % CHUNK == 0 and s_len >= MOBA_TOPK * MOBA_BLOCK
    assert (db * SUB) % TM == 0 and b + db <= 16
    assert GDN_CONV - 1 <= t_dec <= QROWS and past_len % MOBA_BLOCK == 0
    assert 2 * gdn_h + GLA_GATE_RANK <= LANES and tail_main % LANES == 0

    bs = b * s_len
    m_tot = bs + db * SUB
    n_tiles_p = bs // TM

    def pack_rows(p_rows, s_rows):
        s_pad = jnp.pad(s_rows, ((0, 0), (0, SUB - s_rows.shape[1]), (0, 0)))
        return jnp.concatenate([p_rows, s_pad.reshape(db * SUB, s_rows.shape[2])], axis=0)

    def sample_rows(a):
        return a[bs:].reshape(db, SUB, a.shape[1])[:, :t_dec]

    x = _pack_call(x_prompt.reshape(bs, d),
                   jnp.pad(x_sample, ((0, 0), (0, SUB - t_dec), (0, 0))).reshape(db * SUB, d))

    c16 = jnp.concatenate([c_prompt, c_sample, jnp.zeros((16 - b - db, d), f32)], axis=0)
    mod = _mod_call(c16, ada_w, ada_b)

    half = ROPE_DIM // 2
    inv_freq = ROPE_THETA ** (-jnp.arange(half, dtype=f32) / half)
    pos_s = past_len + jnp.minimum(jnp.arange(SUB), t_dec - 1)
    pos = jnp.concatenate([jnp.tile(jnp.arange(s_len), b), jnp.tile(pos_s, db)]).astype(f32)
    ang = pos[:, None] * inv_freq[None, :]
    ones = jnp.ones((m_tot, HEAD_DIM - ROPE_DIM), f32)
    cos_t = jnp.concatenate([jnp.cos(ang), jnp.cos(ang), ones], axis=1)
    sin_t = jnp.concatenate([-jnp.sin(ang), jnp.sin(ang), 0.0 * ones], axis=1)

    w_in_t = jnp.swapaxes(w_in, 1, 2)
    w_small_t = jnp.concatenate([w_in_t[:, small0:gla0], w_in_t[:, f0:],
                                 jnp.zeros((depth, LANES - 2 * gdn_h - GLA_GATE_RANK, d), f32)], axis=1)
    zeros_prev = jnp.zeros((b, SUBLANES, 3 * gdn_w), f32)

    outs = {k: [] for k in ("kp", "vp", "ks", "vs", "gp", "gs", "cp", "cs", "lp", "ls")}
    for l in range(depth):
        mod8 = jnp.concatenate([jnp.repeat(mod[l, :b], (s_len // TM) * (TM // SUB), axis=0), mod[l, b:b + db]], axis=0)

        h = _premod_call(x, norm_ffn_a[l], mod8, 0, 1)
        a = _mm_up_call(h, ffn_a_wg, ffn_a_wu, l)
        x = _mm_res_call(a, ffn_a_wd, l, x, mod8, 2, 0.5, "ffn_a_down")

        h = _premod_call(x, norm_mix[l], mod8, 3, 4)
        proj = _mm_plain_call(h, w_in_t, l, 0, main_w, "w_in_main")
        tail = _mm_plain_call(h, w_in_t, l, gla0, tail_main, "w_in_tail")
        small = _mm_plain_call(h, w_small_t, l, 0, LANES, "w_in_small")

        q_rot, k_rot = _rope_call(proj, cos_t, sin_t, qw, kvw)
        v_col0 = qw + kvw
        om_p = _moba_prompt_call(q_rot, k_rot, proj, moba_norm[l], b, s_len, kvh, v_col0)
        om_s = _moba_sample_call(page_table, q_rot, k_rot, proj, cache_k, cache_v, moba_norm[l], l,
                                 db, bs, kvh, v_col0, t_dec)
        om_s = om_s.reshape(db, QROWS, qw)[:, :t_dec].astype(bf16)

        col0 = qw + 2 * kvw
        alog = jnp.zeros((1, LANES), f32).at[0, :gdn_h].set(gdn_a_log[l])
        dtb = jnp.zeros((1, LANES), f32).at[0, :gdn_h].set(gdn_dt_bias[l])
        od_p, gp = _gdn_call(proj, small, gdn_conv_w, zeros_prev, alog, dtb,
                             jnp.zeros((b, gdn_h, HEAD_DIM, HEAD_DIM), f32), gdn_norm[l], l,
                             nseq=b, rows_per_seq=s_len, row0=0, chunk=CHUNK, heads=gdn_h, col0=col0,
                             t_valid=s_len, name="gdn_prompt")
        prev_s = jnp.pad(state_gdn_conv[l], ((0, 0), (SUBLANES - (GDN_CONV - 1), 0), (0, 0)))
        od_s, gs = _gdn_call(proj, small, gdn_conv_w, prev_s, alog, dtb, state_gdn[l], gdn_norm[l], l,
                             nseq=db, rows_per_seq=SUB, row0=bs, chunk=SUB, heads=gdn_h, col0=col0,
                             t_valid=t_dec, name="gdn_sample")

        wg_pad = jnp.zeros((LANES, gla_kw), f32).at[2 * gdn_h:2 * gdn_h + GLA_GATE_RANK].set(gla_w_gate[l])
        ol_p, lp = _gla_call(tail, small, wg_pad, gla_b_gate[l], jnp.zeros((b, gla_h, GLA_DK, HEAD_DIM), f32),
                             gla_norm[l], nseq=b, rows_per_seq=s_len, row0=0, chunk=CHUNK, heads=gla_h,
                             t_valid=s_len, name="gla_prompt")
        ol_s, ls = _gla_call(tail, small, wg_pad, gla_b_gate[l], state_gla[l], gla_norm[l],
                             nseq=db, rows_per_seq=SUB, row0=bs, chunk=SUB, heads=gla_h,
                             t_valid=t_dec, name="gla_sample")

        o_p = jnp.concatenate([om_p, od_p, ol_p], axis=1)
        o_s = jnp.concatenate([om_s, od_s.reshape(db, SUB, gdn_w)[:, :t_dec], ol_s.reshape(db, SUB, gla_w)[:, :t_dec]], axis=2)
        o_mix = pack_rows(o_p, o_s)
        x = _mm_res_call([o_mix], w_out, l, x, mod8, 5, 1.0, "w_out")

        h = _premod_call(x, norm_ffn_b[l], mod8, 6, 7)
        a = _mm_up_call(h, ffn_b_wg, ffn_b_wu, l)
        x = _mm_res_call(a, ffn_b_wd, l, x, mod8, 8, 0.5, "ffn_b_down")

        conv_cols = slice(col0, col0 + 3 * gdn_w)
        proj_s = proj[bs:].reshape(db, SUB, main_w)
        outs["kp"].append(k_rot[:bs].reshape(b, s_len, kvh, HEAD_DIM))
        outs["vp"].append(proj[:bs, v_col0:v_col0 + kvw].reshape(b, s_len, kvh, HEAD_DIM))
        outs["ks"].append(sample_rows(k_rot).reshape(db, t_dec, kvh, HEAD_DIM))
        outs["vs"].append(proj_s[:, :t_dec, v_col0:v_col0 + kvw].reshape(db, t_dec, kvh, HEAD_DIM))
        outs["gp"].append(gp)
        outs["gs"].append(gs)
        outs["cp"].append(jnp.stack([proj[(i + 1) * s_len - (GDN_CONV - 1):(i + 1) * s_len, conv_cols] for i in range(b)]))
        outs["cs"].append(proj_s[:, t_dec - (GDN_CONV - 1):t_dec, conv_cols])
        outs["lp"].append(lp)
        outs["ls"].append(ls)

    y_p = _rms_call(x, final_norm, 0, bs)
    y_s = _rms_call(x, final_norm, bs, db * SUB)
    st = {k: jnp.stack(v) for k, v in outs.items()}
    return (y_p.reshape(b, s_len, d), y_s.reshape(db, SUB, d)[:, :t_dec],
            st["kp"], st["vp"], st["ks"], st["vs"], st["gp"], st["gs"], st["cp"], st["cs"], st["lp"], st["ls"])
```
